```python
import math
import jax
import jax.numpy as jnp
from jax import lax
import numpy as np

D_MODEL = 1024
BATCH = 8
SEQ = 4096
DEPTH = 1

A_HEADS = 16
A_HEAD_DIM = 64
A_WIDTH = A_HEADS * A_HEAD_DIM
DILATED_PATTERNS = ((128, 1), (512, 4), (2048, 16))
N_BUCKETS = 32
MAX_EXACT = N_BUCKETS // 2
MAX_DISTANCE = 2048
M_HEADS = 4
M_WIDTH = 2 * D_MODEL
M_HEAD_DIM = M_WIDTH // M_HEADS
CONV_K = 4
QKV_BLOCK = 4
CHUNK = 64
EPS = 1e-6
N_IN = 4 * A_WIDTH + 3 * M_WIDTH + 2 * D_MODEL
SPLIT_POINTS = (A_WIDTH, 2 * A_WIDTH, 3 * A_WIDTH, 4 * A_WIDTH, 4 * A_WIDTH + M_WIDTH, 4 * A_WIDTH + 2 * M_WIDTH, 4 * A_WIDTH + 3 * M_WIDTH)

kernel_name = 'hybrid_dilated_attn_mlstm_gated'


def rmsnorm(x, g):
    xf = x.astype(jnp.float32)
    y = xf * lax.rsqrt(jnp.mean(xf * xf, axis=-1, keepdims=True) + EPS)
    return (y * g.astype(jnp.float32)).astype(x.dtype)


def t5_bucket(dist):
    large = MAX_EXACT + (jnp.log(jnp.maximum(dist, MAX_EXACT).astype(jnp.float32) / MAX_EXACT)
                         / math.log(MAX_DISTANCE / MAX_EXACT) * (N_BUCKETS - MAX_EXACT)).astype(jnp.int32)
    return jnp.where(dist < MAX_EXACT, dist, jnp.minimum(large, N_BUCKETS - 1))


def dilated_band_attention(q, k, v, rel_bias, window, dilation):
    B, S, H, dh = q.shape
    band = window // dilation
    span = band * dilation
    s_pad = -(-S // span) * span
    L = s_pad // dilation
    nb = L // band

    def to_blocks(t):
        t = jnp.pad(t, ((0, 0), (0, s_pad - S), (0, 0), (0, 0)))
        t = t.reshape(B, L, dilation, H, dh).transpose(0, 2, 3, 1, 4)
        return t.reshape(B, dilation, H, nb, band, dh)

    def with_prev(t):
        prev = jnp.pad(t, ((0, 0), (0, 0), (0, 0), (1, 0), (0, 0), (0, 0)))[:, :, :, :-1]
        return jnp.concatenate([prev, t], axis=4)

    qb = to_blocks(q)
    kc = with_prev(to_blocks(k))
    vc = with_prev(to_blocks(v))
    logits = jnp.einsum('brhnqc,brhnkc->brhnqk', qb, kc, preferred_element_type=jnp.float32) * (dh ** -0.5)
    qi = jnp.arange(band)[:, None]
    kj = jnp.arange(2 * band)[None, :]
    delta = qi + band - kj
    bias = rel_bias.astype(jnp.float32)[t5_bucket(jnp.clip(delta, 0, band) * dilation)]
    bias = jnp.transpose(bias, (2, 0, 1))[:, None]
    key_pos = jnp.arange(nb)[:, None, None] * band + kj[None] - band
    valid = (delta >= 0) & (delta <= band) & (key_pos >= 0)
    logits = jnp.where(valid, logits + bias, -jnp.inf)
    m = jnp.max(logits, axis=-1)
    p = jnp.exp(logits - m[..., None])
    s = jnp.sum(p, axis=-1)
    o = jnp.einsum('brhnqk,brhnkc->brhnqc', p, vc.astype(jnp.float32)) / s[..., None]

    def from_blocks(t):
        t = t.reshape((B, dilation, H, L) + t.shape[5:])
        t = jnp.moveaxis(t, 3, 1)
        return t.reshape((B, s_pad, H) + t.shape[4:])[:, :S]

    return from_blocks(o), from_blocks(m), from_blocks(s)


def dilated_attention(q, k, v, rel_bias):
    B, S, H, dh = q.shape
    outs = [dilated_band_attention(q, k, v, rel_bias, w, d) for (w, d) in DILATED_PATTERNS]
    o = jnp.stack([t[0] for t in outs])
    m = jnp.stack([t[1] for t in outs])
    s = jnp.stack([t[2] for t in outs])
    wgt = jnp.exp(m - jnp.max(m, axis=0, keepdims=True)) * s
    out = jnp.sum(wgt[..., None] * o, axis=0) / jnp.sum(wgt, axis=0)[..., None]
    return out.reshape(B, S, H * dh)


def causal_depthwise_conv(x, w, b):
    S = x.shape[1]
    xp = jnp.pad(x, ((0, 0), (CONV_K - 1, 0), (0, 0)))
    y = b
    for tap in range(CONV_K):
        y = y + xp[:, tap:tap + S] * w[tap]
    return y


def block_diag_proj(x, w):
    B, S, C = x.shape
    return jnp.einsum('bsgi,gio->bsgo', x.reshape(B, S, C // QKV_BLOCK, QKV_BLOCK), w).reshape(B, S, C)


def mlstm_chunkwise(q, k, v, li, lf):
    B, S, H, dh = q.shape
    nc = S // CHUNK

    def chunks(t):
        t = t.astype(jnp.float32).reshape((B, nc, CHUNK, H) + t.shape[3:])
        return jnp.moveaxis(jnp.moveaxis(t, 1, 0), 3, 2)

    qc, kc, vc = chunks(q), chunks(k) * (dh ** -0.5), chunks(v)
    lic, lfc = chunks(li), chunks(lf)
    tril = jnp.tril(jnp.ones((CHUNK, CHUNK), dtype=bool))

    def step(carry, xs):
        C, n, m = carry
        qt, kt, vt, it, ft = xs
        b = jnp.cumsum(ft, axis=-1)
        g = b[..., -1]
        D = jnp.where(tril, b[..., :, None] - b[..., None, :] + it[..., None, :], -jnp.inf)
        inter = b + m[..., None]
        m_t = jnp.maximum(inter, jnp.max(D, axis=-1))
        qk = jnp.einsum('bhtd,bhsd->bhts', qt, kt) * jnp.exp(D - m_t[..., None])
        w_inter = jnp.exp(inter - m_t)
        num = w_inter[..., None] * jnp.einsum('bhtd,bhde->bhte', qt, C) + jnp.einsum('bhts,bhse->bhte', qk, vt)
        den = w_inter * jnp.einsum('bhtd,bhd->bht', qt, n) + jnp.sum(qk, axis=-1)
        h = num / jnp.maximum(jnp.abs(den), jnp.exp(-m_t))[..., None]
        to_end = g[..., None] - b + it
        m_new = jnp.maximum(g + m, jnp.max(to_end, axis=-1))
        w_s = jnp.exp(to_end - m_new[..., None])
        decay = jnp.exp(g + m - m_new)
        C_new = decay[..., None, None] * C + jnp.einsum('bhsd,bhse->bhde', kt * w_s[..., None], vt)
        n_new = decay[..., None] * n + jnp.einsum('bhs,bhsd->bhd', w_s, kt)
        return (C_new, n_new, m_new), h

    init = (jnp.zeros((B, H, dh, dh), jnp.float32), jnp.zeros((B, H, dh), jnp.float32), jnp.zeros((B, H), jnp.float32))
    _, h = lax.scan(step, init, (qc, kc, vc, lic, lfc))
    return jnp.transpose(h, (1, 0, 3, 2, 4)).reshape(B, S, H, dh)


def mlstm_branch(x_m, z_m, o_m, conv_w, conv_b, wq, wk, wv, w_if, b_if, head_norm_g, skip):
    B, S, _ = x_m.shape
    x_c = jax.nn.silu(causal_depthwise_conv(x_m, conv_w, conv_b))
    q = block_diag_proj(x_c, wq)
    k = block_diag_proj(x_c, wk)
    v = block_diag_proj(x_m, wv)
    gate_pre = (jnp.concatenate([q, k, v], axis=-1) @ w_if + b_if).astype(jnp.float32)
    li = gate_pre[..., :M_HEADS]
    lf = jax.nn.log_sigmoid(gate_pre[..., M_HEADS:])
    heads = lambda t: t.reshape(B, S, M_HEADS, M_HEAD_DIM)
    h = mlstm_chunkwise(heads(q), heads(k), heads(v), li, lf)
    h = jax.nn.sigmoid(heads(o_m).astype(jnp.float32)) * h
    mu = jnp.mean(h, axis=-1, keepdims=True)
    var = jnp.mean(jnp.square(h - mu), axis=-1, keepdims=True)
    h = (h - mu) * lax.rsqrt(var + EPS) * head_norm_g.astype(jnp.float32).reshape(M_HEADS, M_HEAD_DIM)
    h = h.reshape(B, S, M_WIDTH) + skip * x_c
    return h * jax.nn.silu(z_m)


def setup_inputs(seed: int = 0) -> dict:
    key = jax.random.key(seed)
    ks = jax.random.split(key, 20)
    nrm = lambda k, shape, scale: jax.random.normal(k, shape, jnp.float32) * scale
    n_blocks = M_WIDTH // QKV_BLOCK
    x = nrm(ks[0], (BATCH, SEQ, D_MODEL), 1.0)
    norm_in_g = 1.0 + nrm(ks[1], (DEPTH, D_MODEL), 0.02)
    w_in = nrm(ks[2], (DEPTH, D_MODEL, N_IN), D_MODEL ** -0.5)
    gate_b = nrm(ks[3], (DEPTH, 2 * D_MODEL), 0.01)
    conv_w = nrm(ks[4], (DEPTH, CONV_K, M_WIDTH), CONV_K ** -0.5)
    conv_b = nrm(ks[5], (DEPTH, M_WIDTH), 0.01)
    wq_m = nrm(ks[6], (DEPTH, n_blocks, QKV_BLOCK, QKV_BLOCK), QKV_BLOCK ** -0.5)
    wk_m = nrm(ks[7], (DEPTH, n_blocks, QKV_BLOCK, QKV_BLOCK), QKV_BLOCK ** -0.5)
    wv_m = nrm(ks[8], (DEPTH, n_blocks, QKV_BLOCK, QKV_BLOCK), QKV_BLOCK ** -0.5)
    w_if = nrm(ks[9], (DEPTH, 3 * M_WIDTH, 2 * M_HEADS), (3 * M_WIDTH) ** -0.5)
    b_if = jnp.concatenate([nrm(ks[10], (DEPTH, M_HEADS), 0.1),
                            jnp.linspace(3.0, 6.0, M_HEADS)[None] + nrm(ks[11], (DEPTH, M_HEADS), 0.01)], axis=-1)
    head_norm_g = 1.0 + nrm(ks[12], (DEPTH, M_WIDTH), 0.02)
    skip_m = 1.0 + nrm(ks[13], (DEPTH, M_WIDTH), 0.02)
    w_pa = nrm(ks[14], (DEPTH, A_WIDTH, D_MODEL), A_WIDTH ** -0.5)
    w_pb = nrm(ks[15], (DEPTH, M_WIDTH, D_MODEL), M_WIDTH ** -0.5)
    w_out = nrm(ks[16], (DEPTH, D_MODEL, D_MODEL), D_MODEL ** -0.5)
    rel_bias = nrm(ks[17], (N_BUCKETS, A_HEADS), 0.3)
    norm_out_g = 1.0 + nrm(ks[18], (D_MODEL,), 0.02)
    return {'x': x, 'norm_in_g': norm_in_g, 'w_in': w_in, 'gate_b': gate_b, 'conv_w': conv_w, 'conv_b': conv_b,
            'wq_m': wq_m, 'wk_m': wk_m, 'wv_m': wv_m, 'w_if': w_if, 'b_if': b_if, 'head_norm_g': head_norm_g,
            'skip_m': skip_m, 'w_pa': w_pa, 'w_pb': w_pb, 'w_out': w_out, 'rel_bias': rel_bias, 'norm_out_g': norm_out_g}


def reference(x, norm_in_g, w_in, gate_b, conv_w, conv_b, wq_m, wk_m, wv_m, w_if, b_if, head_norm_g,
              skip_m, w_pa, w_pb, w_out, rel_bias, norm_out_g):
    B, S, _ = x.shape
    h = x
    for layer in range(DEPTH):
        xn = rmsnorm(h, norm_in_g[layer])
        proj = xn @ w_in[layer]
        q_a, k_a, v_a, z_a, x_m, z_m, o_m, gates = jnp.split(proj, SPLIT_POINTS, axis=-1)
        to_heads = lambda t: t.reshape(B, S, A_HEADS, A_HEAD_DIM)
        y_a = dilated_attention(to_heads(q_a), to_heads(k_a), to_heads(v_a), rel_bias)
        y_a = (y_a * jax.nn.silu(z_a)) @ w_pa[layer]
        y_m = mlstm_branch(x_m, z_m, o_m, conv_w[layer], conv_b[layer], wq_m[layer], wk_m[layer], wv_m[layer],
                           w_if[layer], b_if[layer], head_norm_g[layer], skip_m[layer]) @ w_pb[layer]
        g = jax.nn.sigmoid(gates.astype(jnp.float32) + gate_b[layer])
        g_a, g_m = jnp.split(g, 2, axis=-1)
        merged = g_a * y_a + g_m * y_m
        h = h + (merged @ w_out[layer]).astype(h.dtype)
    return rmsnorm(h, norm_out_g)
```

```python
import functools
import math

import jax
import jax.numpy as jnp
from jax import lax
from jax.experimental import pallas as pl
from jax.experimental.pallas import tpu as pltpu

F32 = jnp.float32
BF16 = jnp.bfloat16

A_HEADS = 16
A_HEAD_DIM = 64
DILATED_PATTERNS = ((128, 1), (512, 4), (2048, 16))
MAX_DISTANCE = 2048
M_HEADS = 4
EPS = 1e-6
MASKED = -1e30

LANES = 128
MXU_DIM = 256
VMEM_LIMIT_BYTES = 56 * 1024 * 1024

MLSTM_CHUNK = 256


def _cparams(n_axes):
    return pltpu.CompilerParams(dimension_semantics=("arbitrary",) * n_axes,
                                vmem_limit_bytes=VMEM_LIMIT_BYTES)


def _norm_matmul_kernel(x_ref, g_ref, w_ref, o_ref, xn_ref):
    @pl.when(pl.program_id(1) == 0)
    def _():
        xf = x_ref[...]
        ms = jnp.mean(xf * xf, axis=-1, keepdims=True)
        xn_ref[...] = (xf * lax.rsqrt(ms + EPS) * g_ref[...]).astype(BF16)

    o_ref[...] = jnp.dot(xn_ref[...], w_ref[...], preferred_element_type=F32).astype(o_ref.dtype)


def _norm_matmul(x2, g, w, out_dtype, tm, tn):
    n, d = x2.shape
    nc = w.shape[1]
    return pl.pallas_call(
        _norm_matmul_kernel,
        grid=(n // tm, nc // tn),
        in_specs=[pl.BlockSpec((tm, d), lambda i, j: (i, 0)),
                  pl.BlockSpec((1, d), lambda i, j: (0, 0)),
                  pl.BlockSpec((d, tn), lambda i, j: (0, j))],
        out_specs=pl.BlockSpec((tm, tn), lambda i, j: (i, j)),
        out_shape=jax.ShapeDtypeStruct((n, nc), out_dtype),
        scratch_shapes=[pltpu.VMEM((tm, d), BF16)],
        compiler_params=_cparams(2),
        name="norm_inproj",
    )(x2, g, w)


def _t5_bucket(dist, n_buckets):
    max_exact = n_buckets // 2
    large = max_exact + (jnp.log(jnp.maximum(dist, max_exact).astype(F32) / max_exact)
                         / math.log(MAX_DISTANCE / max_exact) * (n_buckets - max_exact)).astype(jnp.int32)
    return jnp.where(dist < max_exact, dist, jnp.minimum(large, n_buckets - 1))


def _band_bias(rel_bias, band, dilation):
    qi = jnp.arange(band)[:, None]
    kj = jnp.arange(2 * band)[None, :]
    delta = qi + band - kj
    bias = rel_bias.astype(F32)[_t5_bucket(jnp.clip(delta, 0, band) * dilation, rel_bias.shape[0])]
    bias = jnp.transpose(bias, (2, 0, 1))
    valid = (delta >= 0) & (delta <= band)
    regular = jnp.where(valid[None], bias, MASKED)
    first = jnp.where((valid & (kj >= band))[None], bias, MASKED)
    return jnp.stack([first, regular])


def _attn_kernel(q_ref, kp_ref, kc_ref, vp_ref, vc_ref, bias_ref, o_ref, lse_ref, *, band, scale):
    variant = jnp.minimum(pl.program_id(2), 1)
    lane = lax.broadcasted_iota(jnp.int32, (band, LANES), 1)
    first_head = lane < A_HEAD_DIM
    stat = jnp.zeros((band, LANES), F32)
    heads_per_tile = LANES // A_HEAD_DIM
    for j in range(A_HEADS // heads_per_tile):
        sl = slice(LANES * j, LANES * (j + 1))
        q2 = q_ref[0, :, sl] * jnp.asarray(scale, BF16)
        k2 = jnp.concatenate([kp_ref[0, :, sl], kc_ref[0, :, sl]], axis=0)
        v2 = jnp.concatenate([vp_ref[0, :, sl], vc_ref[0, :, sl]], axis=0)
        outs = []
        for hh in range(heads_per_tile):
            h = heads_per_tile * j + hh
            mine = first_head if hh == 0 else jnp.logical_not(first_head)
            qm = jnp.where(mine, q2, jnp.zeros_like(q2))
            s = lax.dot_general(qm, k2, (((1,), (1,)), ((), ())), preferred_element_type=F32)
            s = s + bias_ref[variant, h]
            m = jnp.max(s, axis=1, keepdims=True)
            p = jnp.exp(s - m)
            l = jnp.sum(p, axis=1, keepdims=True)
            pv = jnp.dot(p.astype(BF16), v2, preferred_element_type=F32)
            outs.append(pv / l)
            stat = jnp.where(lane == h, m + jnp.log(l), stat)
        o_ref[0, :, sl] = jnp.where(first_head, outs[0], outs[1])
    lse_ref[0] = stat


def _attn_pattern(qkv, bias, window, dilation):
    b_sz, s_len, three_aw = qkv.shape
    aw = three_aw // 3
    band = window // dilation
    assert s_len % window == 0 and band % LANES == 0
    seq_l = s_len // dilation
    nb = seq_l // band
    qv = qkv.reshape(b_sz, seq_l, dilation * three_aw)

    def blk(which, prev):
        if prev:
            return pl.BlockSpec((1, band, aw), lambda b, r, n: (b, jnp.maximum(n - 1, 0), 3 * r + which))
        return pl.BlockSpec((1, band, aw), lambda b, r, n: (b, n, 3 * r + which))

    o, lse = pl.pallas_call(
        functools.partial(_attn_kernel, band=band, scale=A_HEAD_DIM ** -0.5),
        grid=(b_sz, dilation, nb),
        in_specs=[blk(0, False), blk(1, True), blk(1, False), blk(2, True), blk(2, False),
                  pl.BlockSpec((2, A_HEADS, band, 2 * band), lambda b, r, n: (0, 0, 0, 0))],
        out_specs=[pl.BlockSpec((1, band, aw), lambda b, r, n: (b, n, r)),
                   pl.BlockSpec((1, band, LANES), lambda b, r, n: (b, n, r))],
        out_shape=[jax.ShapeDtypeStruct((b_sz, seq_l, dilation * aw), F32),
                   jax.ShapeDtypeStruct((b_sz, seq_l, dilation * LANES), F32)],
        compiler_params=_cparams(3),
        name=f"attn_d{dilation}",
    )(qv, qv, qv, qv, qv, bias)
    return o.reshape(b_sz * s_len, aw), lse.reshape(b_sz * s_len, LANES)


def _block_diag_tiles(w):
    nblk, qb, _ = w.shape
    per_tile = MXU_DIM // qb
    wt = w.reshape(nblk // per_tile, per_tile, qb, qb)
    eye = jnp.eye(per_tile, dtype=w.dtype)
    t = wt[:, :, :, None, :] * eye[None, :, None, :, None]
    return t.reshape(nblk // per_tile, MXU_DIM, MXU_DIM)


def _mlstm_front_kernel(x_ref, halo_ref, cw_ref, cb_ref, wq_ref, wk_ref, wv_ref, wif_ref, bif_ref,
                        q_ref, k_ref, v_ref, xc_ref, gate_ref, xe_ref, *, tiles_per_seq, k_scale):
    tm, width = x_ref.shape
    taps = cw_ref.shape[0]
    pad = halo_ref.shape[0]
    first = (pl.program_id(0) % tiles_per_seq) == 0
    xm = x_ref[...]
    xe_ref[0:pad, :] = jnp.where(first, jnp.zeros_like(halo_ref[...]), halo_ref[...])
    xe_ref[pad:pad + tm, :] = xm
    y = cb_ref[...] + xm * cw_ref[taps - 1:taps, :]
    for back in range(1, taps):
        y = y + xe_ref[pad - back:pad - back + tm, :] * cw_ref[taps - 1 - back:taps - back, :]
    xc = y * jax.nn.sigmoid(y)
    xc_ref[...] = xc
    xcb = xc.astype(BF16)
    xmb = xm.astype(BF16)
    n_tiles = width // MXU_DIM
    gate = jnp.zeros((tm, LANES), F32) + bif_ref[...]
    for j in range(n_tiles):
        sl = slice(MXU_DIM * j, MXU_DIM * (j + 1))
        qj = jnp.dot(xcb[:, sl], wq_ref[j], preferred_element_type=F32)
        kj = jnp.dot(xcb[:, sl], wk_ref[j], preferred_element_type=F32)
        vj = jnp.dot(xmb[:, sl], wv_ref[j], preferred_element_type=F32)
        qb, kb, vb = qj.astype(BF16), kj.astype(BF16), vj.astype(BF16)
        q_ref[:, sl] = qb
        k_ref[:, sl] = (kj * k_scale).astype(BF16)
        v_ref[:, sl] = vb
        for part, val in enumerate((qb, kb, vb)):
            rows = slice(part * width + MXU_DIM * j, part * width + MXU_DIM * (j + 1))
            gate = gate + jnp.dot(val, wif_ref[rows, :], preferred_element_type=F32)
    gate_ref[...] = gate


def _mlstm_front(rest2, conv_w, conv_b, wq_t, wk_t, wv_t, wif_pad, bif_pad, s_len, width, tm):
    n = rest2.shape[0]
    halo = 8
    assert conv_w.shape[0] - 1 <= halo
    const = lambda *shape: pl.BlockSpec(shape, lambda i: (0,) * len(shape))
    tok = lambda w: pl.BlockSpec((tm, w), lambda i: (i, 0))
    return pl.pallas_call(
        functools.partial(_mlstm_front_kernel, tiles_per_seq=s_len // tm,
                          k_scale=(width // M_HEADS) ** -0.5),
        grid=(n // tm,),
        in_specs=[tok(width),
                  pl.BlockSpec((halo, width), lambda i: (jnp.maximum(i * (tm // halo) - 1, 0), 0)),
                  const(*conv_w.shape), const(1, width),
                  const(*wq_t.shape), const(*wk_t.shape), const(*wv_t.shape),
                  const(*wif_pad.shape), const(1, LANES)],
        out_specs=[tok(width), tok(width), tok(width), tok(width), tok(LANES)],
        out_shape=[jax.ShapeDtypeStruct((n, width), BF16)] * 3
        + [jax.ShapeDtypeStruct((n, width), F32), jax.ShapeDtypeStruct((n, LANES), F32)],
        scratch_shapes=[pltpu.VMEM((tm + halo, width), F32)],
        compiler_params=_cparams(1),
        name="mlstm_front",
    )(rest2, rest2, conv_w, conv_b, wq_t, wk_t, wv_t, wif_pad, bif_pad)


def _split3(x):
    hi = x.astype(BF16)
    r1 = x - hi.astype(F32)
    mid = r1.astype(BF16)
    lo = (r1 - mid.astype(F32)).astype(BF16)
    return hi, mid, lo


def _mlstm_gates_kernel(g_ref, o_ref):
    g = g_ref[0]
    lc = g.shape[1]
    lf = jnp.minimum(g, 0.0) - jnp.log1p(jnp.exp(-jnp.abs(g)))
    upper = (lax.broadcasted_iota(jnp.int32, (lc, lc), 0)
             <= lax.broadcasted_iota(jnp.int32, (lc, lc), 1)).astype(BF16)
    csum = sum(jnp.dot(part, upper, preferred_element_type=F32) for part in _split3(lf))
    row = lax.broadcasted_iota(jnp.int32, g.shape, 0)
    o_ref[0] = jnp.where(row < M_HEADS, g, csum)


def _mlstm_gates(gate_rows, lc):
    b_sz, two_h, s_len = gate_rows.shape
    return pl.pallas_call(
        _mlstm_gates_kernel,
        grid=(b_sz, s_len // lc),
        in_specs=[pl.BlockSpec((1, two_h, lc), lambda b, c: (b, 0, c))],
        out_specs=pl.BlockSpec((1, two_h, lc), lambda b, c: (b, 0, c)),
        out_shape=jax.ShapeDtypeStruct(gate_rows.shape, F32),
        compiler_params=_cparams(2),
        name="mlstm_gates",
    )(gate_rows)


def _mlstm_core_kernel(q_ref, k_ref, v_ref, grow_ref, gcol_ref, om_ref, zm_ref, xc_ref, hg_ref, skip_ref,
                       y_ref, c_ref, n_ref, m_ref):
    @pl.when(pl.program_id(2) == 0)
    def _():
        c_ref[...] = jnp.zeros_like(c_ref)
        n_ref[...] = jnp.zeros_like(n_ref)
        m_ref[...] = jnp.zeros_like(m_ref)

    q, k, v = q_ref[0], k_ref[0], v_ref[0]
    lc = q.shape[0]
    i_row, b_row = grow_ref[0, 0, 0:1, :], grow_ref[0, 0, 1:2, :]
    i_col, b_col = gcol_ref[0, 0, :, 0:1], gcol_ref[0, 0, :, 1:2]
    m_prev = m_ref[...]
    g = b_row[:, lc - 1:lc]

    causal = (lax.broadcasted_iota(jnp.int32, (lc, lc), 0) >= lax.broadcasted_iota(jnp.int32, (lc, lc), 1))
    dmat = jnp.where(causal, b_col - b_row + i_row, MASKED)
    inter = b_col + m_prev
    m_t = jnp.maximum(inter, jnp.max(dmat, axis=1, keepdims=True))
    qk = lax.dot_general(q, k, (((1,), (1,)), ((), ())), preferred_element_type=F32) * jnp.exp(dmat - m_t)
    w_inter = jnp.exp(inter - m_t)
    c_old = c_ref[...]
    num = (w_inter * jnp.dot(q, c_old.astype(BF16), preferred_element_type=F32)
           + jnp.dot(qk.astype(BF16), v, preferred_element_type=F32))
    den = (w_inter * jnp.sum(q.astype(F32) * n_ref[...], axis=1, keepdims=True)
           + jnp.sum(qk, axis=1, keepdims=True))
    h = num / jnp.maximum(jnp.abs(den), jnp.exp(-m_t))

    m_new = jnp.maximum(g + m_prev, jnp.max(g - b_row + i_row, axis=1, keepdims=True))
    w_s = jnp.exp(g - b_col + i_col - m_new)
    decay = jnp.exp(g + m_prev - m_new)
    kw = k.astype(F32) * w_s
    c_ref[...] = decay * c_old + lax.dot_general(kw.astype(BF16), v, (((0,), (0,)), ((), ())),
                                                 preferred_element_type=F32)
    n_ref[...] = decay * n_ref[...] + jnp.sum(kw, axis=0, keepdims=True)
    m_ref[...] = m_new

    hgated = jax.nn.sigmoid(om_ref[0]) * h
    mu = jnp.mean(hgated, axis=1, keepdims=True)
    cen = hgated - mu
    var = jnp.mean(cen * cen, axis=1, keepdims=True)
    hn = cen * lax.rsqrt(var + EPS) * hg_ref[...]
    zm = zm_ref[0]
    y_ref[0] = ((hn + skip_ref[...] * xc_ref[0]) * (zm * jax.nn.sigmoid(zm))).astype(y_ref.dtype)


def _mlstm_core(q, k, v, grow, gcol, rest3, xc, head_norm_g, skip, lc, om_col0, zm_col0):
    b_sz, s_len, width = q.shape
    dh = width // M_HEADS
    seq = lambda col0: pl.BlockSpec((1, lc, dh), lambda b, h, c: (b, c, col0 + h))
    vec = pl.BlockSpec((1, dh), lambda b, h, c: (0, h))
    return pl.pallas_call(
        _mlstm_core_kernel,
        grid=(b_sz, M_HEADS, s_len // lc),
        in_specs=[seq(0), seq(0), seq(0),
                  pl.BlockSpec((1, 1, 2, lc), lambda b, h, c: (b, h, 0, c)),
                  pl.BlockSpec((1, 1, lc, 2), lambda b, h, c: (b, h, c, 0)),
                  seq(om_col0 // dh), seq(zm_col0 // dh), seq(0), vec, vec],
        out_specs=seq(0),
        out_shape=jax.ShapeDtypeStruct((b_sz, s_len, width), BF16),
        scratch_shapes=[pltpu.VMEM((dh, dh), F32), pltpu.VMEM((1, dh), F32), pltpu.VMEM((1, 1), F32)],
        compiler_params=_cparams(3),
        name="mlstm_core",
    )(q, k, v, grow, gcol, rest3, rest3, xc, head_norm_g, skip)


def _merge_out_kernel(o1_ref, o2_ref, o3_ref, l1_ref, l2_ref, l3_ref, za_ref, gates_ref, gb_ref, ym_ref, x_ref,
                      wpa_ref, wpb_ref, wout_ref, gout_ref, out_ref):
    tm, aw = o1_ref.shape
    d_model = x_ref.shape[1]
    lses = [l1_ref[...], l2_ref[...], l3_ref[...]]
    top = jnp.maximum(jnp.maximum(lses[0], lses[1]), lses[2])
    wgts = [jnp.exp(l - top) for l in lses]
    first_head = lax.broadcasted_iota(jnp.int32, (tm, LANES), 1) < A_HEAD_DIM
    heads_per_tile = LANES // A_HEAD_DIM
    cols = []
    for j in range(aw // LANES):
        sl = slice(LANES * j, LANES * (j + 1))
        hd = heads_per_tile * j
        num = jnp.zeros((tm, LANES), F32)
        den = jnp.zeros((tm, LANES), F32)
        for w, o_ref in zip(wgts, (o1_ref, o2_ref, o3_ref)):
            wt = jnp.where(first_head, w[:, hd:hd + 1], w[:, hd + 1:hd + 2])
            num = num + wt * o_ref[:, sl]
            den = den + wt
        za = za_ref[:, sl]
        cols.append(((num / den) * (za * jax.nn.sigmoid(za))).astype(BF16))
    ya = jnp.dot(jnp.concatenate(cols, axis=1), wpa_ref[...], preferred_element_type=F32)
    ym = jnp.dot(ym_ref[...], wpb_ref[...], preferred_element_type=F32)
    gate = jax.nn.sigmoid(gates_ref[...] + gb_ref[...])
    merged = gate[:, :d_model] * ya + gate[:, d_model:] * ym
    hres = x_ref[...] + jnp.dot(merged.astype(BF16), wout_ref[...], preferred_element_type=F32)
    ms = jnp.mean(hres * hres, axis=-1, keepdims=True)
    out_ref[...] = hres * lax.rsqrt(ms + EPS) * gout_ref[...]


def _merge_out(os_, lses, rest2, gate_b, ym2, x2, wpa, wpb, wout, gout, za_col0, gates_col0, tm):
    n, d_model = x2.shape
    aw = os_[0].shape[1]
    mw = ym2.shape[1]
    tok = lambda w, cb=0: pl.BlockSpec((tm, w), lambda i: (i, cb))
    const = lambda *shape: pl.BlockSpec(shape, lambda i: (0,) * len(shape))
    return pl.pallas_call(
        _merge_out_kernel,
        grid=(n // tm,),
        in_specs=[tok(aw)] * 3 + [tok(LANES)] * 3
        + [tok(aw, za_col0 // aw), tok(2 * d_model, gates_col0 // (2 * d_model)), const(1, 2 * d_model),
           tok(mw), tok(d_model), const(aw, d_model), const(mw, d_model), const(d_model, d_model),
           const(1, d_model)],
        out_specs=tok(d_model),
        out_shape=jax.ShapeDtypeStruct((n, d_model), F32),
        compiler_params=_cparams(1),
        name="merge_out",
    )(*os_, *lses, rest2, rest2, gate_b, ym2, x2, wpa, wpb, wout, gout)


def _layer(h2, b_sz, s_len, norm_in_g, w_in, gate_b, conv_w, conv_b, wq_m, wk_m, wv_m, w_if, b_if,
           head_norm_g, skip_m, w_pa, w_pb, w_out, rel_bias, gout):
    n, d_model = h2.shape
    aw = w_pa.shape[0]
    mw = w_pb.shape[0]
    assert aw == A_HEADS * A_HEAD_DIM and aw == d_model and mw == 2 * d_model
    wb = w_in.astype(BF16)
    w_qkv = wb[:, :3 * aw]
    w_rest = jnp.concatenate([wb[:, 4 * aw:], wb[:, 3 * aw:4 * aw]], axis=1)
    xm_col0, zm_col0, om_col0, gates_col0, za_col0 = 0, mw, 2 * mw, 3 * mw, 3 * mw + 2 * d_model
    g_in = norm_in_g.reshape(1, d_model)
    qkv = _norm_matmul(h2, g_in, w_qkv, BF16, tm=1024, tn=1536)
    rest2 = _norm_matmul(h2, g_in, w_rest, F32, tm=1024, tn=1536)

    qkv3 = qkv.reshape(b_sz, s_len, 3 * aw)
    os_, lses = [], []
    for window, dilation in DILATED_PATTERNS:
        o, lse = _attn_pattern(qkv3, _band_bias(rel_bias, window // dilation, dilation), window, dilation)
        os_.append(o)
        lses.append(lse)

    assert xm_col0 == 0
    wif_pad = jnp.pad(w_if, ((0, 0), (0, LANES - w_if.shape[1]))).astype(BF16)
    bif_pad = jnp.pad(b_if, (0, LANES - b_if.shape[0])).reshape(1, LANES)
    q, k, v, xc, gate_pre = _mlstm_front(
        rest2, conv_w, conv_b.reshape(1, mw), _block_diag_tiles(wq_m).astype(BF16),
        _block_diag_tiles(wk_m).astype(BF16), _block_diag_tiles(wv_m).astype(BF16), wif_pad, bif_pad,
        s_len, mw, tm=256)
    gate_rows = jnp.transpose(gate_pre[:, :2 * M_HEADS].reshape(b_sz, s_len, 2 * M_HEADS), (0, 2, 1))
    gates_rows = _mlstm_gates(gate_rows, MLSTM_CHUNK).reshape(b_sz, 2, M_HEADS, s_len)
    grow = jnp.transpose(gates_rows, (0, 2, 1, 3))
    gcol = jnp.transpose(gates_rows, (0, 2, 3, 1))
    to3 = lambda t: t.reshape(b_sz, s_len, t.shape[-1])
    ym = _mlstm_core(to3(q), to3(k), to3(v), grow, gcol, to3(rest2), to3(xc),
                     head_norm_g.reshape(1, mw), skip_m.reshape(1, mw), MLSTM_CHUNK, om_col0, zm_col0)

    return _merge_out(os_, lses, rest2, gate_b.reshape(1, 2 * d_model), ym.reshape(n, mw), h2,
                      w_pa.astype(BF16), w_pb.astype(BF16), w_out.astype(BF16), gout.reshape(1, d_model),
                      za_col0, gates_col0, tm=256)


def kernel(x, norm_in_g, w_in, gate_b, conv_w, conv_b, wq_m, wk_m, wv_m, w_if, b_if, head_norm_g, skip_m,
           w_pa, w_pb, w_out, rel_bias, norm_out_g):
    b_sz, s_len, d_model = x.shape
    depth = w_in.shape[0]
    assert depth == 1
    out = _layer(x.reshape(b_sz * s_len, d_model), b_sz, s_len, norm_in_g[0], w_in[0], gate_b[0], conv_w[0],
                 conv_b[0], wq_m[0], wk_m[0], wv_m[0], w_if[0], b_if[0], head_norm_g[0], skip_m[0], w_pa[0],
                 w_pb[0], w_out[0], rel_bias, norm_out_g)
    return out.reshape(b_sz, s_len, d_model)
```

```python
import functools
import math

import jax
import jax.numpy as jnp
from jax import lax
from jax.experimental import pallas as pl
from jax.experimental.pallas import tpu as pltpu

F32 = jnp.float32
BF16 = jnp.bfloat16

A_HEADS = 16
A_HEAD_DIM = 64
DILATED_PATTERNS = ((128, 1), (512, 4), (2048, 16))
MAX_DISTANCE = 2048
M_HEADS = 4
EPS = 1e-6
MASKED = -1e30

LANES = 128
MXU_DIM = 256
VMEM_LIMIT_BYTES = 56 * 1024 * 1024

MLSTM_CHUNK = 256


def _cparams(n_axes):
    return pltpu.CompilerParams(dimension_semantics=("arbitrary",) * n_axes,
                                vmem_limit_bytes=VMEM_LIMIT_BYTES)


def _norm_matmul_kernel(x_ref, g_ref, w_ref, o_ref, xn_ref):
    @pl.when(pl.program_id(1) == 0)
    def _():
        xf = x_ref[...]
        ms = jnp.mean(xf * xf, axis=-1, keepdims=True)
        xn_ref[...] = (xf * lax.rsqrt(ms + EPS) * g_ref[...]).astype(BF16)

    o_ref[...] = jnp.dot(xn_ref[...], w_ref[...], preferred_element_type=F32).astype(o_ref.dtype)


def _norm_matmul(x2, g, w, out_dtype, tm, tn):
    n, d = x2.shape
    nc = w.shape[1]
    return pl.pallas_call(
        _norm_matmul_kernel,
        grid=(n // tm, nc // tn),
        in_specs=[pl.BlockSpec((tm, d), lambda i, j: (i, 0)),
                  pl.BlockSpec((1, d), lambda i, j: (0, 0)),
                  pl.BlockSpec((d, tn), lambda i, j: (0, j))],
        out_specs=pl.BlockSpec((tm, tn), lambda i, j: (i, j)),
        out_shape=jax.ShapeDtypeStruct((n, nc), out_dtype),
        scratch_shapes=[pltpu.VMEM((tm, d), BF16)],
        compiler_params=_cparams(2),
        name="norm_inproj",
    )(x2, g, w)


def _t5_bucket(dist, n_buckets):
    max_exact = n_buckets // 2
    large = max_exact + (jnp.log(jnp.maximum(dist, max_exact).astype(F32) / max_exact)
                         / math.log(MAX_DISTANCE / max_exact) * (n_buckets - max_exact)).astype(jnp.int32)
    return jnp.where(dist < max_exact, dist, jnp.minimum(large, n_buckets - 1))


def _band_bias(rel_bias, band, dilation):
    delta = jnp.arange(band + 1)
    vals = rel_bias.astype(F32)[_t5_bucket(delta * dilation, rel_bias.shape[0])]
    e = jnp.concatenate([vals[::-1].T, jnp.full((rel_bias.shape[1], band), MASKED, F32)], axis=1)
    regular = jnp.tile(e, (1, band))[:, :band * 2 * band].reshape(-1, band, 2 * band)
    kj = jnp.arange(2 * band)[None, None, :]
    first = jnp.where(kj >= band, regular, MASKED)
    return jnp.stack([first, regular])


def _attn_kernel(q_ref, kp_ref, kc_ref, vp_ref, vc_ref, bias_ref, o_ref, kk_ref, vv_ref, po_ref, pl_ref, *,
                 patterns, scale):
    sb_len = q_ref.shape[1]
    first_sb = pl.program_id(2) == 0
    kk_ref[0:sb_len, :] = kp_ref[0]
    kk_ref[sb_len:2 * sb_len, :] = kc_ref[0]
    vv_ref[0:sb_len, :] = vp_ref[0]
    vv_ref[sb_len:2 * sb_len, :] = vc_ref[0]
    heads_per_tile = LANES // A_HEAD_DIM
    assert heads_per_tile == 2

    for p, (window, d) in enumerate(patterns):
        band = window // d
        blocks_per_residue = sb_len // window
        first_head = lax.broadcasted_iota(jnp.int32, (band, LANES), 1) < A_HEAD_DIM

        def rows(start, n, d=d):
            return pl.ds(start, n, stride=d) if d > 1 else pl.ds(start, n)

        def block(i, carry, p=p, d=d, band=band, blocks_per_residue=blocks_per_residue, first_head=first_head,
                  rows=rows):
            r = i // blocks_per_residue
            jb = i % blocks_per_residue
            start = r + jb * (band * d)
            variant = jnp.where(jnp.logical_and(first_sb, jb == 0), 0, 1)
            q2 = (q_ref[0, rows(start, band), :] * scale).astype(BF16)
            k2 = kk_ref[rows(sb_len + start - band * d, 2 * band), :].astype(BF16)
            v2 = vv_ref[rows(sb_len + start - band * d, 2 * band), :].astype(BF16)
            outs, lses = [], []
            for hh in range(heads_per_tile):
                mine = first_head if hh == 0 else jnp.logical_not(first_head)
                qm = jnp.where(mine, q2, jnp.zeros_like(q2))
                s = lax.dot_general(qm, k2, (((1,), (1,)), ((), ())), preferred_element_type=F32)
                s = s + bias_ref[p, variant, hh]
                m = jnp.max(s, axis=1, keepdims=True)
                e = jnp.exp(s - m)
                l = jnp.sum(e, axis=1, keepdims=True)
                pv = jnp.dot(e.astype(BF16), v2, preferred_element_type=F32)
                outs.append(pv / l)
                lses.append(m + jnp.log(l))
            po_ref[p, rows(start, band), :] = jnp.where(first_head, outs[0], outs[1])
            pl_ref[p, rows(start, band), :] = jnp.where(first_head, lses[0], lses[1])
            return carry

        lax.fori_loop(0, sb_len // band, block, 0, unroll=8)

    chunk = 256
    for c in range(sb_len // chunk):
        sl = slice(chunk * c, chunk * (c + 1))
        lse = [pl_ref[p, sl, :] for p in range(len(patterns))]
        top = functools.reduce(jnp.maximum, lse)
        wgt = [jnp.exp(l - top) for l in lse]
        num = sum(w * po_ref[p, sl, :] for p, w in enumerate(wgt))
        o_ref[0, sl, :] = num / sum(wgt)


def _attention(proj3, rel_bias, q_col0, k_col0, v_col0, aw):
    b_sz, s_len, _ = proj3.shape
    sb_len = max(w for w, _ in DILATED_PATTERNS)
    band = DILATED_PATTERNS[0][0] // DILATED_PATTERNS[0][1]
    assert all(w // d == band and sb_len % w == 0 for w, d in DILATED_PATTERNS)
    assert s_len % sb_len == 0 and band % LANES == 0
    n_pairs = aw // LANES
    n_pat = len(DILATED_PATTERNS)
    bias = jnp.stack([_band_bias(rel_bias, band, d) for _, d in DILATED_PATTERNS])
    bias = bias.reshape(n_pat, 2, n_pairs, 2, band, 2 * band).transpose(2, 0, 1, 3, 4, 5)

    def blk(col0, prev):
        cb = col0 // LANES
        if prev:
            return pl.BlockSpec((1, sb_len, LANES), lambda j, b, s: (b, jnp.maximum(s - 1, 0), cb + j))
        return pl.BlockSpec((1, sb_len, LANES), lambda j, b, s: (b, s, cb + j))

    return pl.pallas_call(
        functools.partial(_attn_kernel, patterns=DILATED_PATTERNS, scale=A_HEAD_DIM ** -0.5),
        grid=(n_pairs, b_sz, s_len // sb_len),
        in_specs=[blk(q_col0, False), blk(k_col0, True), blk(k_col0, False), blk(v_col0, True),
                  blk(v_col0, False),
                  pl.BlockSpec((None, n_pat, 2, 2, band, 2 * band), lambda j, b, s: (j, 0, 0, 0, 0, 0))],
        out_specs=pl.BlockSpec((1, sb_len, LANES), lambda j, b, s: (b, s, j)),
        out_shape=jax.ShapeDtypeStruct((b_sz, s_len, aw), F32),
        scratch_shapes=[pltpu.VMEM((2 * sb_len, LANES), F32), pltpu.VMEM((2 * sb_len, LANES), F32),
                        pltpu.VMEM((n_pat, sb_len, LANES), F32), pltpu.VMEM((n_pat, sb_len, LANES), F32)],
        compiler_params=_cparams(3),
        name="dilated_attn",
    )(proj3, proj3, proj3, proj3, proj3, bias)


def _block_diag_tiles(w):
    nblk, qb, _ = w.shape
    per_tile = MXU_DIM // qb
    wt = w.reshape(nblk // per_tile, per_tile, qb, qb)
    eye = jnp.eye(per_tile, dtype=w.dtype)
    t = wt[:, :, :, None, :] * eye[None, :, None, :, None]
    return t.reshape(nblk // per_tile, MXU_DIM, MXU_DIM)


def _mlstm_front_kernel(x_ref, halo_ref, cw_ref, cb_ref, wq_ref, wk_ref, wv_ref, wif_ref, bif_ref,
                        q_ref, k_ref, v_ref, xc_ref, gate_ref, xe_ref, *, tiles_per_seq, k_scale):
    tm, width = x_ref.shape
    taps = cw_ref.shape[0]
    pad = halo_ref.shape[0]
    first = (pl.program_id(0) % tiles_per_seq) == 0
    xm = x_ref[...]
    xe_ref[0:pad, :] = jnp.where(first, jnp.zeros_like(halo_ref[...]), halo_ref[...])
    xe_ref[pad:pad + tm, :] = xm
    y = cb_ref[...] + xm * cw_ref[taps - 1:taps, :]
    for back in range(1, taps):
        y = y + xe_ref[pad - back:pad - back + tm, :] * cw_ref[taps - 1 - back:taps - back, :]
    xc = y * jax.nn.sigmoid(y)
    xc_ref[...] = xc
    xcb = xc.astype(BF16)
    xmb = xm.astype(BF16)
    n_tiles = width // MXU_DIM
    gate = jnp.zeros((tm, LANES), F32) + bif_ref[...]
    for j in range(n_tiles):
        sl = slice(MXU_DIM * j, MXU_DIM * (j + 1))
        qj = jnp.dot(xcb[:, sl], wq_ref[j], preferred_element_type=F32)
        kj = jnp.dot(xcb[:, sl], wk_ref[j], preferred_element_type=F32)
        vj = jnp.dot(xmb[:, sl], wv_ref[j], preferred_element_type=F32)
        qb, kb, vb = qj.astype(BF16), kj.astype(BF16), vj.astype(BF16)
        q_ref[:, sl] = qb
        k_ref[:, sl] = (kj * k_scale).astype(BF16)
        v_ref[:, sl] = vb
        for part, val in enumerate((qb, kb, vb)):
            rows = slice(part * width + MXU_DIM * j, part * width + MXU_DIM * (j + 1))
            gate = gate + jnp.dot(val, wif_ref[rows, :], preferred_element_type=F32)
    gate_ref[...] = gate


def _mlstm_front(rest2, conv_w, conv_b, wq_t, wk_t, wv_t, wif_pad, bif_pad, s_len, width, tm):
    n = rest2.shape[0]
    halo = 8
    assert conv_w.shape[0] - 1 <= halo
    const = lambda *shape: pl.BlockSpec(shape, lambda i: (0,) * len(shape))
    tok = lambda w: pl.BlockSpec((tm, w), lambda i: (i, 0))
    return pl.pallas_call(
        functools.partial(_mlstm_front_kernel, tiles_per_seq=s_len // tm,
                          k_scale=(width // M_HEADS) ** -0.5),
        grid=(n // tm,),
        in_specs=[tok(width),
                  pl.BlockSpec((halo, width), lambda i: (jnp.maximum(i * (tm // halo) - 1, 0), 0)),
                  const(*conv_w.shape), const(1, width),
                  const(*wq_t.shape), const(*wk_t.shape), const(*wv_t.shape),
                  const(*wif_pad.shape), const(1, LANES)],
        out_specs=[tok(width), tok(width), tok(width), tok(width), tok(LANES)],
        out_shape=[jax.ShapeDtypeStruct((n, width), BF16)] * 3
        + [jax.ShapeDtypeStruct((n, width), F32), jax.ShapeDtypeStruct((n, LANES), F32)],
        scratch_shapes=[pltpu.VMEM((tm + halo, width), F32)],
        compiler_params=_cparams(1),
        name="mlstm_front",
    )(rest2, rest2, conv_w, conv_b, wq_t, wk_t, wv_t, wif_pad, bif_pad)


def _split3(x):
    hi = x.astype(BF16)
    r1 = x - hi.astype(F32)
    mid = r1.astype(BF16)
    lo = (r1 - mid.astype(F32)).astype(BF16)
    return hi, mid, lo


def _mlstm_gates_kernel(g_ref, o_ref):
    g = g_ref[0]
    lc = g.shape[1]
    lf = jnp.minimum(g, 0.0) - jnp.log1p(jnp.exp(-jnp.abs(g)))
    upper = (lax.broadcasted_iota(jnp.int32, (lc, lc), 0)
             <= lax.broadcasted_iota(jnp.int32, (lc, lc), 1)).astype(BF16)
    csum = sum(jnp.dot(part, upper, preferred_element_type=F32) for part in _split3(lf))
    row = lax.broadcasted_iota(jnp.int32, g.shape, 0)
    o_ref[0] = jnp.where(row < M_HEADS, g, csum)


def _mlstm_gates(gate_rows, lc):
    b_sz, two_h, s_len = gate_rows.shape
    return pl.pallas_call(
        _mlstm_gates_kernel,
        grid=(b_sz, s_len // lc),
        in_specs=[pl.BlockSpec((1, two_h, lc), lambda b, c: (b, 0, c))],
        out_specs=pl.BlockSpec((1, two_h, lc), lambda b, c: (b, 0, c)),
        out_shape=jax.ShapeDtypeStruct(gate_rows.shape, F32),
        compiler_params=_cparams(2),
        name="mlstm_gates",
    )(gate_rows)


def _mlstm_core_kernel(q_ref, k_ref, v_ref, grow_ref, gcol_ref, om_ref, zm_ref, xc_ref, hg_ref, skip_ref,
                       y_ref, c_ref, n_ref, m_ref):
    @pl.when(pl.program_id(2) == 0)
    def _():
        c_ref[...] = jnp.zeros_like(c_ref)
        n_ref[...] = jnp.zeros_like(n_ref)
        m_ref[...] = jnp.zeros_like(m_ref)

    q, k, v = q_ref[0], k_ref[0], v_ref[0]
    lc = q.shape[0]
    i_row, b_row = grow_ref[0, 0, 0:1, :], grow_ref[0, 0, 1:2, :]
    i_col, b_col = gcol_ref[0, 0, :, 0:1], gcol_ref[0, 0, :, 1:2]
    m_prev = m_ref[...]
    g = b_row[:, lc - 1:lc]

    causal = (lax.broadcasted_iota(jnp.int32, (lc, lc), 0) >= lax.broadcasted_iota(jnp.int32, (lc, lc), 1))
    dmat = jnp.where(causal, b_col - b_row + i_row, MASKED)
    inter = b_col + m_prev
    m_t = jnp.maximum(inter, jnp.max(dmat, axis=1, keepdims=True))
    qk = lax.dot_general(q, k, (((1,), (1,)), ((), ())), preferred_element_type=F32) * jnp.exp(dmat - m_t)
    w_inter = jnp.exp(inter - m_t)
    c_old = c_ref[...]
    num = (w_inter * jnp.dot(q, c_old.astype(BF16), preferred_element_type=F32)
           + jnp.dot(qk.astype(BF16), v, preferred_element_type=F32))
    den = (w_inter * jnp.sum(q.astype(F32) * n_ref[...], axis=1, keepdims=True)
           + jnp.sum(qk, axis=1, keepdims=True))
    h = num / jnp.maximum(jnp.abs(den), jnp.exp(-m_t))

    m_new = jnp.maximum(g + m_prev, jnp.max(g - b_row + i_row, axis=1, keepdims=True))
    w_s = jnp.exp(g - b_col + i_col - m_new)
    decay = jnp.exp(g + m_prev - m_new)
    kw = k.astype(F32) * w_s
    c_ref[...] = decay * c_old + lax.dot_general(kw.astype(BF16), v, (((0,), (0,)), ((), ())),
                                                 preferred_element_type=F32)
    n_ref[...] = decay * n_ref[...] + jnp.sum(kw, axis=0, keepdims=True)
    m_ref[...] = m_new

    hgated = jax.nn.sigmoid(om_ref[0]) * h
    mu = jnp.mean(hgated, axis=1, keepdims=True)
    cen = hgated - mu
    var = jnp.mean(cen * cen, axis=1, keepdims=True)
    hn = cen * lax.rsqrt(var + EPS) * hg_ref[...]
    zm = zm_ref[0]
    y_ref[0] = ((hn + skip_ref[...] * xc_ref[0]) * (zm * jax.nn.sigmoid(zm))).astype(y_ref.dtype)


def _mlstm_core(q, k, v, grow, gcol, rest3, xc, head_norm_g, skip, lc, om_col0, zm_col0):
    b_sz, s_len, width = q.shape
    dh = width // M_HEADS
    seq = lambda col0: pl.BlockSpec((1, lc, dh), lambda b, h, c: (b, c, col0 + h))
    vec = pl.BlockSpec((1, dh), lambda b, h, c: (0, h))
    return pl.pallas_call(
        _mlstm_core_kernel,
        grid=(b_sz, M_HEADS, s_len // lc),
        in_specs=[seq(0), seq(0), seq(0),
                  pl.BlockSpec((1, 1, 2, lc), lambda b, h, c: (b, h, 0, c)),
                  pl.BlockSpec((1, 1, lc, 2), lambda b, h, c: (b, h, c, 0)),
                  seq(om_col0 // dh), seq(zm_col0 // dh), seq(0), vec, vec],
        out_specs=seq(0),
        out_shape=jax.ShapeDtypeStruct((b_sz, s_len, width), BF16),
        scratch_shapes=[pltpu.VMEM((dh, dh), F32), pltpu.VMEM((1, dh), F32), pltpu.VMEM((1, 1), F32)],
        compiler_params=_cparams(3),
        name="mlstm_core",
    )(q, k, v, grow, gcol, rest3, rest3, xc, head_norm_g, skip)


def _merge_out_kernel(ya_ref, za_ref, gates_ref, gb_ref, ym_ref, x_ref, wpa_ref, wpb_ref, wout_ref, gout_ref,
                      out_ref):
    d_model = x_ref.shape[1]
    za = za_ref[...]
    ya = jnp.dot((ya_ref[...] * (za * jax.nn.sigmoid(za))).astype(BF16), wpa_ref[...],
                 preferred_element_type=F32)
    ym = jnp.dot(ym_ref[...], wpb_ref[...], preferred_element_type=F32)
    gate = jax.nn.sigmoid(gates_ref[...] + gb_ref[...])
    merged = gate[:, :d_model] * ya + gate[:, d_model:] * ym
    hres = x_ref[...] + jnp.dot(merged.astype(BF16), wout_ref[...], preferred_element_type=F32)
    ms = jnp.mean(hres * hres, axis=-1, keepdims=True)
    out_ref[...] = hres * lax.rsqrt(ms + EPS) * gout_ref[...]


def _merge_out(ya2, proj2, gate_b, ym2, x2, wpa, wpb, wout, gout, za_col0, gates_col0, tm):
    n, d_model = x2.shape
    aw = ya2.shape[1]
    mw = ym2.shape[1]
    tok = lambda w, cb=0: pl.BlockSpec((tm, w), lambda i: (i, cb))
    const = lambda *shape: pl.BlockSpec(shape, lambda i: (0,) * len(shape))
    return pl.pallas_call(
        _merge_out_kernel,
        grid=(n // tm,),
        in_specs=[tok(aw), tok(aw, za_col0 // aw), tok(2 * d_model, gates_col0 // (2 * d_model)),
                  const(1, 2 * d_model), tok(mw), tok(d_model), const(aw, d_model), const(mw, d_model),
                  const(d_model, d_model), const(1, d_model)],
        out_specs=tok(d_model),
        out_shape=jax.ShapeDtypeStruct((n, d_model), F32),
        compiler_params=_cparams(1),
        name="merge_out",
    )(ya2, proj2, proj2, gate_b, ym2, x2, wpa, wpb, wout, gout)


def _layer(h2, b_sz, s_len, norm_in_g, w_in, gate_b, conv_w, conv_b, wq_m, wk_m, wv_m, w_if, b_if,
           head_norm_g, skip_m, w_pa, w_pb, w_out, rel_bias, gout):
    n, d_model = h2.shape
    aw = w_pa.shape[0]
    mw = w_pb.shape[0]
    assert aw == A_HEADS * A_HEAD_DIM and aw == d_model and mw == 2 * d_model
    wb = w_in.astype(BF16)
    w_all = jnp.concatenate([wb[:, 4 * aw:], wb[:, 3 * aw:4 * aw], wb[:, :3 * aw]], axis=1)
    xm_col0, zm_col0, om_col0, gates_col0, za_col0 = 0, mw, 2 * mw, 3 * mw, 3 * mw + 2 * d_model
    q_col0 = za_col0 + aw
    rest2 = _norm_matmul(h2, norm_in_g.reshape(1, d_model), w_all, F32, tm=1024, tn=1536)
    to3 = lambda t: t.reshape(b_sz, s_len, t.shape[-1])

    ya = _attention(to3(rest2), rel_bias, q_col0, q_col0 + aw, q_col0 + 2 * aw, aw)

    assert xm_col0 == 0
    wif_pad = jnp.pad(w_if, ((0, 0), (0, LANES - w_if.shape[1]))).astype(BF16)
    bif_pad = jnp.pad(b_if, (0, LANES - b_if.shape[0])).reshape(1, LANES)
    q, k, v, xc, gate_pre = _mlstm_front(
        rest2, conv_w, conv_b.reshape(1, mw), _block_diag_tiles(wq_m).astype(BF16),
        _block_diag_tiles(wk_m).astype(BF16), _block_diag_tiles(wv_m).astype(BF16), wif_pad, bif_pad,
        s_len, mw, tm=256)
    gate_rows = jnp.transpose(gate_pre[:, :2 * M_HEADS].reshape(b_sz, s_len, 2 * M_HEADS), (0, 2, 1))
    gates_rows = _mlstm_gates(gate_rows, MLSTM_CHUNK).reshape(b_sz, 2, M_HEADS, s_len)
    grow = jnp.transpose(gates_rows, (0, 2, 1, 3))
    gcol = jnp.transpose(gates_rows, (0, 2, 3, 1))
    ym = _mlstm_core(to3(q), to3(k), to3(v), grow, gcol, to3(rest2), to3(xc),
                     head_norm_g.reshape(1, mw), skip_m.reshape(1, mw), MLSTM_CHUNK, om_col0, zm_col0)

    return _merge_out(ya.reshape(n, aw), rest2, gate_b.reshape(1, 2 * d_model), ym.reshape(n, mw), h2,
                      w_pa.astype(BF16), w_pb.astype(BF16), w_out.astype(BF16), gout.reshape(1, d_model),
                      za_col0, gates_col0, tm=256)


def kernel(x, norm_in_g, w_in, gate_b, conv_w, conv_b, wq_m, wk_m, wv_m, w_if, b_if, head_norm_g, skip_m,
           w_pa, w_pb, w_out, rel_bias, norm_out_g):
    b_sz, s_len, d_model = x.shape
    depth = w_in.shape[0]
    assert depth == 1
    out = _layer(x.reshape(b_sz * s_len, d_model), b_sz, s_len, norm_in_g[0], w_in[0], gate_b[0], conv_w[0],
                 conv_b[0], wq_m[0], wk_m[0], wv_m[0], w_if[0], b_if[0], head_norm_g[0], skip_m[0], w_pa[0],
                 w_pb[0], w_out[0], rel_bias, norm_out_g)
    return out.reshape(b_sz, s_len, d_model)
```

```python
import functools
import math

import jax
import jax.numpy as jnp
from jax import lax
from jax.experimental import pallas as pl
from jax.experimental.pallas import tpu as pltpu

F32 = jnp.float32
BF16 = jnp.bfloat16

A_HEADS = 16
A_HEAD_DIM = 64
DILATED_PATTERNS = ((128, 1), (512, 4), (2048, 16))
MAX_DISTANCE = 2048
M_HEADS = 4
EPS = 1e-6
MASKED = -1e30

LANES = 128
MXU_DIM = 256
VMEM_LIMIT_BYTES = 56 * 1024 * 1024

MLSTM_CHUNK = 256
MLSTM_GROUP = 4

T_INPROJ_ROWS = 2048
T_QKV_COLS = 768
T_REST_COLS = 1536
T_FRONT_ROWS = 512
FRONT_ROW_CHUNK = 256
T_MERGE_ROWS = 512


def _cparams(n_axes):
    return pltpu.CompilerParams(dimension_semantics=("arbitrary",) * n_axes,
                                vmem_limit_bytes=VMEM_LIMIT_BYTES)


def _norm_matmul_kernel(x_ref, g_ref, w_ref, o_ref, xn_ref):
    @pl.when(pl.program_id(1) == 0)
    def _():
        xf = x_ref[...]
        ms = jnp.mean(xf * xf, axis=-1, keepdims=True)
        xn_ref[...] = (xf * lax.rsqrt(ms + EPS) * g_ref[...]).astype(BF16)

    o = jnp.dot(xn_ref[...], w_ref[...], preferred_element_type=F32).astype(o_ref.dtype)
    if len(o_ref.shape) == 2:
        o_ref[...] = o
    else:
        for c in range(o_ref.shape[0]):
            o_ref[c] = o[:, LANES * c:LANES * (c + 1)]


def _norm_matmul(x2, g, w, out_dtype, tm, tn, slabs=False):
    n, d = x2.shape
    nc = w.shape[1]
    if slabs:
        out_spec = pl.BlockSpec((tn // LANES, tm, LANES), lambda i, j: (j, i, 0))
        out_shape = jax.ShapeDtypeStruct((nc // LANES, n, LANES), out_dtype)
    else:
        out_spec = pl.BlockSpec((tm, tn), lambda i, j: (i, j))
        out_shape = jax.ShapeDtypeStruct((n, nc), out_dtype)
    return pl.pallas_call(
        _norm_matmul_kernel,
        grid=(n // tm, nc // tn),
        in_specs=[pl.BlockSpec((tm, d), lambda i, j: (i, 0)),
                  pl.BlockSpec((1, d), lambda i, j: (0, 0)),
                  pl.BlockSpec((d, tn), lambda i, j: (0, j))],
        out_specs=out_spec,
        out_shape=out_shape,
        scratch_shapes=[pltpu.VMEM((tm, d), BF16)],
        compiler_params=_cparams(2),
        name="norm_inproj",
    )(x2, g, w)


def _t5_bucket(dist, n_buckets):
    max_exact = n_buckets // 2
    large = max_exact + (jnp.log(jnp.maximum(dist, max_exact).astype(F32) / max_exact)
                         / math.log(MAX_DISTANCE / max_exact) * (n_buckets - max_exact)).astype(jnp.int32)
    return jnp.where(dist < max_exact, dist, jnp.minimum(large, n_buckets - 1))


def _band_bias(rel_bias, band, dilation):
    delta = jnp.arange(band + 1)
    vals = rel_bias.astype(F32)[_t5_bucket(delta * dilation, rel_bias.shape[0])]
    e = jnp.concatenate([vals[::-1].T, jnp.full((rel_bias.shape[1], band), MASKED, F32)], axis=1)
    regular = jnp.tile(e, (1, band))[:, :band * 2 * band].reshape(-1, band, 2 * band)
    kj = jnp.arange(2 * band)[None, None, :]
    first = jnp.where(kj >= band, regular, MASKED)
    return jnp.stack([first, regular])


def _attn_kernel(q_ref, kp_ref, kc_ref, vp_ref, vc_ref, bias_ref, o_ref, kk_ref, vv_ref, po_ref, pl_ref, *,
                 patterns, scale):
    sb_len = q_ref.shape[0]
    first_sb = pl.program_id(2) == 0
    kk_ref[0:sb_len, :] = kp_ref[...]
    kk_ref[sb_len:2 * sb_len, :] = kc_ref[...]
    vv_ref[0:sb_len, :] = vp_ref[...]
    vv_ref[sb_len:2 * sb_len, :] = vc_ref[...]
    heads_per_tile = LANES // A_HEAD_DIM
    assert heads_per_tile == 2

    for p, (window, d) in enumerate(patterns):
        band = window // d
        blocks_per_residue = sb_len // window
        first_head = lax.broadcasted_iota(jnp.int32, (band, LANES), 1) < A_HEAD_DIM

        def rows(start, n, d=d):
            return pl.ds(start, n, stride=d) if d > 1 else pl.ds(start, n)

        def block(i, carry, p=p, d=d, band=band, blocks_per_residue=blocks_per_residue, first_head=first_head,
                  rows=rows):
            r = i // blocks_per_residue
            jb = i % blocks_per_residue
            start = r + jb * (band * d)
            variant = jnp.where(jnp.logical_and(first_sb, jb == 0), 0, 1)
            q2 = (q_ref[rows(start, band), :] * scale).astype(BF16)
            k2 = kk_ref[rows(sb_len + start - band * d, 2 * band), :].astype(BF16)
            v2 = vv_ref[rows(sb_len + start - band * d, 2 * band), :].astype(BF16)
            outs, lses = [], []
            for hh in range(heads_per_tile):
                mine = first_head if hh == 0 else jnp.logical_not(first_head)
                qm = jnp.where(mine, q2, jnp.zeros_like(q2))
                s = lax.dot_general(qm, k2, (((1,), (1,)), ((), ())), preferred_element_type=F32)
                s = s + bias_ref[p, variant, hh]
                m = jnp.max(s, axis=1, keepdims=True)
                e = jnp.exp(s - m)
                l = jnp.sum(e, axis=1, keepdims=True)
                pv = jnp.dot(e.astype(BF16), v2, preferred_element_type=F32)
                outs.append(pv / l)
                lses.append(m + jnp.log(l))
            po_ref[p, rows(start, band), :] = jnp.where(first_head, outs[0], outs[1])
            pl_ref[p, rows(start, band), :] = jnp.where(first_head, lses[0], lses[1])
            return carry

        lax.fori_loop(0, sb_len // band, block, 0, unroll=8)

    chunk = 256
    for c in range(sb_len // chunk):
        sl = slice(chunk * c, chunk * (c + 1))
        lse = [pl_ref[p, sl, :] for p in range(len(patterns))]
        top = functools.reduce(jnp.maximum, lse)
        wgt = [jnp.exp(l - top) for l in lse]
        num = sum(w * po_ref[p, sl, :] for p, w in enumerate(wgt))
        o_ref[0, sl, :] = (num / sum(wgt)).astype(o_ref.dtype)


def _attention(qkv, rel_bias, aw):
    _, b_sz, s_len, _ = qkv.shape
    sb_len = max(w for w, _ in DILATED_PATTERNS)
    band = DILATED_PATTERNS[0][0] // DILATED_PATTERNS[0][1]
    assert all(w // d == band and sb_len % w == 0 for w, d in DILATED_PATTERNS)
    assert s_len % sb_len == 0 and band % LANES == 0
    n_pairs = aw // LANES
    n_pat = len(DILATED_PATTERNS)
    bias = jnp.stack([_band_bias(rel_bias, band, d) for _, d in DILATED_PATTERNS])
    bias = bias.reshape(n_pat, 2, n_pairs, 2, band, 2 * band).transpose(2, 0, 1, 3, 4, 5)

    def blk(which, prev):
        if prev:
            return pl.BlockSpec((None, None, sb_len, LANES),
                                lambda j, b, s: (which * n_pairs + j, b, jnp.maximum(s - 1, 0), 0))
        return pl.BlockSpec((None, None, sb_len, LANES), lambda j, b, s: (which * n_pairs + j, b, s, 0))

    return pl.pallas_call(
        functools.partial(_attn_kernel, patterns=DILATED_PATTERNS, scale=A_HEAD_DIM ** -0.5),
        grid=(n_pairs, b_sz, s_len // sb_len),
        in_specs=[blk(0, False), blk(1, True), blk(1, False), blk(2, True), blk(2, False),
                  pl.BlockSpec((None, n_pat, 2, 2, band, 2 * band), lambda j, b, s: (j, 0, 0, 0, 0, 0))],
        out_specs=pl.BlockSpec((1, sb_len, LANES), lambda j, b, s: (b, s, j)),
        out_shape=jax.ShapeDtypeStruct((b_sz, s_len, aw), BF16),
        scratch_shapes=[pltpu.VMEM((2 * sb_len, LANES), F32), pltpu.VMEM((2 * sb_len, LANES), F32),
                        pltpu.VMEM((n_pat, sb_len, LANES), F32), pltpu.VMEM((n_pat, sb_len, LANES), F32)],
        compiler_params=_cparams(3),
        name="dilated_attn",
    )(qkv, qkv, qkv, qkv, qkv, bias)


def _block_diag_tiles(w):
    nblk, qb, _ = w.shape
    per_tile = MXU_DIM // qb
    wt = w.reshape(nblk // per_tile, per_tile, qb, qb)
    eye = jnp.eye(per_tile, dtype=w.dtype)
    t = wt[:, :, :, None, :] * eye[None, :, None, :, None]
    return t.reshape(nblk // per_tile, MXU_DIM, MXU_DIM)


def _mlstm_front_kernel(x_ref, halo_ref, cw_ref, cb_ref, wq_ref, wk_ref, wv_ref, wif_ref, bif_ref,
                        q_ref, k_ref, v_ref, xc_ref, gate_ref, xe_ref, *, tiles_per_seq, k_scale):
    tm, width = x_ref.shape
    taps = cw_ref.shape[0]
    pad = halo_ref.shape[0]
    first = (pl.program_id(0) % tiles_per_seq) == 0
    halo = halo_ref[...].astype(F32)
    xe_ref[0:pad, :] = jnp.where(first, jnp.zeros_like(halo), halo)
    xe_ref[pad:pad + tm, :] = x_ref[...].astype(F32)
    n_tiles = width // MXU_DIM
    for r0 in range(0, tm, FRONT_ROW_CHUNK):
        rs = slice(r0, r0 + FRONT_ROW_CHUNK)
        xmb = x_ref[rs, :]
        y = cb_ref[...]
        for back in range(taps):
            y = y + (xe_ref[pad + r0 - back:pad + r0 - back + FRONT_ROW_CHUNK, :]
                     * cw_ref[taps - 1 - back:taps - back, :])
        xcb = (y * jax.nn.sigmoid(y)).astype(BF16)
        xc_ref[rs, :] = xcb
        qbs, kbs, vbs = [], [], []
        for j in range(n_tiles):
            sl = slice(MXU_DIM * j, MXU_DIM * (j + 1))
            qj = jnp.dot(xcb[:, sl], wq_ref[j], preferred_element_type=F32)
            kj = jnp.dot(xcb[:, sl], wk_ref[j], preferred_element_type=F32)
            vj = jnp.dot(xmb[:, sl], wv_ref[j], preferred_element_type=F32)
            qbs.append(qj.astype(BF16))
            kbs.append(kj.astype(BF16))
            vbs.append(vj.astype(BF16))
            k_ref[rs, sl] = (kj * k_scale).astype(BF16)
        qb, kb, vb = (jnp.concatenate(t, axis=1) for t in (qbs, kbs, vbs))
        q_ref[rs, :] = qb
        v_ref[rs, :] = vb
        gate_ref[rs, :] = (bif_ref[...]
                           + jnp.dot(qb, wif_ref[0:width, :], preferred_element_type=F32)
                           + jnp.dot(kb, wif_ref[width:2 * width, :], preferred_element_type=F32)
                           + jnp.dot(vb, wif_ref[2 * width:3 * width, :], preferred_element_type=F32))


def _mlstm_front(rest2, conv_w, conv_b, wq_t, wk_t, wv_t, wif_pad, bif_pad, s_len, width, tm):
    n = rest2.shape[0]
    halo = 16
    assert conv_w.shape[0] - 1 <= halo
    const = lambda *shape: pl.BlockSpec(shape, lambda i: (0,) * len(shape))
    tok = lambda w: pl.BlockSpec((tm, w), lambda i: (i, 0))
    return pl.pallas_call(
        functools.partial(_mlstm_front_kernel, tiles_per_seq=s_len // tm,
                          k_scale=(width // M_HEADS) ** -0.5),
        grid=(n // tm,),
        in_specs=[tok(width),
                  pl.BlockSpec((halo, width), lambda i: (jnp.maximum(i * (tm // halo) - 1, 0), 0)),
                  const(*conv_w.shape), const(1, width),
                  const(*wq_t.shape), const(*wk_t.shape), const(*wv_t.shape),
                  const(*wif_pad.shape), const(1, LANES)],
        out_specs=[tok(width), tok(width), tok(width), tok(width), tok(LANES)],
        out_shape=[jax.ShapeDtypeStruct((n, width), BF16)] * 4 + [jax.ShapeDtypeStruct((n, LANES), F32)],
        scratch_shapes=[pltpu.VMEM((tm + halo, width), F32)],
        compiler_params=_cparams(1),
        name="mlstm_front",
    )(rest2, rest2, conv_w, conv_b, wq_t, wk_t, wv_t, wif_pad, bif_pad)


def _split3(x):
    hi = x.astype(BF16)
    r1 = x - hi.astype(F32)
    mid = r1.astype(BF16)
    lo = (r1 - mid.astype(F32)).astype(BF16)
    return hi, mid, lo


def _mlstm_gates_kernel(g_ref, o_ref):
    g = g_ref[0]
    lc = g.shape[1]
    lf = jnp.minimum(g, 0.0) - jnp.log1p(jnp.exp(-jnp.abs(g)))
    upper = (lax.broadcasted_iota(jnp.int32, (lc, lc), 0)
             <= lax.broadcasted_iota(jnp.int32, (lc, lc), 1)).astype(BF16)
    csum = sum(jnp.dot(part, upper, preferred_element_type=F32) for part in _split3(lf))
    row = lax.broadcasted_iota(jnp.int32, g.shape, 0)
    o_ref[0] = jnp.where(row < M_HEADS, g, csum)


def _mlstm_gates(gate_rows, lc):
    b_sz, two_h, s_len = gate_rows.shape
    return pl.pallas_call(
        _mlstm_gates_kernel,
        grid=(b_sz, s_len // lc),
        in_specs=[pl.BlockSpec((1, two_h, lc), lambda b, c: (b, 0, c))],
        out_specs=pl.BlockSpec((1, two_h, lc), lambda b, c: (b, 0, c)),
        out_shape=jax.ShapeDtypeStruct(gate_rows.shape, F32),
        compiler_params=_cparams(2),
        name="mlstm_gates",
    )(gate_rows)


def _mlstm_core_kernel(q_ref, k_ref, v_ref, grow_ref, om_ref, zm_ref, xc_ref, hg_ref, skip_ref,
                       y_ref, c_ref, n_ref, m_ref):
    @pl.when(pl.program_id(2) == 0)
    def _():
        c_ref[...] = jnp.zeros_like(c_ref)
        n_ref[...] = jnp.zeros_like(n_ref)
        m_ref[...] = jnp.zeros_like(m_ref)

    lc = q_ref.shape[1]
    causal = (lax.broadcasted_iota(jnp.int32, (lc, lc), 0) >= lax.broadcasted_iota(jnp.int32, (lc, lc), 1))
    for s in range(q_ref.shape[0]):
        q, k, v = q_ref[s], k_ref[s], v_ref[s]
        grow = grow_ref[s, 0]
        gcol = grow.T
        i_row, b_row = grow[0:1, :], grow[1:2, :]
        i_col, b_col = gcol[:, 0:1], gcol[:, 1:2]
        m_prev = m_ref[s]
        g = b_row[:, lc - 1:lc]

        dmat = jnp.where(causal, b_col - b_row + i_row, MASKED)
        inter = b_col + m_prev
        m_t = jnp.maximum(inter, jnp.max(dmat, axis=1, keepdims=True))
        qk = lax.dot_general(q, k, (((1,), (1,)), ((), ())), preferred_element_type=F32) * jnp.exp(dmat - m_t)
        w_inter = jnp.exp(inter - m_t)
        c_old = c_ref[s]
        num = (w_inter * jnp.dot(q, c_old.astype(BF16), preferred_element_type=F32)
               + jnp.dot(qk.astype(BF16), v, preferred_element_type=F32))
        den = (w_inter * jnp.sum(q.astype(F32) * n_ref[s], axis=1, keepdims=True)
               + jnp.sum(qk, axis=1, keepdims=True))
        h = num / jnp.maximum(jnp.abs(den), jnp.exp(-m_t))

        m_new = jnp.maximum(g + m_prev, jnp.max(g - b_row + i_row, axis=1, keepdims=True))
        w_s = jnp.exp(g - b_col + i_col - m_new)
        decay = jnp.exp(g + m_prev - m_new)
        kw = k.astype(F32) * w_s
        c_ref[s] = decay * c_old + lax.dot_general(kw.astype(BF16), v, (((0,), (0,)), ((), ())),
                                                   preferred_element_type=F32)
        n_ref[s] = decay * n_ref[s] + jnp.sum(kw, axis=0, keepdims=True)
        m_ref[s] = m_new

        hgated = jax.nn.sigmoid(om_ref[s].astype(F32)) * h
        mu = jnp.mean(hgated, axis=1, keepdims=True)
        cen = hgated - mu
        var = jnp.mean(cen * cen, axis=1, keepdims=True)
        hn = cen * lax.rsqrt(var + EPS) * hg_ref[...]
        zm = zm_ref[s].astype(F32)
        y_ref[s] = ((hn + skip_ref[...] * xc_ref[s].astype(F32)) * (zm * jax.nn.sigmoid(zm))).astype(y_ref.dtype)


def _mlstm_core(q, k, v, grow, rest3, xc, head_norm_g, skip, lc, om_col0, zm_col0):
    b_sz, s_len, width = q.shape
    dh = width // M_HEADS
    grp = MLSTM_GROUP
    assert b_sz % grp == 0
    seq = lambda col0: pl.BlockSpec((grp, lc, dh), lambda b, h, c: (b, c, col0 + h))
    vec = pl.BlockSpec((1, dh), lambda b, h, c: (0, h))
    return pl.pallas_call(
        _mlstm_core_kernel,
        grid=(b_sz // grp, M_HEADS, s_len // lc),
        in_specs=[seq(0), seq(0), seq(0),
                  pl.BlockSpec((grp, 1, 2, lc), lambda b, h, c: (b, h, 0, c)),
                  seq(om_col0 // dh), seq(zm_col0 // dh), seq(0), vec, vec],
        out_specs=seq(0),
        out_shape=jax.ShapeDtypeStruct((b_sz, s_len, width), BF16),
        scratch_shapes=[pltpu.VMEM((grp, dh, dh), F32), pltpu.VMEM((grp, 1, dh), F32),
                        pltpu.VMEM((grp, 1, 1), F32)],
        compiler_params=_cparams(3),
        name="mlstm_core",
    )(q, k, v, grow, rest3, rest3, xc, head_norm_g, skip)


def _merge_out_kernel(ya_ref, za_ref, gates_ref, gb_ref, ym_ref, x_ref, wpa_ref, wpb_ref, wout_ref, gout_ref,
                      out_ref):
    d_model = x_ref.shape[1]
    za = za_ref[...].astype(F32)
    ya = jnp.dot((ya_ref[...].astype(F32) * (za * jax.nn.sigmoid(za))).astype(BF16), wpa_ref[...],
                 preferred_element_type=F32)
    ym = jnp.dot(ym_ref[...], wpb_ref[...], preferred_element_type=F32)
    gate = jax.nn.sigmoid(gates_ref[...].astype(F32) + gb_ref[...])
    merged = gate[:, :d_model] * ya + gate[:, d_model:] * ym
    hres = x_ref[...] + jnp.dot(merged.astype(BF16), wout_ref[...], preferred_element_type=F32)
    ms = jnp.mean(hres * hres, axis=-1, keepdims=True)
    out_ref[...] = hres * lax.rsqrt(ms + EPS) * gout_ref[...]


def _merge_out(ya2, proj2, gate_b, ym2, x2, wpa, wpb, wout, gout, za_col0, gates_col0, tm):
    n, d_model = x2.shape
    aw = ya2.shape[1]
    mw = ym2.shape[1]
    tok = lambda w, cb=0: pl.BlockSpec((tm, w), lambda i: (i, cb))
    const = lambda *shape: pl.BlockSpec(shape, lambda i: (0,) * len(shape))
    return pl.pallas_call(
        _merge_out_kernel,
        grid=(n // tm,),
        in_specs=[tok(aw), tok(aw, za_col0 // aw), tok(2 * d_model, gates_col0 // (2 * d_model)),
                  const(1, 2 * d_model), tok(mw), tok(d_model), const(aw, d_model), const(mw, d_model),
                  const(d_model, d_model), const(1, d_model)],
        out_specs=tok(d_model),
        out_shape=jax.ShapeDtypeStruct((n, d_model), F32),
        compiler_params=_cparams(1),
        name="merge_out",
    )(ya2, proj2, proj2, gate_b, ym2, x2, wpa, wpb, wout, gout)


def _layer(h2, b_sz, s_len, norm_in_g, w_in, gate_b, conv_w, conv_b, wq_m, wk_m, wv_m, w_if, b_if,
           head_norm_g, skip_m, w_pa, w_pb, w_out, rel_bias, gout):
    n, d_model = h2.shape
    aw = w_pa.shape[0]
    mw = w_pb.shape[0]
    assert aw == A_HEADS * A_HEAD_DIM and aw == d_model and mw == 2 * d_model
    wb = w_in.astype(BF16)
    w_rest = jnp.concatenate([wb[:, 4 * aw:], wb[:, 3 * aw:4 * aw]], axis=1)
    xm_col0, zm_col0, om_col0, gates_col0, za_col0 = 0, mw, 2 * mw, 3 * mw, 3 * mw + 2 * d_model
    g_in = norm_in_g.reshape(1, d_model)
    qkv = _norm_matmul(h2, g_in, wb[:, :3 * aw], F32, tm=T_INPROJ_ROWS, tn=T_QKV_COLS, slabs=True)
    rest2 = _norm_matmul(h2, g_in, w_rest, BF16, tm=T_INPROJ_ROWS, tn=T_REST_COLS)
    to3 = lambda t: t.reshape(b_sz, s_len, t.shape[-1])

    ya = _attention(qkv.reshape(3 * aw // LANES, b_sz, s_len, LANES), rel_bias, aw)

    assert xm_col0 == 0
    wif_pad = jnp.pad(w_if, ((0, 0), (0, LANES - w_if.shape[1]))).astype(BF16)
    bif_pad = jnp.pad(b_if, (0, LANES - b_if.shape[0])).reshape(1, LANES)
    q, k, v, xc, gate_pre = _mlstm_front(
        rest2, conv_w, conv_b.reshape(1, mw), _block_diag_tiles(wq_m).astype(BF16),
        _block_diag_tiles(wk_m).astype(BF16), _block_diag_tiles(wv_m).astype(BF16), wif_pad, bif_pad,
        s_len, mw, tm=T_FRONT_ROWS)
    gate_rows = jnp.transpose(gate_pre[:, :2 * M_HEADS].reshape(b_sz, s_len, 2 * M_HEADS), (0, 2, 1))
    gates_rows = _mlstm_gates(gate_rows, MLSTM_CHUNK).reshape(b_sz, 2, M_HEADS, s_len)
    grow = jnp.transpose(gates_rows, (0, 2, 1, 3))
    ym = _mlstm_core(to3(q), to3(k), to3(v), grow, to3(rest2), to3(xc),
                     head_norm_g.reshape(1, mw), skip_m.reshape(1, mw), MLSTM_CHUNK, om_col0, zm_col0)

    return _merge_out(ya.reshape(n, aw), rest2, gate_b.reshape(1, 2 * d_model), ym.reshape(n, mw), h2,
                      w_pa.astype(BF16), w_pb.astype(BF16), w_out.astype(BF16), gout.reshape(1, d_model),
                      za_col0, gates_col0, tm=T_MERGE_ROWS)


def kernel(x, norm_in_g, w_in, gate_b, conv_w, conv_b, wq_m, wk_m, wv_m, w_if, b_if, head_norm_g, skip_m,
           w_pa, w_pb, w_out, rel_bias, norm_out_g):
    b_sz, s_len, d_model = x.shape
    depth = w_in.shape[0]
    assert depth == 1
    out = _layer(x.reshape(b_sz * s_len, d_model), b_sz, s_len, norm_in_g[0], w_in[0], gate_b[0], conv_w[0],
                 conv_b[0], wq_m[0], wk_m[0], wv_m[0], w_if[0], b_if[0], head_norm_g[0], skip_m[0], w_pa[0],
                 w_pb[0], w_out[0], rel_bias, norm_out_g)
    return out.reshape(b_sz, s_len, d_model)
```

```python
import functools
import math

import jax
import jax.numpy as jnp
from jax import lax
from jax.experimental import pallas as pl
from jax.experimental.pallas import tpu as pltpu

F32 = jnp.float32
BF16 = jnp.bfloat16

A_HEADS = 16
A_HEAD_DIM = 64
DILATED_PATTERNS = ((128, 1), (512, 4), (2048, 16))
MAX_DISTANCE = 2048
M_HEADS = 4
EPS = 1e-6
MASKED = -1e30
LOG2E = math.log2(math.e)


def _sigmoid(x):
    return 0.5 * jnp.tanh(0.5 * x) + 0.5


def _silu(x):
    h = 0.5 * x
    return h * jnp.tanh(h) + h


LANES = 128
MXU_DIM = 256
VMEM_LIMIT_BYTES = 56 * 1024 * 1024

MLSTM_CHUNK = 256
MLSTM_GROUP = 4
GATE_ROWS = 8

T_INPROJ_ROWS = 2048
T_QKV_COLS = 768
T_REST_COLS = 1024
T_FRONT_ROWS = 512
FRONT_ROW_CHUNK = 256
T_MERGE_ROWS = 512


def _cparams(n_axes):
    return pltpu.CompilerParams(dimension_semantics=("arbitrary",) * n_axes,
                                vmem_limit_bytes=VMEM_LIMIT_BYTES)


def _norm_matmul_kernel(x_ref, g_ref, w_ref, o_ref, xn_ref):
    @pl.when(pl.program_id(1) == 0)
    def _():
        xf = x_ref[...]
        ms = jnp.mean(xf * xf, axis=-1, keepdims=True)
        xn_ref[...] = (xf * lax.rsqrt(ms + EPS) * g_ref[...]).astype(BF16)

    o = jnp.dot(xn_ref[...], w_ref[...], preferred_element_type=F32).astype(o_ref.dtype)
    if len(o_ref.shape) == 2:
        o_ref[...] = o
    else:
        for c in range(o_ref.shape[0]):
            o_ref[c] = o[:, LANES * c:LANES * (c + 1)]


def _norm_matmul(x2, g, w, w_col_block, nc, out_dtype, tm, tn, slabs=False):
    n, d = x2.shape
    if slabs:
        out_spec = pl.BlockSpec((tn // LANES, tm, LANES), lambda i, j: (j, i, 0))
        out_shape = jax.ShapeDtypeStruct((nc // LANES, n, LANES), out_dtype)
    else:
        out_spec = pl.BlockSpec((tm, tn), lambda i, j: (i, j))
        out_shape = jax.ShapeDtypeStruct((n, nc), out_dtype)
    return pl.pallas_call(
        _norm_matmul_kernel,
        grid=(n // tm, nc // tn),
        in_specs=[pl.BlockSpec((tm, d), lambda i, j: (i, 0)),
                  pl.BlockSpec((1, d), lambda i, j: (0, 0)),
                  pl.BlockSpec((d, tn), lambda i, j: (0, w_col_block(j)))],
        out_specs=out_spec,
        out_shape=out_shape,
        scratch_shapes=[pltpu.VMEM((tm, d), BF16)],
        compiler_params=_cparams(2),
        name="norm_inproj",
    )(x2, g, w)


def _t5_bucket(dist, n_buckets):
    max_exact = n_buckets // 2
    large = max_exact + (jnp.log(jnp.maximum(dist, max_exact).astype(F32) / max_exact)
                         / math.log(MAX_DISTANCE / max_exact) * (n_buckets - max_exact)).astype(jnp.int32)
    return jnp.where(dist < max_exact, dist, jnp.minimum(large, n_buckets - 1))


def _band_bias_rows(rel_bias, band, dilation):
    delta = jnp.arange(band + 1)
    vals = rel_bias.astype(F32)[_t5_bucket(delta * dilation, rel_bias.shape[0])] * LOG2E
    return jnp.concatenate([vals[::-1].T, jnp.full((rel_bias.shape[1], band - 1), MASKED, F32)], axis=1)


def _attn_kernel(q_ref, kp_ref, kc_ref, vp_ref, vc_ref, brow_ref, o_ref, kk_ref, vv_ref, po_ref, pl_ref, bias_ref,
                 *, patterns, scale):
    sb_len = q_ref.shape[0]
    first_sb = pl.program_id(2) == 0

    @pl.when(jnp.logical_and(pl.program_id(1) == 0, first_sb))
    def _():
        for p, (window, d) in enumerate(patterns):
            band = window // d
            has_prev = lax.broadcasted_iota(jnp.int32, (band, 2 * band), 1) >= band
            for hh in range(2):
                row = brow_ref[2 * p + hh:2 * p + hh + 1, :]
                table = pltpu.roll(jnp.broadcast_to(row, (band, 2 * band)), 0, 1, stride=1, stride_axis=0)
                bias_ref[p, 1, hh] = table
                bias_ref[p, 0, hh] = jnp.where(has_prev, table, MASKED)

    kk_ref[0:sb_len, :] = kp_ref[...]
    kk_ref[sb_len:2 * sb_len, :] = kc_ref[...]
    vv_ref[0:sb_len, :] = vp_ref[...]
    vv_ref[sb_len:2 * sb_len, :] = vc_ref[...]
    heads_per_tile = LANES // A_HEAD_DIM
    assert heads_per_tile == 2

    for p, (window, d) in enumerate(patterns):
        band = window // d
        blocks_per_residue = sb_len // window
        first_head = lax.broadcasted_iota(jnp.int32, (band, LANES), 1) < A_HEAD_DIM
        first_head_keys = lax.broadcasted_iota(jnp.int32, (2 * band, LANES), 1) < A_HEAD_DIM

        def rows(start, n, d=d):
            return pl.ds(start, n, stride=d) if d > 1 else pl.ds(start, n)

        def block(i, carry, p=p, d=d, band=band, blocks_per_residue=blocks_per_residue, first_head=first_head,
                  first_head_keys=first_head_keys, rows=rows):
            r = i // blocks_per_residue
            jb = i % blocks_per_residue
            start = r + jb * (band * d)
            variant = jnp.where(jnp.logical_and(first_sb, jb == 0), 0, 1)
            q2 = (q_ref[rows(start, band), :] * (scale * LOG2E)).astype(BF16)
            k2 = kk_ref[rows(sb_len + start - band * d, 2 * band), :].astype(BF16)
            v2 = vv_ref[rows(sb_len + start - band * d, 2 * band), :].astype(BF16)
            ones = jnp.ones_like(v2)
            pvs, ms = [], []
            for hh in range(heads_per_tile):
                mine = first_head if hh == 0 else jnp.logical_not(first_head)
                qm = jnp.where(mine, q2, jnp.zeros_like(q2))
                vx = jnp.where(first_head_keys, v2, ones) if hh == 0 else jnp.where(first_head_keys, ones, v2)
                s = lax.dot_general(qm, k2, (((1,), (1,)), ((), ())), preferred_element_type=F32)
                s = s + bias_ref[p, variant, hh]
                m = jnp.max(s, axis=1, keepdims=True)
                e = jnp.exp2(s - m)
                pvs.append(jnp.dot(e.astype(BF16), vx, preferred_element_type=F32))
                ms.append(m)
            num = jnp.where(first_head, pvs[0], pvs[1])
            den = pltpu.roll(jnp.where(first_head, pvs[1], pvs[0]), A_HEAD_DIM, axis=1)
            po_ref[p, rows(start, band), :] = num / den
            pl_ref[p, rows(start, band), :] = jnp.where(first_head, ms[0], ms[1]) + jnp.log(den) * LOG2E
            return carry

        lax.fori_loop(0, sb_len // band, block, 0, unroll=8)

    chunk = 256
    for c in range(sb_len // chunk):
        sl = slice(chunk * c, chunk * (c + 1))
        lse = [pl_ref[p, sl, :] for p in range(len(patterns))]
        top = functools.reduce(jnp.maximum, lse)
        wgt = [jnp.exp2(l - top) for l in lse]
        num = sum(w * po_ref[p, sl, :] for p, w in enumerate(wgt))
        o_ref[0, sl, :] = (num / sum(wgt)).astype(o_ref.dtype)


def _attention(qkv, rel_bias, aw):
    _, b_sz, s_len, _ = qkv.shape
    sb_len = max(w for w, _ in DILATED_PATTERNS)
    band = DILATED_PATTERNS[0][0] // DILATED_PATTERNS[0][1]
    assert all(w // d == band and sb_len % w == 0 for w, d in DILATED_PATTERNS)
    assert s_len % sb_len == 0 and band % LANES == 0
    n_pairs = aw // LANES
    n_pat = len(DILATED_PATTERNS)
    brow = jnp.stack([_band_bias_rows(rel_bias, band, d) for _, d in DILATED_PATTERNS])
    brow = brow.reshape(n_pat, n_pairs, 2, 2 * band).transpose(1, 0, 2, 3).reshape(n_pairs, n_pat * 2, 2 * band)

    def blk(which, prev):
        if prev:
            return pl.BlockSpec((None, None, sb_len, LANES),
                                lambda j, b, s: (which * n_pairs + j, b, jnp.maximum(s - 1, 0), 0))
        return pl.BlockSpec((None, None, sb_len, LANES), lambda j, b, s: (which * n_pairs + j, b, s, 0))

    return pl.pallas_call(
        functools.partial(_attn_kernel, patterns=DILATED_PATTERNS, scale=A_HEAD_DIM ** -0.5),
        grid=(n_pairs, b_sz, s_len // sb_len),
        in_specs=[blk(0, False), blk(1, True), blk(1, False), blk(2, True), blk(2, False),
                  pl.BlockSpec((None, n_pat * 2, 2 * band), lambda j, b, s: (j, 0, 0))],
        out_specs=pl.BlockSpec((1, sb_len, LANES), lambda j, b, s: (b, s, j)),
        out_shape=jax.ShapeDtypeStruct((b_sz, s_len, aw), BF16),
        scratch_shapes=[pltpu.VMEM((2 * sb_len, LANES), F32), pltpu.VMEM((2 * sb_len, LANES), F32),
                        pltpu.VMEM((n_pat, sb_len, LANES), F32), pltpu.VMEM((n_pat, sb_len, LANES), F32),
                        pltpu.VMEM((n_pat, 2, 2, band, 2 * band), F32)],
        compiler_params=_cparams(3),
        name="dilated_attn",
    )(qkv, qkv, qkv, qkv, qkv, brow)


def _block_diag_tiles(w):
    nblk, qb, _ = w.shape
    per_tile = MXU_DIM // qb
    wt = w.reshape(nblk // per_tile, per_tile, qb, qb)
    eye = jnp.eye(per_tile, dtype=w.dtype)
    t = wt[:, :, :, None, :] * eye[None, :, None, :, None]
    return t.reshape(nblk // per_tile, MXU_DIM, MXU_DIM)


def _mlstm_front_kernel(x_ref, halo_ref, cw_ref, cb_ref, wq_ref, wk_ref, wv_ref, wif_ref, bif_ref,
                        q_ref, k_ref, v_ref, xc_ref, gate_ref, xe_ref, *, tiles_per_seq, k_scale):
    tm, width = x_ref.shape
    taps = cw_ref.shape[0]
    pad = halo_ref.shape[0]
    first = (pl.program_id(0) % tiles_per_seq) == 0
    halo = halo_ref[...].astype(F32)
    xe_ref[0:pad, :] = jnp.where(first, jnp.zeros_like(halo), halo)
    xe_ref[pad:pad + tm, :] = x_ref[...].astype(F32)
    n_tiles = width // MXU_DIM
    for r0 in range(0, tm, FRONT_ROW_CHUNK):
        rs = slice(r0, r0 + FRONT_ROW_CHUNK)
        xmb = x_ref[rs, :]
        y = cb_ref[...]
        for back in range(taps):
            y = y + (xe_ref[pad + r0 - back:pad + r0 - back + FRONT_ROW_CHUNK, :]
                     * cw_ref[taps - 1 - back:taps - back, :])
        xcb = _silu(y).astype(BF16)
        xc_ref[rs, :] = xcb
        qbs, kbs, vbs = [], [], []
        for j in range(n_tiles):
            sl = slice(MXU_DIM * j, MXU_DIM * (j + 1))
            qj = jnp.dot(xcb[:, sl], wq_ref[j], preferred_element_type=F32)
            kj = jnp.dot(xcb[:, sl], wk_ref[j], preferred_element_type=F32)
            vj = jnp.dot(xmb[:, sl], wv_ref[j], preferred_element_type=F32)
            qbs.append(qj.astype(BF16))
            kbs.append(kj.astype(BF16))
            vbs.append(vj.astype(BF16))
            k_ref[rs, sl] = (kj * k_scale).astype(BF16)
        qb, kb, vb = (jnp.concatenate(t, axis=1) for t in (qbs, kbs, vbs))
        q_ref[rs, :] = qb
        v_ref[rs, :] = vb
        gate_ref[rs, :] = (bif_ref[...]
                           + jnp.dot(qb, wif_ref[0:width, :], preferred_element_type=F32)
                           + jnp.dot(kb, wif_ref[width:2 * width, :], preferred_element_type=F32)
                           + jnp.dot(vb, wif_ref[2 * width:3 * width, :], preferred_element_type=F32))


def _mlstm_front(rest2, conv_w, conv_b, wq_t, wk_t, wv_t, wif_pad, bif_pad, s_len, width, tm):
    n = rest2.shape[0]
    halo = 16
    assert conv_w.shape[0] - 1 <= halo
    const = lambda *shape: pl.BlockSpec(shape, lambda i: (0,) * len(shape))
    tok = lambda w: pl.BlockSpec((tm, w), lambda i: (i, 0))
    return pl.pallas_call(
        functools.partial(_mlstm_front_kernel, tiles_per_seq=s_len // tm,
                          k_scale=(width // M_HEADS) ** -0.5),
        grid=(n // tm,),
        in_specs=[tok(width),
                  pl.BlockSpec((halo, width), lambda i: (jnp.maximum(i * (tm // halo) - 1, 0), 0)),
                  const(*conv_w.shape), const(1, width),
                  const(*wq_t.shape), const(*wk_t.shape), const(*wv_t.shape),
                  const(*wif_pad.shape), const(1, LANES)],
        out_specs=[tok(width), tok(width), tok(width), tok(width), tok(LANES)],
        out_shape=[jax.ShapeDtypeStruct((n, width), BF16)] * 4 + [jax.ShapeDtypeStruct((n, LANES), F32)],
        scratch_shapes=[pltpu.VMEM((tm + halo, width), F32)],
        compiler_params=_cparams(1),
        name="mlstm_front",
    )(rest2, rest2, conv_w, conv_b, wq_t, wk_t, wv_t, wif_pad, bif_pad)


def _split3(x):
    hi = x.astype(BF16)
    r1 = x - hi.astype(F32)
    mid = r1.astype(BF16)
    lo = (r1 - mid.astype(F32)).astype(BF16)
    return hi, mid, lo


def _mlstm_core_kernel(q_ref, k_ref, v_ref, grow_ref, om_ref, zm_ref, xc_ref, hg_ref, skip_ref,
                       y_ref, c_ref, n_ref, m_ref):
    @pl.when(pl.program_id(2) == 0)
    def _():
        c_ref[...] = jnp.zeros_like(c_ref)
        n_ref[...] = jnp.zeros_like(n_ref)
        m_ref[...] = jnp.zeros_like(m_ref)

    lc = q_ref.shape[1]
    row_id = lax.broadcasted_iota(jnp.int32, (lc, lc), 0)
    col_id = lax.broadcasted_iota(jnp.int32, (lc, lc), 1)
    causal = row_id >= col_id
    upper = (row_id <= col_id).astype(BF16)
    gate_row = lax.broadcasted_iota(jnp.int32, (GATE_ROWS, lc), 0)
    for s in range(q_ref.shape[0]):
        q, k, v = q_ref[s], k_ref[s], v_ref[s]
        pre = grow_ref[s, 0]
        lf = jnp.minimum(pre, 0.0) - jnp.log1p(jnp.exp(-jnp.abs(pre)))
        csum = sum(jnp.dot(part, upper, preferred_element_type=F32) for part in _split3(lf))
        grow = jnp.where(gate_row == 0, pre, csum)
        gcol = grow.T
        i_row, b_row = grow[0:1, :], grow[1:2, :]
        i_col, b_col = gcol[:, 0:1], gcol[:, 1:2]
        m_prev = m_ref[s]
        g = b_row[:, lc - 1:lc]

        dmat = jnp.where(causal, b_col - b_row + i_row, MASKED)
        inter = b_col + m_prev
        m_t = jnp.maximum(inter, jnp.max(dmat, axis=1, keepdims=True))
        qk = lax.dot_general(q, k, (((1,), (1,)), ((), ())), preferred_element_type=F32) * jnp.exp(dmat - m_t)
        w_inter = jnp.exp(inter - m_t)
        c_old = c_ref[s]
        num = (w_inter * jnp.dot(q, c_old.astype(BF16), preferred_element_type=F32)
               + jnp.dot(qk.astype(BF16), v, preferred_element_type=F32))
        den = (w_inter * jnp.sum(q.astype(F32) * n_ref[s], axis=1, keepdims=True)
               + jnp.sum(qk, axis=1, keepdims=True))
        h = num / jnp.maximum(jnp.abs(den), jnp.exp(-m_t))

        m_new = jnp.maximum(g + m_prev, jnp.max(g - b_row + i_row, axis=1, keepdims=True))
        w_s = jnp.exp(g - b_col + i_col - m_new)
        decay = jnp.exp(g + m_prev - m_new)
        kw = (k.astype(F32) * w_s).astype(BF16)
        c_ref[s] = decay * c_old + lax.dot_general(kw, v, (((0,), (0,)), ((), ())), preferred_element_type=F32)
        n_ref[s] = decay * n_ref[s] + jnp.dot(jnp.ones((8, lc), BF16), kw, preferred_element_type=F32)[0:1]
        m_ref[s] = m_new

        hgated = _sigmoid(om_ref[s].astype(F32)) * h
        mu = jnp.mean(hgated, axis=1, keepdims=True)
        cen = hgated - mu
        var = jnp.mean(cen * cen, axis=1, keepdims=True)
        hn = cen * lax.rsqrt(var + EPS) * hg_ref[...]
        zm = zm_ref[s].astype(F32)
        y_ref[s] = ((hn + skip_ref[...] * xc_ref[s].astype(F32)) * _silu(zm)).astype(y_ref.dtype)


def _mlstm_core(q, k, v, grow, rest3, xc, head_norm_g, skip, lc, om_col0, zm_col0):
    b_sz, s_len, width = q.shape
    dh = width // M_HEADS
    grp = MLSTM_GROUP
    assert b_sz % grp == 0
    seq = lambda col0: pl.BlockSpec((grp, lc, dh), lambda b, h, c: (b, c, col0 + h))
    vec = pl.BlockSpec((1, dh), lambda b, h, c: (0, h))
    return pl.pallas_call(
        _mlstm_core_kernel,
        grid=(b_sz // grp, M_HEADS, s_len // lc),
        in_specs=[seq(0), seq(0), seq(0),
                  pl.BlockSpec((grp, 1, GATE_ROWS, lc), lambda b, h, c: (b, h, 0, c)),
                  seq(om_col0 // dh), seq(zm_col0 // dh), seq(0), vec, vec],
        out_specs=seq(0),
        out_shape=jax.ShapeDtypeStruct((b_sz, s_len, width), BF16),
        scratch_shapes=[pltpu.VMEM((grp, dh, dh), F32), pltpu.VMEM((grp, 1, dh), F32),
                        pltpu.VMEM((grp, 1, 1), F32)],
        compiler_params=_cparams(3),
        name="mlstm_core",
    )(q, k, v, grow, rest3, rest3, xc, head_norm_g, skip)


def _merge_out_kernel(ya_ref, za_ref, gates_ref, gb_ref, ym_ref, x_ref, wpa_ref, wpb_ref, wout_ref, gout_ref,
                      out_ref):
    d_model = x_ref.shape[1]
    za = za_ref[...].astype(F32)
    ya = jnp.dot((ya_ref[...].astype(F32) * _silu(za)).astype(BF16), wpa_ref[...],
                 preferred_element_type=F32)
    ym = jnp.dot(ym_ref[...], wpb_ref[...], preferred_element_type=F32)
    gate = _sigmoid(gates_ref[...].astype(F32) + gb_ref[...])
    merged = gate[:, :d_model] * ya + gate[:, d_model:] * ym
    hres = x_ref[...] + jnp.dot(merged.astype(BF16), wout_ref[...], preferred_element_type=F32)
    ms = jnp.mean(hres * hres, axis=-1, keepdims=True)
    out_ref[...] = hres * lax.rsqrt(ms + EPS) * gout_ref[...]


def _merge_out(ya2, proj2, gate_b, ym2, x2, wpa, wpb, wout, gout, za_col0, gates_col0, tm):
    n, d_model = x2.shape
    aw = ya2.shape[1]
    mw = ym2.shape[1]
    tok = lambda w, cb=0: pl.BlockSpec((tm, w), lambda i: (i, cb))
    const = lambda *shape: pl.BlockSpec(shape, lambda i: (0,) * len(shape))
    return pl.pallas_call(
        _merge_out_kernel,
        grid=(n // tm,),
        in_specs=[tok(aw), tok(aw, za_col0 // aw), tok(2 * d_model, gates_col0 // (2 * d_model)),
                  const(1, 2 * d_model), tok(mw), tok(d_model), const(aw, d_model), const(mw, d_model),
                  const(d_model, d_model), const(1, d_model)],
        out_specs=tok(d_model),
        out_shape=jax.ShapeDtypeStruct((n, d_model), F32),
        compiler_params=_cparams(1),
        name="merge_out",
    )(ya2, proj2, proj2, gate_b, ym2, x2, wpa, wpb, wout, gout)


def _layer(h2, b_sz, s_len, norm_in_g, w_in, gate_b, conv_w, conv_b, wq_m, wk_m, wv_m, w_if, b_if,
           head_norm_g, skip_m, w_pa, w_pb, w_out, rel_bias, gout):
    n, d_model = h2.shape
    aw = w_pa.shape[0]
    mw = w_pb.shape[0]
    assert aw == A_HEADS * A_HEAD_DIM and aw == d_model and mw == 2 * d_model
    wb = w_in.astype(BF16)
    xm_col0, zm_col0, om_col0, gates_col0, za_col0 = 0, mw, 2 * mw, 3 * mw, 3 * mw + 2 * d_model
    n_rest = za_col0 + aw
    assert 3 * aw % T_REST_COLS == 0 and aw == T_REST_COLS
    rest_first = 4 * aw // T_REST_COLS
    rest_tiles_before_za = (n_rest - aw) // T_REST_COLS
    za_block = 3 * aw // T_REST_COLS
    g_in = norm_in_g.reshape(1, d_model)
    qkv = _norm_matmul(h2, g_in, wb, lambda j: j, 3 * aw, F32, tm=T_INPROJ_ROWS, tn=T_QKV_COLS, slabs=True)
    rest2 = _norm_matmul(h2, g_in, wb, lambda j: jnp.where(j < rest_tiles_before_za, j + rest_first, za_block),
                         n_rest, BF16, tm=T_INPROJ_ROWS, tn=T_REST_COLS)
    to3 = lambda t: t.reshape(b_sz, s_len, t.shape[-1])

    ya = _attention(qkv.reshape(3 * aw // LANES, b_sz, s_len, LANES), rel_bias, aw)

    assert xm_col0 == 0
    wif_pad = jnp.pad(w_if, ((0, 0), (0, LANES - w_if.shape[1]))).astype(BF16)
    bif_pad = jnp.pad(b_if, (0, LANES - b_if.shape[0])).reshape(1, LANES)
    q, k, v, xc, gate_pre = _mlstm_front(
        rest2, conv_w, conv_b.reshape(1, mw), _block_diag_tiles(wq_m).astype(BF16),
        _block_diag_tiles(wk_m).astype(BF16), _block_diag_tiles(wv_m).astype(BF16), wif_pad, bif_pad,
        s_len, mw, tm=T_FRONT_ROWS)
    grow = jnp.transpose(gate_pre[:, :2 * M_HEADS].reshape(b_sz, s_len, 2, M_HEADS), (0, 3, 2, 1))
    grow = jnp.pad(grow, ((0, 0), (0, 0), (0, GATE_ROWS - 2), (0, 0)))
    ym = _mlstm_core(to3(q), to3(k), to3(v), grow, to3(rest2), to3(xc),
                     head_norm_g.reshape(1, mw), skip_m.reshape(1, mw), MLSTM_CHUNK, om_col0, zm_col0)

    return _merge_out(ya.reshape(n, aw), rest2, gate_b.reshape(1, 2 * d_model), ym.reshape(n, mw), h2,
                      w_pa.astype(BF16), w_pb.astype(BF16), w_out.astype(BF16), gout.reshape(1, d_model),
                      za_col0, gates_col0, tm=T_MERGE_ROWS)


def kernel(x, norm_in_g, w_in, gate_b, conv_w, conv_b, wq_m, wk_m, wv_m, w_if, b_if, head_norm_g, skip_m,
           w_pa, w_pb, w_out, rel_bias, norm_out_g):
    b_sz, s_len, d_model = x.shape
    depth = w_in.shape[0]
    assert depth == 1
    out = _layer(x.reshape(b_sz * s_len, d_model), b_sz, s_len, norm_in_g[0], w_in[0], gate_b[0], conv_w[0],
                 conv_b[0], wq_m[0], wk_m[0], wv_m[0], w_if[0], b_if[0], head_norm_g[0], skip_m[0], w_pa[0],
                 w_pb[0], w_out[0], rel_bias, norm_out_g)
    return out.reshape(b_sz, s_len, d_model)
```

```python
import functools
import math

import jax
import jax.numpy as jnp
from jax import lax
from jax.experimental import pallas as pl
from jax.experimental.pallas import tpu as pltpu

F32 = jnp.float32
BF16 = jnp.bfloat16

A_HEADS = 16
A_HEAD_DIM = 64
DILATED_PATTERNS = ((128, 1), (512, 4), (2048, 16))
MAX_DISTANCE = 2048
M_HEADS = 4
EPS = 1e-6
MASKED = -1e30
LOG2E = math.log2(math.e)


def _sigmoid(x):
    return 0.5 * jnp.tanh(0.5 * x) + 0.5


def _silu(x):
    h = 0.5 * x
    return h * jnp.tanh(h) + h


LANES = 128
MXU_DIM = 256
VMEM_LIMIT_BYTES = 56 * 1024 * 1024

MLSTM_CHUNK = 256
MLSTM_GROUP = 4
GATE_ROWS = 8

T_QKV_ROWS = 2048
T_QKV_COLS = 768
T_REST_ROWS = 1024
T_REST_COLS = 1024
INPROJ_ROW_CHUNK = 256
CONV_ROW_CHUNK = 128
T_FRONT_ROWS = 512
FRONT_ROW_CHUNK = 256
T_MERGE_ROWS = 512


def _cparams(n_axes):
    return pltpu.CompilerParams(dimension_semantics=("arbitrary",) * n_axes,
                                vmem_limit_bytes=VMEM_LIMIT_BYTES)


def _rmsnorm_bf16(xf, g):
    ms = jnp.mean(xf * xf, axis=-1, keepdims=True)
    return (xf * lax.rsqrt(ms + EPS) * g).astype(BF16)


def _inproj_qkv_kernel(x_ref, g_ref, w_ref, o_ref, xn_ref):
    @pl.when(pl.program_id(1) == 0)
    def _():
        xn_ref[...] = _rmsnorm_bf16(x_ref[...], g_ref[...])

    o = jnp.dot(xn_ref[...], w_ref[...], preferred_element_type=F32)
    for c in range(o_ref.shape[0]):
        o_ref[c] = o[:, LANES * c:LANES * (c + 1)]


def _inproj_qkv(x2, g, w, nc, tm, tn):
    n, d = x2.shape
    return pl.pallas_call(
        _inproj_qkv_kernel,
        grid=(n // tm, nc // tn),
        in_specs=[pl.BlockSpec((tm, d), lambda i, j: (i, 0)),
                  pl.BlockSpec((1, d), lambda i, j: (0, 0)),
                  pl.BlockSpec((d, tn), lambda i, j: (0, j))],
        out_specs=pl.BlockSpec((tn // LANES, tm, LANES), lambda i, j: (j, i, 0)),
        out_shape=jax.ShapeDtypeStruct((nc // LANES, n, LANES), F32),
        scratch_shapes=[pltpu.VMEM((tm, d), BF16)],
        compiler_params=_cparams(2),
        name="inproj_qkv",
    )(x2, g, w)


def _inproj_rest_kernel(x_ref, xh_ref, g_ref, w_ref, cw_ref, cb_ref, o_ref, xc_ref, xn_ref, ya_ref, yb_ref, *,
                        tiles_per_seq, n_conv_tiles):
    i, j = pl.program_id(0), pl.program_id(1)
    tm, halo, taps = x_ref.shape[0], xh_ref.shape[0], cw_ref.shape[0]
    first = (i % tiles_per_seq) == 0

    @pl.when(j == 0)
    def _():
        xn_ref[0:halo, :] = _rmsnorm_bf16(xh_ref[...], g_ref[...])
        xn_ref[halo:halo + tm, :] = _rmsnorm_bf16(x_ref[...], g_ref[...])

    y_refs = (ya_ref, yb_ref)

    def project_keep(y_ref, r0, nr):
        lo = 0 if r0 == 0 else halo + r0
        y = jnp.dot(xn_ref[lo:halo + r0 + nr, :], w_ref[...], preferred_element_type=F32)
        if r0 == 0:
            y_ref[0:halo, :] = jnp.where(first, jnp.zeros_like(y[0:halo]), y[0:halo])
            y = y[halo:]
        y_ref[halo + r0:halo + r0 + nr, :] = y
        o_ref[r0:r0 + nr, :] = y.astype(o_ref.dtype)

    def project(r0, nr):
        o_ref[r0:r0 + nr, :] = jnp.dot(xn_ref[halo + r0:halo + r0 + nr, :], w_ref[...],
                                       preferred_element_type=F32).astype(o_ref.dtype)

    def conv_silu(y_ref, r0, nr):
        for c0 in range(r0, r0 + nr, CONV_ROW_CHUNK):
            for l0 in range(0, xc_ref.shape[1], LANES):
                cols = slice(l0, l0 + LANES)
                acc = cb_ref[:, cols]
                for back in range(taps):
                    rows = slice(halo + c0 - back, halo + c0 - back + CONV_ROW_CHUNK)
                    acc = acc + y_ref[rows, cols] * cw_ref[taps - 1 - back:taps - back, cols]
                xc_ref[c0:c0 + CONV_ROW_CHUNK, cols] = _silu(acc).astype(xc_ref.dtype)

    for t in range(n_conv_tiles + 1):
        @pl.when(j == t)
        def _(t=t):
            for r0 in range(0, tm, INPROJ_ROW_CHUNK):
                if t < n_conv_tiles:
                    project_keep(y_refs[t % 2], r0, INPROJ_ROW_CHUNK)
                else:
                    project(r0, INPROJ_ROW_CHUNK)
                if t > 0:
                    conv_silu(y_refs[(t - 1) % 2], r0, INPROJ_ROW_CHUNK)

    @pl.when(j > n_conv_tiles)
    def _():
        project(0, tm)


def _inproj_rest(x2, g, w, w_col_block, nc, conv_w, conv_b, s_len, tm, tn):
    n, d = x2.shape
    mw = conv_w.shape[1]
    n_conv_tiles = mw // tn
    halo = 16
    assert conv_w.shape[0] - 1 <= halo and mw % tn == 0 and nc // tn > n_conv_tiles
    conv_tile = lambda i, j: (0, jnp.clip(j - 1, 0, n_conv_tiles - 1))
    return pl.pallas_call(
        functools.partial(_inproj_rest_kernel, tiles_per_seq=s_len // tm, n_conv_tiles=n_conv_tiles),
        grid=(n // tm, nc // tn),
        in_specs=[pl.BlockSpec((tm, d), lambda i, j: (i, 0)),
                  pl.BlockSpec((halo, d), lambda i, j: (jnp.maximum(i * (tm // halo) - 1, 0), 0)),
                  pl.BlockSpec((1, d), lambda i, j: (0, 0)),
                  pl.BlockSpec((d, tn), lambda i, j: (0, w_col_block(j))),
                  pl.BlockSpec((conv_w.shape[0], tn), conv_tile),
                  pl.BlockSpec((1, tn), conv_tile)],
        out_specs=[pl.BlockSpec((tm, tn), lambda i, j: (i, j)),
                   pl.BlockSpec((tm, tn), lambda i, j: (i, jnp.clip(j - 1, 0, n_conv_tiles - 1)))],
        out_shape=[jax.ShapeDtypeStruct((n, nc), BF16), jax.ShapeDtypeStruct((n, mw), BF16)],
        scratch_shapes=[pltpu.VMEM((halo + tm, d), BF16), pltpu.VMEM((halo + tm, tn), F32),
                        pltpu.VMEM((halo + tm, tn), F32)],
        compiler_params=_cparams(2),
        name="inproj_rest",
    )(x2, x2, g, w, conv_w, conv_b)


def _t5_bucket(dist, n_buckets):
    max_exact = n_buckets // 2
    large = max_exact + (jnp.log(jnp.maximum(dist, max_exact).astype(F32) / max_exact)
                         / math.log(MAX_DISTANCE / max_exact) * (n_buckets - max_exact)).astype(jnp.int32)
    return jnp.where(dist < max_exact, dist, jnp.minimum(large, n_buckets - 1))


def _band_bias_rows(rel_bias, band, dilation):
    delta = jnp.arange(band + 1)
    vals = rel_bias.astype(F32)[_t5_bucket(delta * dilation, rel_bias.shape[0])] * LOG2E
    return jnp.concatenate([vals[::-1].T, jnp.full((rel_bias.shape[1], band - 1), MASKED, F32)], axis=1)


def _attn_kernel(q_ref, kp_ref, kc_ref, vp_ref, vc_ref, brow_ref, o_ref, kk_ref, vv_ref, po_ref, pl_ref, bias_ref,
                 *, patterns, scale):
    sb_len = q_ref.shape[0]
    first_sb = pl.program_id(2) == 0

    @pl.when(jnp.logical_and(pl.program_id(1) == 0, first_sb))
    def _():
        for p, (window, d) in enumerate(patterns):
            band = window // d
            has_prev = lax.broadcasted_iota(jnp.int32, (band, 2 * band), 1) >= band
            for hh in range(2):
                row = brow_ref[2 * p + hh:2 * p + hh + 1, :]
                table = pltpu.roll(jnp.broadcast_to(row, (band, 2 * band)), 0, 1, stride=1, stride_axis=0)
                bias_ref[p, 1, hh] = table
                bias_ref[p, 0, hh] = jnp.where(has_prev, table, MASKED)

    kk_ref[0:sb_len, :] = kp_ref[...]
    kk_ref[sb_len:2 * sb_len, :] = kc_ref[...]
    vv_ref[0:sb_len, :] = vp_ref[...]
    vv_ref[sb_len:2 * sb_len, :] = vc_ref[...]
    heads_per_tile = LANES // A_HEAD_DIM
    assert heads_per_tile == 2

    for p, (window, d) in enumerate(patterns):
        band = window // d
        blocks_per_residue = sb_len // window
        first_head = lax.broadcasted_iota(jnp.int32, (band, LANES), 1) < A_HEAD_DIM
        first_head_keys = lax.broadcasted_iota(jnp.int32, (2 * band, LANES), 1) < A_HEAD_DIM

        def rows(start, n, d=d):
            return pl.ds(start, n, stride=d) if d > 1 else pl.ds(start, n)

        def block(i, carry, p=p, d=d, band=band, blocks_per_residue=blocks_per_residue, first_head=first_head,
                  first_head_keys=first_head_keys, rows=rows):
            r = i // blocks_per_residue
            jb = i % blocks_per_residue
            start = r + jb * (band * d)
            variant = jnp.where(jnp.logical_and(first_sb, jb == 0), 0, 1)
            q2 = (q_ref[rows(start, band), :] * (scale * LOG2E)).astype(BF16)
            k2 = kk_ref[rows(sb_len + start - band * d, 2 * band), :].astype(BF16)
            v2 = vv_ref[rows(sb_len + start - band * d, 2 * band), :].astype(BF16)
            ones = jnp.ones_like(v2)
            pvs, ms = [], []
            for hh in range(heads_per_tile):
                mine = first_head if hh == 0 else jnp.logical_not(first_head)
                qm = jnp.where(mine, q2, jnp.zeros_like(q2))
                vx = jnp.where(first_head_keys, v2, ones) if hh == 0 else jnp.where(first_head_keys, ones, v2)
                s = lax.dot_general(qm, k2, (((1,), (1,)), ((), ())), preferred_element_type=F32)
                s = s + bias_ref[p, variant, hh]
                m = jnp.max(s, axis=1, keepdims=True)
                e = jnp.exp2(s - m)
                pvs.append(jnp.dot(e.astype(BF16), vx, preferred_element_type=F32))
                ms.append(m)
            num = jnp.where(first_head, pvs[0], pvs[1])
            den = pltpu.roll(jnp.where(first_head, pvs[1], pvs[0]), A_HEAD_DIM, axis=1)
            po_ref[p, rows(start, band), :] = num / den
            pl_ref[p, rows(start, band), :] = jnp.where(first_head, ms[0], ms[1]) + jnp.log(den) * LOG2E
            return carry

        lax.fori_loop(0, sb_len // band, block, 0, unroll=8)

    chunk = 256
    for c in range(sb_len // chunk):
        sl = slice(chunk * c, chunk * (c + 1))
        lse = [pl_ref[p, sl, :] for p in range(len(patterns))]
        top = functools.reduce(jnp.maximum, lse)
        wgt = [jnp.exp2(l - top) for l in lse]
        num = sum(w * po_ref[p, sl, :] for p, w in enumerate(wgt))
        o_ref[0, sl, :] = (num / sum(wgt)).astype(o_ref.dtype)


def _attention(qkv, rel_bias, aw):
    _, b_sz, s_len, _ = qkv.shape
    sb_len = max(w for w, _ in DILATED_PATTERNS)
    band = DILATED_PATTERNS[0][0] // DILATED_PATTERNS[0][1]
    assert all(w // d == band and sb_len % w == 0 for w, d in DILATED_PATTERNS)
    assert s_len % sb_len == 0 and band % LANES == 0
    n_pairs = aw // LANES
    n_pat = len(DILATED_PATTERNS)
    brow = jnp.stack([_band_bias_rows(rel_bias, band, d) for _, d in DILATED_PATTERNS])
    brow = brow.reshape(n_pat, n_pairs, 2, 2 * band).transpose(1, 0, 2, 3).reshape(n_pairs, n_pat * 2, 2 * band)

    def blk(which, prev):
        if prev:
            return pl.BlockSpec((None, None, sb_len, LANES),
                                lambda j, b, s: (which * n_pairs + j, b, jnp.maximum(s - 1, 0), 0))
        return pl.BlockSpec((None, None, sb_len, LANES), lambda j, b, s: (which * n_pairs + j, b, s, 0))

    return pl.pallas_call(
        functools.partial(_attn_kernel, patterns=DILATED_PATTERNS, scale=A_HEAD_DIM ** -0.5),
        grid=(n_pairs, b_sz, s_len // sb_len),
        in_specs=[blk(0, False), blk(1, True), blk(1, False), blk(2, True), blk(2, False),
                  pl.BlockSpec((None, n_pat * 2, 2 * band), lambda j, b, s: (j, 0, 0))],
        out_specs=pl.BlockSpec((1, sb_len, LANES), lambda j, b, s: (b, s, j)),
        out_shape=jax.ShapeDtypeStruct((b_sz, s_len, aw), BF16),
        scratch_shapes=[pltpu.VMEM((2 * sb_len, LANES), F32), pltpu.VMEM((2 * sb_len, LANES), F32),
                        pltpu.VMEM((n_pat, sb_len, LANES), F32), pltpu.VMEM((n_pat, sb_len, LANES), F32),
                        pltpu.VMEM((n_pat, 2, 2, band, 2 * band), F32)],
        compiler_params=_cparams(3),
        name="dilated_attn",
    )(qkv, qkv, qkv, qkv, qkv, brow)


def _block_diag_tiles(w):
    *lead, nblk, qb, _ = w.shape
    per_tile = MXU_DIM // qb
    wt = w.reshape(*lead, nblk // per_tile, per_tile, qb, 1, qb)
    eye = jnp.eye(per_tile, dtype=w.dtype).reshape(per_tile, 1, per_tile, 1)
    t = (wt * eye).astype(BF16)
    return t.reshape(*lead, nblk // per_tile, MXU_DIM, MXU_DIM)


def _mlstm_front_kernel(xm_ref, xc_ref, wbd_ref, wif_ref, bif_ref, q_ref, k_ref, v_ref, gate_ref, *, k_scale):
    tm, width = xm_ref.shape
    n_tiles = width // MXU_DIM
    for r0 in range(0, tm, FRONT_ROW_CHUNK):
        rs = slice(r0, r0 + FRONT_ROW_CHUNK)
        xmb = xm_ref[rs, :]
        xcb = xc_ref[rs, :]
        qbs, kbs, vbs = [], [], []
        for j in range(n_tiles):
            sl = slice(MXU_DIM * j, MXU_DIM * (j + 1))
            qj = jnp.dot(xcb[:, sl], wbd_ref[0, j], preferred_element_type=F32)
            kj = jnp.dot(xcb[:, sl], wbd_ref[1, j], preferred_element_type=F32)
            vj = jnp.dot(xmb[:, sl], wbd_ref[2, j], preferred_element_type=F32)
            qbs.append(qj.astype(BF16))
            kbs.append(kj.astype(BF16))
            vbs.append(vj.astype(BF16))
            k_ref[rs, sl] = (kj * k_scale).astype(BF16)
        qb, kb, vb = (jnp.concatenate(t, axis=1) for t in (qbs, kbs, vbs))
        q_ref[rs, :] = qb
        v_ref[rs, :] = vb
        gate_ref[rs, :] = (bif_ref[...]
                           + jnp.dot(qb, wif_ref[0:width, :], preferred_element_type=F32)
                           + jnp.dot(kb, wif_ref[width:2 * width, :], preferred_element_type=F32)
                           + jnp.dot(vb, wif_ref[2 * width:3 * width, :], preferred_element_type=F32))


def _mlstm_front(rest2, xc, wbd, wif_pad, bif_pad, width, tm):
    n = rest2.shape[0]
    const = lambda *shape: pl.BlockSpec(shape, lambda i: (0,) * len(shape))
    tok = lambda w: pl.BlockSpec((tm, w), lambda i: (i, 0))
    return pl.pallas_call(
        functools.partial(_mlstm_front_kernel, k_scale=(width // M_HEADS) ** -0.5),
        grid=(n // tm,),
        in_specs=[tok(width), tok(width), const(*wbd.shape), const(*wif_pad.shape), const(1, LANES)],
        out_specs=[tok(width), tok(width), tok(width), tok(LANES)],
        out_shape=[jax.ShapeDtypeStruct((n, width), BF16)] * 3 + [jax.ShapeDtypeStruct((n, LANES), F32)],
        compiler_params=_cparams(1),
        name="mlstm_front",
    )(rest2, xc, wbd, wif_pad, bif_pad)


def _split3(x):
    hi = x.astype(BF16)
    r1 = x - hi.astype(F32)
    mid = r1.astype(BF16)
    lo = (r1 - mid.astype(F32)).astype(BF16)
    return hi, mid, lo


def _mlstm_core_kernel(q_ref, k_ref, v_ref, grow_ref, om_ref, zm_ref, xc_ref, hg_ref, skip_ref,
                       y_ref, c_ref, n_ref, m_ref):
    @pl.when(pl.program_id(2) == 0)
    def _():
        c_ref[...] = jnp.zeros_like(c_ref)
        n_ref[...] = jnp.zeros_like(n_ref)
        m_ref[...] = jnp.zeros_like(m_ref)

    lc = q_ref.shape[1]
    row_id = lax.broadcasted_iota(jnp.int32, (lc, lc), 0)
    col_id = lax.broadcasted_iota(jnp.int32, (lc, lc), 1)
    causal = row_id >= col_id
    upper = (row_id <= col_id).astype(BF16)
    gate_row = lax.broadcasted_iota(jnp.int32, (GATE_ROWS, lc), 0)
    for s in range(q_ref.shape[0]):
        q, k, v = q_ref[s], k_ref[s], v_ref[s]
        pre = grow_ref[s, 0]
        lf = jnp.minimum(pre, 0.0) - jnp.log1p(jnp.exp(-jnp.abs(pre)))
        csum = sum(jnp.dot(part, upper, preferred_element_type=F32) for part in _split3(lf))
        grow = jnp.where(gate_row == 0, pre, csum)
        gcol = grow.T
        i_row, b_row = grow[0:1, :], grow[1:2, :]
        i_col, b_col = gcol[:, 0:1], gcol[:, 1:2]
        m_prev = m_ref[s]
        g = b_row[:, lc - 1:lc]

        dmat = jnp.where(causal, b_col - b_row + i_row, MASKED)
        inter = b_col + m_prev
        m_t = jnp.maximum(inter, jnp.max(dmat, axis=1, keepdims=True))
        qk = lax.dot_general(q, k, (((1,), (1,)), ((), ())), preferred_element_type=F32) * jnp.exp(dmat - m_t)
        w_inter = jnp.exp(inter - m_t)
        c_old = c_ref[s]
        num = (w_inter * jnp.dot(q, c_old.astype(BF16), preferred_element_type=F32)
               + jnp.dot(qk.astype(BF16), v, preferred_element_type=F32))
        den = (w_inter * jnp.sum(q.astype(F32) * n_ref[s], axis=1, keepdims=True)
               + jnp.sum(qk, axis=1, keepdims=True))
        h = num / jnp.maximum(jnp.abs(den), jnp.exp(-m_t))

        m_new = jnp.maximum(g + m_prev, jnp.max(g - b_row + i_row, axis=1, keepdims=True))
        w_s = jnp.exp(g - b_col + i_col - m_new)
        decay = jnp.exp(g + m_prev - m_new)
        kw = (k.astype(F32) * w_s).astype(BF16)
        c_ref[s] = decay * c_old + lax.dot_general(kw, v, (((0,), (0,)), ((), ())), preferred_element_type=F32)
        n_ref[s] = decay * n_ref[s] + jnp.dot(jnp.ones((8, lc), BF16), kw, preferred_element_type=F32)[0:1]
        m_ref[s] = m_new

        hgated = _sigmoid(om_ref[s].astype(F32)) * h
        mu = jnp.mean(hgated, axis=1, keepdims=True)
        cen = hgated - mu
        var = jnp.mean(cen * cen, axis=1, keepdims=True)
        hn = cen * lax.rsqrt(var + EPS) * hg_ref[...]
        zm = zm_ref[s].astype(F32)
        y_ref[s] = ((hn + skip_ref[...] * xc_ref[s].astype(F32)) * _silu(zm)).astype(y_ref.dtype)


def _mlstm_core(q, k, v, grow, rest3, xc, head_norm_g, skip, lc, om_col0, zm_col0):
    b_sz, s_len, width = q.shape
    dh = width // M_HEADS
    grp = MLSTM_GROUP
    assert b_sz % grp == 0
    seq = lambda col0: pl.BlockSpec((grp, lc, dh), lambda b, h, c: (b, c, col0 + h))
    vec = pl.BlockSpec((1, dh), lambda b, h, c: (0, h))
    return pl.pallas_call(
        _mlstm_core_kernel,
        grid=(b_sz // grp, M_HEADS, s_len // lc),
        in_specs=[seq(0), seq(0), seq(0),
                  pl.BlockSpec((grp, 1, GATE_ROWS, lc), lambda b, h, c: (b, h, 0, c)),
                  seq(om_col0 // dh), seq(zm_col0 // dh), seq(0), vec, vec],
        out_specs=seq(0),
        out_shape=jax.ShapeDtypeStruct((b_sz, s_len, width), BF16),
        scratch_shapes=[pltpu.VMEM((grp, dh, dh), F32), pltpu.VMEM((grp, 1, dh), F32),
                        pltpu.VMEM((grp, 1, 1), F32)],
        compiler_params=_cparams(3),
        name="mlstm_core",
    )(q, k, v, grow, rest3, rest3, xc, head_norm_g, skip)


def _merge_out_kernel(ya_ref, za_ref, gates_ref, gb_ref, ym_ref, x_ref, wpa_ref, wpb_ref, wout_ref, gout_ref,
                      out_ref):
    d_model = x_ref.shape[1]
    za = za_ref[...].astype(F32)
    ya = jnp.dot((ya_ref[...].astype(F32) * _silu(za)).astype(BF16), wpa_ref[...],
                 preferred_element_type=F32)
    ym = jnp.dot(ym_ref[...], wpb_ref[...], preferred_element_type=F32)
    gate = _sigmoid(gates_ref[...].astype(F32) + gb_ref[...])
    merged = gate[:, :d_model] * ya + gate[:, d_model:] * ym
    hres = x_ref[...] + jnp.dot(merged.astype(BF16), wout_ref[...], preferred_element_type=F32)
    ms = jnp.mean(hres * hres, axis=-1, keepdims=True)
    out_ref[...] = hres * lax.rsqrt(ms + EPS) * gout_ref[...]


def _merge_out(ya2, proj2, gate_b, ym2, x2, wpa, wpb, wout, gout, za_col0, gates_col0, tm):
    n, d_model = x2.shape
    aw = ya2.shape[1]
    mw = ym2.shape[1]
    tok = lambda w, cb=0: pl.BlockSpec((tm, w), lambda i: (i, cb))
    const = lambda *shape: pl.BlockSpec(shape, lambda i: (0,) * len(shape))
    return pl.pallas_call(
        _merge_out_kernel,
        grid=(n // tm,),
        in_specs=[tok(aw), tok(aw, za_col0 // aw), tok(2 * d_model, gates_col0 // (2 * d_model)),
                  const(1, 2 * d_model), tok(mw), tok(d_model), const(aw, d_model), const(mw, d_model),
                  const(d_model, d_model), const(1, d_model)],
        out_specs=tok(d_model),
        out_shape=jax.ShapeDtypeStruct((n, d_model), F32),
        compiler_params=_cparams(1),
        name="merge_out",
    )(ya2, proj2, proj2, gate_b, ym2, x2, wpa, wpb, wout, gout)


def _layer(h2, b_sz, s_len, norm_in_g, w_in, gate_b, conv_w, conv_b, wq_m, wk_m, wv_m, w_if, b_if,
           head_norm_g, skip_m, w_pa, w_pb, w_out, rel_bias, gout):
    n, d_model = h2.shape
    aw = w_pa.shape[0]
    mw = w_pb.shape[0]
    assert aw == A_HEADS * A_HEAD_DIM and aw == d_model and mw == 2 * d_model
    wb = w_in.astype(BF16)
    xm_col0, zm_col0, om_col0, gates_col0, za_col0 = 0, mw, 2 * mw, 3 * mw, 3 * mw + 2 * d_model
    n_rest = za_col0 + aw
    assert 3 * aw % T_REST_COLS == 0 and aw == T_REST_COLS
    rest_first = 4 * aw // T_REST_COLS
    rest_tiles_before_za = (n_rest - aw) // T_REST_COLS
    za_block = 3 * aw // T_REST_COLS
    g_in = norm_in_g.reshape(1, d_model)
    qkv = _inproj_qkv(h2, g_in, wb, 3 * aw, tm=T_QKV_ROWS, tn=T_QKV_COLS)
    assert xm_col0 == 0
    rest2, xc = _inproj_rest(h2, g_in, wb, lambda j: jnp.where(j < rest_tiles_before_za, j + rest_first, za_block),
                             n_rest, conv_w, conv_b.reshape(1, mw), s_len, tm=T_REST_ROWS, tn=T_REST_COLS)
    to3 = lambda t: t.reshape(b_sz, s_len, t.shape[-1])

    ya = _attention(qkv.reshape(3 * aw // LANES, b_sz, s_len, LANES), rel_bias, aw)

    wif_pad = jnp.pad(w_if, ((0, 0), (0, LANES - w_if.shape[1]))).astype(BF16)
    bif_pad = jnp.pad(b_if, (0, LANES - b_if.shape[0])).reshape(1, LANES)
    q, k, v, gate_pre = _mlstm_front(rest2, xc, _block_diag_tiles(jnp.stack([wq_m, wk_m, wv_m])), wif_pad, bif_pad,
                                     mw, tm=T_FRONT_ROWS)
    grow = jnp.transpose(gate_pre[:, :2 * M_HEADS].reshape(b_sz, s_len, 2, M_HEADS), (0, 3, 2, 1))
    grow = jnp.pad(grow, ((0, 0), (0, 0), (0, GATE_ROWS - 2), (0, 0)))
    ym = _mlstm_core(to3(q), to3(k), to3(v), grow, to3(rest2), to3(xc),
                     head_norm_g.reshape(1, mw), skip_m.reshape(1, mw), MLSTM_CHUNK, om_col0, zm_col0)

    return _merge_out(ya.reshape(n, aw), rest2, gate_b.reshape(1, 2 * d_model), ym.reshape(n, mw), h2,
                      w_pa.astype(BF16), w_pb.astype(BF16), w_out.astype(BF16), gout.reshape(1, d_model),
                      za_col0, gates_col0, tm=T_MERGE_ROWS)


def kernel(x, norm_in_g, w_in, gate_b, conv_w, conv_b, wq_m, wk_m, wv_m, w_if, b_if, head_norm_g, skip_m,
           w_pa, w_pb, w_out, rel_bias, norm_out_g):
    b_sz, s_len, d_model = x.shape
    depth = w_in.shape[0]
    assert depth == 1
    out = _layer(x.reshape(b_sz * s_len, d_model), b_sz, s_len, norm_in_g[0], w_in[0], gate_b[0], conv_w[0],
                 conv_b[0], wq_m[0], wk_m[0], wv_m[0], w_if[0], b_if[0], head_norm_g[0], skip_m[0], w_pa[0],
                 w_pb[0], w_out[0], rel_bias, norm_out_g)
    return out.reshape(b_sz, s_len, d_model)
```

```python
import functools
import math

import jax
import jax.numpy as jnp
from jax import lax
from jax.experimental import pallas as pl
from jax.experimental.pallas import tpu as pltpu

F32 = jnp.float32
BF16 = jnp.bfloat16

A_HEADS = 16
A_HEAD_DIM = 64
DILATED_PATTERNS = ((128, 1), (512, 4), (2048, 16))
MAX_DISTANCE = 2048
M_HEADS = 4
EPS = 1e-6
MASKED = -1e30
LOG2E = math.log2(math.e)


def _sigmoid(x):
    return 0.5 * jnp.tanh(0.5 * x) + 0.5


def _silu(x):
    h = 0.5 * x
    return h * jnp.tanh(h) + h


LANES = 128
MXU_DIM = 256
VMEM_LIMIT_BYTES = 56 * 1024 * 1024

MLSTM_CHUNK = 256
MLSTM_GROUP = 4
GATE_ROWS = 8

T_QKV_ROWS = 2048
T_QKV_COLS = 768
T_REST_ROWS = 2048
T_REST_COLS = 512
CONV_ROW_CHUNK = 128
T_FRONT_ROWS = 512
FRONT_ROW_CHUNK = 256
T_MERGE_ROWS = 512


def _cparams(n_axes):
    return pltpu.CompilerParams(dimension_semantics=("arbitrary",) * n_axes,
                                vmem_limit_bytes=VMEM_LIMIT_BYTES)


def _rmsnorm_bf16(xf, g):
    ms = jnp.mean(xf * xf, axis=-1, keepdims=True)
    return (xf * lax.rsqrt(ms + EPS) * g).astype(BF16)


def _inproj_qkv_kernel(x_ref, g_ref, w_ref, o_ref, xn_ref):
    @pl.when(pl.program_id(1) == 0)
    def _():
        xn_ref[...] = _rmsnorm_bf16(x_ref[...], g_ref[...])

    o = jnp.dot(xn_ref[...], w_ref[...], preferred_element_type=F32)
    for c in range(o_ref.shape[0]):
        o_ref[c] = o[:, LANES * c:LANES * (c + 1)]


def _inproj_qkv(x2, g, w, nc, tm, tn):
    n, d = x2.shape
    return pl.pallas_call(
        _inproj_qkv_kernel,
        grid=(n // tm, nc // tn),
        in_specs=[pl.BlockSpec((tm, d), lambda i, j: (i, 0)),
                  pl.BlockSpec((1, d), lambda i, j: (0, 0)),
                  pl.BlockSpec((d, tn), lambda i, j: (0, j))],
        out_specs=pl.BlockSpec((tn // LANES, tm, LANES), lambda i, j: (j, i, 0)),
        out_shape=jax.ShapeDtypeStruct((nc // LANES, n, LANES), F32),
        scratch_shapes=[pltpu.VMEM((tm, d), BF16)],
        compiler_params=_cparams(2),
        name="inproj_qkv",
    )(x2, g, w)


def _inproj_rest_kernel(x_ref, xh_ref, g_ref, w_ref, cw_ref, cb_ref, o_ref, xc_ref, xn_ref, y_ref, *,
                        tiles_per_seq, n_conv_tiles):
    i, j = pl.program_id(0), pl.program_id(1)
    tm, halo, taps = x_ref.shape[0], xh_ref.shape[0], cw_ref.shape[0]
    first = (i % tiles_per_seq) == 0

    @pl.when(j == 0)
    def _():
        xn_ref[0:halo, :] = _rmsnorm_bf16(xh_ref[...], g_ref[...])
        xn_ref[halo:halo + tm, :] = _rmsnorm_bf16(x_ref[...], g_ref[...])

    @pl.when(j < n_conv_tiles)
    def _():
        y = jnp.dot(xn_ref[...], w_ref[...], preferred_element_type=F32)
        y_ref[0:halo, :] = jnp.where(first, jnp.zeros_like(y[0:halo]), y[0:halo])
        y_ref[halo:halo + tm, :] = y[halo:]
        o_ref[...] = y[halo:].astype(o_ref.dtype)
        for c0 in range(0, tm, CONV_ROW_CHUNK):
            for l0 in range(0, xc_ref.shape[1], LANES):
                cols = slice(l0, l0 + LANES)
                acc = cb_ref[:, cols]
                for back in range(taps):
                    rows = slice(halo + c0 - back, halo + c0 - back + CONV_ROW_CHUNK)
                    acc = acc + y_ref[rows, cols] * cw_ref[taps - 1 - back:taps - back, cols]
                xc_ref[c0:c0 + CONV_ROW_CHUNK, cols] = _silu(acc).astype(xc_ref.dtype)

    @pl.when(j >= n_conv_tiles)
    def _():
        o_ref[...] = jnp.dot(xn_ref[halo:halo + tm, :], w_ref[...],
                             preferred_element_type=F32).astype(o_ref.dtype)


def _inproj_rest(x2, g, w, w_col_block, nc, conv_w, conv_b, s_len, tm, tn):
    n, d = x2.shape
    mw = conv_w.shape[1]
    n_conv_tiles = mw // tn
    halo = 16
    assert conv_w.shape[0] - 1 <= halo and mw % tn == 0 and nc // tn > n_conv_tiles
    conv_tile = lambda i, j: (0, jnp.minimum(j, n_conv_tiles - 1))
    return pl.pallas_call(
        functools.partial(_inproj_rest_kernel, tiles_per_seq=s_len // tm, n_conv_tiles=n_conv_tiles),
        grid=(n // tm, nc // tn),
        in_specs=[pl.BlockSpec((tm, d), lambda i, j: (i, 0)),
                  pl.BlockSpec((halo, d), lambda i, j: (jnp.maximum(i * (tm // halo) - 1, 0), 0)),
                  pl.BlockSpec((1, d), lambda i, j: (0, 0)),
                  pl.BlockSpec((d, tn), lambda i, j: (0, w_col_block(j))),
                  pl.BlockSpec((conv_w.shape[0], tn), conv_tile),
                  pl.BlockSpec((1, tn), conv_tile)],
        out_specs=[pl.BlockSpec((tm, tn), lambda i, j: (i, j)),
                   pl.BlockSpec((tm, tn), lambda i, j: (i, jnp.minimum(j, n_conv_tiles - 1)))],
        out_shape=[jax.ShapeDtypeStruct((n, nc), BF16), jax.ShapeDtypeStruct((n, mw), BF16)],
        scratch_shapes=[pltpu.VMEM((halo + tm, d), BF16), pltpu.VMEM((halo + tm, tn), F32)],
        compiler_params=_cparams(2),
        name="inproj_rest",
    )(x2, x2, g, w, conv_w, conv_b)


def _t5_bucket(dist, n_buckets):
    max_exact = n_buckets // 2
    large = max_exact + (jnp.log(jnp.maximum(dist, max_exact).astype(F32) / max_exact)
                         / math.log(MAX_DISTANCE / max_exact) * (n_buckets - max_exact)).astype(jnp.int32)
    return jnp.where(dist < max_exact, dist, jnp.minimum(large, n_buckets - 1))


def _band_bias_rows(rel_bias, band, dilation):
    delta = jnp.arange(band + 1)
    vals = rel_bias.astype(F32)[_t5_bucket(delta * dilation, rel_bias.shape[0])] * LOG2E
    return jnp.concatenate([vals[::-1].T, jnp.full((rel_bias.shape[1], band - 1), MASKED, F32)], axis=1)


def _attn_kernel(q_ref, kp_ref, kc_ref, vp_ref, vc_ref, brow_ref, o_ref, kk_ref, vv_ref, po_ref, pl_ref, bias_ref,
                 *, patterns, scale):
    sb_len = q_ref.shape[0]
    first_sb = pl.program_id(2) == 0

    @pl.when(jnp.logical_and(pl.program_id(1) == 0, first_sb))
    def _():
        for p, (window, d) in enumerate(patterns):
            band = window // d
            has_prev = lax.broadcasted_iota(jnp.int32, (band, 2 * band), 1) >= band
            for hh in range(2):
                row = brow_ref[2 * p + hh:2 * p + hh + 1, :]
                table = pltpu.roll(jnp.broadcast_to(row, (band, 2 * band)), 0, 1, stride=1, stride_axis=0)
                bias_ref[p, 1, hh] = table
                bias_ref[p, 0, hh] = jnp.where(has_prev, table, MASKED)

    kk_ref[0:sb_len, :] = kp_ref[...]
    kk_ref[sb_len:2 * sb_len, :] = kc_ref[...]
    vv_ref[0:sb_len, :] = vp_ref[...]
    vv_ref[sb_len:2 * sb_len, :] = vc_ref[...]
    heads_per_tile = LANES // A_HEAD_DIM
    assert heads_per_tile == 2

    for p, (window, d) in enumerate(patterns):
        band = window // d
        blocks_per_residue = sb_len // window
        first_head = lax.broadcasted_iota(jnp.int32, (band, LANES), 1) < A_HEAD_DIM
        first_head_keys = lax.broadcasted_iota(jnp.int32, (2 * band, LANES), 1) < A_HEAD_DIM

        def rows(start, n, d=d):
            return pl.ds(start, n, stride=d) if d > 1 else pl.ds(start, n)

        def block(i, carry, p=p, d=d, band=band, blocks_per_residue=blocks_per_residue, first_head=first_head,
                  first_head_keys=first_head_keys, rows=rows):
            r = i // blocks_per_residue
            jb = i % blocks_per_residue
            start = r + jb * (band * d)
            variant = jnp.where(jnp.logical_and(first_sb, jb == 0), 0, 1)
            q2 = (q_ref[rows(start, band), :] * (scale * LOG2E)).astype(BF16)
            k2 = kk_ref[rows(sb_len + start - band * d, 2 * band), :].astype(BF16)
            v2 = vv_ref[rows(sb_len + start - band * d, 2 * band), :].astype(BF16)
            ones = jnp.ones_like(v2)
            pvs, ms = [], []
            for hh in range(heads_per_tile):
                mine = first_head if hh == 0 else jnp.logical_not(first_head)
                qm = jnp.where(mine, q2, jnp.zeros_like(q2))
                vx = jnp.where(first_head_keys, v2, ones) if hh == 0 else jnp.where(first_head_keys, ones, v2)
                s = lax.dot_general(qm, k2, (((1,), (1,)), ((), ())), preferred_element_type=F32)
                s = s + bias_ref[p, variant, hh]
                m = jnp.max(s, axis=1, keepdims=True)
                e = jnp.exp2(s - m)
                pvs.append(jnp.dot(e.astype(BF16), vx, preferred_element_type=F32))
                ms.append(m)
            num = jnp.where(first_head, pvs[0], pvs[1])
            den = pltpu.roll(jnp.where(first_head, pvs[1], pvs[0]), A_HEAD_DIM, axis=1)
            po_ref[p, rows(start, band), :] = num / den
            pl_ref[p, rows(start, band), :] = jnp.where(first_head, ms[0], ms[1]) + jnp.log(den) * LOG2E
            return carry

        lax.fori_loop(0, sb_len // band, block, 0, unroll=8)

    chunk = 256
    for c in range(sb_len // chunk):
        sl = slice(chunk * c, chunk * (c + 1))
        lse = [pl_ref[p, sl, :] for p in range(len(patterns))]
        top = functools.reduce(jnp.maximum, lse)
        wgt = [jnp.exp2(l - top) for l in lse]
        num = sum(w * po_ref[p, sl, :] for p, w in enumerate(wgt))
        o_ref[0, sl, :] = (num / sum(wgt)).astype(o_ref.dtype)


def _attention(qkv, rel_bias, aw):
    _, b_sz, s_len, _ = qkv.shape
    sb_len = max(w for w, _ in DILATED_PATTERNS)
    band = DILATED_PATTERNS[0][0] // DILATED_PATTERNS[0][1]
    assert all(w // d == band and sb_len % w == 0 for w, d in DILATED_PATTERNS)
    assert s_len % sb_len == 0 and band % LANES == 0
    n_pairs = aw // LANES
    n_pat = len(DILATED_PATTERNS)
    brow = jnp.stack([_band_bias_rows(rel_bias, band, d) for _, d in DILATED_PATTERNS])
    brow = brow.reshape(n_pat, n_pairs, 2, 2 * band).transpose(1, 0, 2, 3).reshape(n_pairs, n_pat * 2, 2 * band)

    def blk(which, prev):
        if prev:
            return pl.BlockSpec((None, None, sb_len, LANES),
                                lambda j, b, s: (which * n_pairs + j, b, jnp.maximum(s - 1, 0), 0))
        return pl.BlockSpec((None, None, sb_len, LANES), lambda j, b, s: (which * n_pairs + j, b, s, 0))

    return pl.pallas_call(
        functools.partial(_attn_kernel, patterns=DILATED_PATTERNS, scale=A_HEAD_DIM ** -0.5),
        grid=(n_pairs, b_sz, s_len // sb_len),
        in_specs=[blk(0, False), blk(1, True), blk(1, False), blk(2, True), blk(2, False),
                  pl.BlockSpec((None, n_pat * 2, 2 * band), lambda j, b, s: (j, 0, 0))],
        out_specs=pl.BlockSpec((1, sb_len, LANES), lambda j, b, s: (b, s, j)),
        out_shape=jax.ShapeDtypeStruct((b_sz, s_len, aw), BF16),
        scratch_shapes=[pltpu.VMEM((2 * sb_len, LANES), F32), pltpu.VMEM((2 * sb_len, LANES), F32),
                        pltpu.VMEM((n_pat, sb_len, LANES), F32), pltpu.VMEM((n_pat, sb_len, LANES), F32),
                        pltpu.VMEM((n_pat, 2, 2, band, 2 * band), F32)],
        compiler_params=_cparams(3),
        name="dilated_attn",
    )(qkv, qkv, qkv, qkv, qkv, brow)


def _block_diag_tiles(w):
    *lead, nblk, qb, _ = w.shape
    per_tile = MXU_DIM // qb
    wt = w.reshape(*lead, nblk // per_tile, per_tile, qb, 1, qb)
    eye = jnp.eye(per_tile, dtype=w.dtype).reshape(per_tile, 1, per_tile, 1)
    t = (wt * eye).astype(BF16)
    return t.reshape(*lead, nblk // per_tile, MXU_DIM, MXU_DIM)


def _mlstm_front_kernel(xm_ref, xc_ref, wbd_ref, wif_ref, bif_ref, q_ref, k_ref, v_ref, gate_ref, *, k_scale):
    tm, width = xm_ref.shape
    n_tiles = width // MXU_DIM
    for r0 in range(0, tm, FRONT_ROW_CHUNK):
        rs = slice(r0, r0 + FRONT_ROW_CHUNK)
        xmb = xm_ref[rs, :]
        xcb = xc_ref[rs, :]
        qbs, kbs, vbs = [], [], []
        for j in range(n_tiles):
            sl = slice(MXU_DIM * j, MXU_DIM * (j + 1))
            qj = jnp.dot(xcb[:, sl], wbd_ref[0, j], preferred_element_type=F32)
            kj = jnp.dot(xcb[:, sl], wbd_ref[1, j], preferred_element_type=F32)
            vj = jnp.dot(xmb[:, sl], wbd_ref[2, j], preferred_element_type=F32)
            qbs.append(qj.astype(BF16))
            kbs.append(kj.astype(BF16))
            vbs.append(vj.astype(BF16))
            k_ref[rs, sl] = (kj * k_scale).astype(BF16)
        qb, kb, vb = (jnp.concatenate(t, axis=1) for t in (qbs, kbs, vbs))
        q_ref[rs, :] = qb
        v_ref[rs, :] = vb
        gate_ref[rs, :] = (bif_ref[...]
                           + jnp.dot(qb, wif_ref[0:width, :], preferred_element_type=F32)
                           + jnp.dot(kb, wif_ref[width:2 * width, :], preferred_element_type=F32)
                           + jnp.dot(vb, wif_ref[2 * width:3 * width, :], preferred_element_type=F32))


def _mlstm_front(rest2, xc, wbd, wif_pad, bif_pad, width, tm):
    n = rest2.shape[0]
    const = lambda *shape: pl.BlockSpec(shape, lambda i: (0,) * len(shape))
    tok = lambda w: pl.BlockSpec((tm, w), lambda i: (i, 0))
    return pl.pallas_call(
        functools.partial(_mlstm_front_kernel, k_scale=(width // M_HEADS) ** -0.5),
        grid=(n // tm,),
        in_specs=[tok(width), tok(width), const(*wbd.shape), const(*wif_pad.shape), const(1, LANES)],
        out_specs=[tok(width), tok(width), tok(width), tok(LANES)],
        out_shape=[jax.ShapeDtypeStruct((n, width), BF16)] * 3 + [jax.ShapeDtypeStruct((n, LANES), F32)],
        compiler_params=_cparams(1),
        name="mlstm_front",
    )(rest2, xc, wbd, wif_pad, bif_pad)


def _split3(x):
    hi = x.astype(BF16)
    r1 = x - hi.astype(F32)
    mid = r1.astype(BF16)
    lo = (r1 - mid.astype(F32)).astype(BF16)
    return hi, mid, lo


def _mlstm_core_kernel(q_ref, k_ref, v_ref, grow_ref, om_ref, zm_ref, xc_ref, hg_ref, skip_ref,
                       y_ref, c_ref, n_ref, m_ref):
    @pl.when(pl.program_id(2) == 0)
    def _():
        c_ref[...] = jnp.zeros_like(c_ref)
        n_ref[...] = jnp.zeros_like(n_ref)
        m_ref[...] = jnp.zeros_like(m_ref)

    lc = q_ref.shape[1]
    row_id = lax.broadcasted_iota(jnp.int32, (lc, lc), 0)
    col_id = lax.broadcasted_iota(jnp.int32, (lc, lc), 1)
    causal = row_id >= col_id
    upper = (row_id <= col_id).astype(BF16)
    gate_row = lax.broadcasted_iota(jnp.int32, (GATE_ROWS, lc), 0)
    for s in range(q_ref.shape[0]):
        q, k, v = q_ref[s], k_ref[s], v_ref[s]
        pre = grow_ref[s, 0]
        lf = jnp.minimum(pre, 0.0) - jnp.log1p(jnp.exp(-jnp.abs(pre)))
        csum = sum(jnp.dot(part, upper, preferred_element_type=F32) for part in _split3(lf))
        grow = jnp.where(gate_row == 0, pre, csum)
        gcol = grow.T
        i_row, b_row = grow[0:1, :], grow[1:2, :]
        i_col, b_col = gcol[:, 0:1], gcol[:, 1:2]
        m_prev = m_ref[s]
        g = b_row[:, lc - 1:lc]

        dmat = jnp.where(causal, b_col - b_row + i_row, MASKED)
        inter = b_col + m_prev
        m_t = jnp.maximum(inter, jnp.max(dmat, axis=1, keepdims=True))
        qk = lax.dot_general(q, k, (((1,), (1,)), ((), ())), preferred_element_type=F32) * jnp.exp(dmat - m_t)
        w_inter = jnp.exp(inter - m_t)
        c_old = c_ref[s]
        num = (w_inter * jnp.dot(q, c_old.astype(BF16), preferred_element_type=F32)
               + jnp.dot(qk.astype(BF16), v, preferred_element_type=F32))
        den = (w_inter * jnp.sum(q.astype(F32) * n_ref[s], axis=1, keepdims=True)
               + jnp.sum(qk, axis=1, keepdims=True))
        h = num / jnp.maximum(jnp.abs(den), jnp.exp(-m_t))

        m_new = jnp.maximum(g + m_prev, jnp.max(g - b_row + i_row, axis=1, keepdims=True))
        w_s = jnp.exp(g - b_col + i_col - m_new)
        decay = jnp.exp(g + m_prev - m_new)
        kw = (k.astype(F32) * w_s).astype(BF16)
        c_ref[s] = decay * c_old + lax.dot_general(kw, v, (((0,), (0,)), ((), ())), preferred_element_type=F32)
        n_ref[s] = decay * n_ref[s] + jnp.dot(jnp.ones((8, lc), BF16), kw, preferred_element_type=F32)[0:1]
        m_ref[s] = m_new

        hgated = _sigmoid(om_ref[s].astype(F32)) * h
        mu = jnp.mean(hgated, axis=1, keepdims=True)
        cen = hgated - mu
        var = jnp.mean(cen * cen, axis=1, keepdims=True)
        hn = cen * lax.rsqrt(var + EPS) * hg_ref[...]
        zm = zm_ref[s].astype(F32)
        y_ref[s] = ((hn + skip_ref[...] * xc_ref[s].astype(F32)) * _silu(zm)).astype(y_ref.dtype)


def _mlstm_core(q, k, v, grow, rest3, xc, head_norm_g, skip, lc, om_col0, zm_col0):
    b_sz, s_len, width = q.shape
    dh = width // M_HEADS
    grp = MLSTM_GROUP
    assert b_sz % grp == 0
    seq = lambda col0: pl.BlockSpec((grp, lc, dh), lambda b, h, c: (b, c, col0 + h))
    vec = pl.BlockSpec((1, dh), lambda b, h, c: (0, h))
    return pl.pallas_call(
        _mlstm_core_kernel,
        grid=(b_sz // grp, M_HEADS, s_len // lc),
        in_specs=[seq(0), seq(0), seq(0),
                  pl.BlockSpec((grp, 1, GATE_ROWS, lc), lambda b, h, c: (b, h, 0, c)),
                  seq(om_col0 // dh), seq(zm_col0 // dh), seq(0), vec, vec],
        out_specs=seq(0),
        out_shape=jax.ShapeDtypeStruct((b_sz, s_len, width), BF16),
        scratch_shapes=[pltpu.VMEM((grp, dh, dh), F32), pltpu.VMEM((grp, 1, dh), F32),
                        pltpu.VMEM((grp, 1, 1), F32)],
        compiler_params=_cparams(3),
        name="mlstm_core",
    )(q, k, v, grow, rest3, rest3, xc, head_norm_g, skip)


def _merge_out_kernel(ya_ref, za_ref, gates_ref, gb_ref, ym_ref, x_ref, wpa_ref, wpb_ref, wout_ref, gout_ref,
                      out_ref):
    d_model = x_ref.shape[1]
    za = za_ref[...].astype(F32)
    ya = jnp.dot((ya_ref[...].astype(F32) * _silu(za)).astype(BF16), wpa_ref[...],
                 preferred_element_type=F32)
    ym = jnp.dot(ym_ref[...], wpb_ref[...], preferred_element_type=F32)
    gate = _sigmoid(gates_ref[...].astype(F32) + gb_ref[...])
    merged = gate[:, :d_model] * ya + gate[:, d_model:] * ym
    hres = x_ref[...] + jnp.dot(merged.astype(BF16), wout_ref[...], preferred_element_type=F32)
    ms = jnp.mean(hres * hres, axis=-1, keepdims=True)
    out_ref[...] = hres * lax.rsqrt(ms + EPS) * gout_ref[...]


def _merge_out(ya2, proj2, gate_b, ym2, x2, wpa, wpb, wout, gout, za_col0, gates_col0, tm):
    n, d_model = x2.shape
    aw = ya2.shape[1]
    mw = ym2.shape[1]
    tok = lambda w, cb=0: pl.BlockSpec((tm, w), lambda i: (i, cb))
    const = lambda *shape: pl.BlockSpec(shape, lambda i: (0,) * len(shape))
    return pl.pallas_call(
        _merge_out_kernel,
        grid=(n // tm,),
        in_specs=[tok(aw), tok(aw, za_col0 // aw), tok(2 * d_model, gates_col0 // (2 * d_model)),
                  const(1, 2 * d_model), tok(mw), tok(d_model), const(aw, d_model), const(mw, d_model),
                  const(d_model, d_model), const(1, d_model)],
        out_specs=tok(d_model),
        out_shape=jax.ShapeDtypeStruct((n, d_model), F32),
        compiler_params=_cparams(1),
        name="merge_out",
    )(ya2, proj2, proj2, gate_b, ym2, x2, wpa, wpb, wout, gout)


def _layer(h2, b_sz, s_len, norm_in_g, w_in, gate_b, conv_w, conv_b, wq_m, wk_m, wv_m, w_if, b_if,
           head_norm_g, skip_m, w_pa, w_pb, w_out, rel_bias, gout):
    n, d_model = h2.shape
    aw = w_pa.shape[0]
    mw = w_pb.shape[0]
    assert aw == A_HEADS * A_HEAD_DIM and aw == d_model and mw == 2 * d_model
    wb = w_in.astype(BF16)
    xm_col0, zm_col0, om_col0, gates_col0, za_col0 = 0, mw, 2 * mw, 3 * mw, 3 * mw + 2 * d_model
    n_rest = za_col0 + aw
    assert aw % T_REST_COLS == 0
    rest_first = 4 * aw // T_REST_COLS
    rest_tiles_before_za = (n_rest - aw) // T_REST_COLS
    za_first = 3 * aw // T_REST_COLS
    g_in = norm_in_g.reshape(1, d_model)
    qkv = _inproj_qkv(h2, g_in, wb, 3 * aw, tm=T_QKV_ROWS, tn=T_QKV_COLS)
    assert xm_col0 == 0
    rest2, xc = _inproj_rest(
        h2, g_in, wb,
        lambda j: jnp.where(j < rest_tiles_before_za, j + rest_first, j - rest_tiles_before_za + za_first),
        n_rest, conv_w, conv_b.reshape(1, mw), s_len, tm=T_REST_ROWS, tn=T_REST_COLS)
    to3 = lambda t: t.reshape(b_sz, s_len, t.shape[-1])

    ya = _attention(qkv.reshape(3 * aw // LANES, b_sz, s_len, LANES), rel_bias, aw)

    wif_pad = jnp.pad(w_if, ((0, 0), (0, LANES - w_if.shape[1]))).astype(BF16)
    bif_pad = jnp.pad(b_if, (0, LANES - b_if.shape[0])).reshape(1, LANES)
    q, k, v, gate_pre = _mlstm_front(rest2, xc, _block_diag_tiles(jnp.stack([wq_m, wk_m, wv_m])), wif_pad, bif_pad,
                                     mw, tm=T_FRONT_ROWS)
    grow = jnp.transpose(gate_pre[:, :2 * M_HEADS].reshape(b_sz, s_len, 2, M_HEADS), (0, 3, 2, 1))
    grow = jnp.pad(grow, ((0, 0), (0, 0), (0, GATE_ROWS - 2), (0, 0)))
    ym = _mlstm_core(to3(q), to3(k), to3(v), grow, to3(rest2), to3(xc),
                     head_norm_g.reshape(1, mw), skip_m.reshape(1, mw), MLSTM_CHUNK, om_col0, zm_col0)

    return _merge_out(ya.reshape(n, aw), rest2, gate_b.reshape(1, 2 * d_model), ym.reshape(n, mw), h2,
                      w_pa.astype(BF16), w_pb.astype(BF16), w_out.astype(BF16), gout.reshape(1, d_model),
                      za_col0, gates_col0, tm=T_MERGE_ROWS)


def kernel(x, norm_in_g, w_in, gate_b, conv_w, conv_b, wq_m, wk_m, wv_m, w_if, b_if, head_norm_g, skip_m,
           w_pa, w_pb, w_out, rel_bias, norm_out_g):
    b_sz, s_len, d_model = x.shape
    depth = w_in.shape[0]
    assert depth == 1
    out = _layer(x.reshape(b_sz * s_len, d_model), b_sz, s_len, norm_in_g[0], w_in[0], gate_b[0], conv_w[0],
                 conv_b[0], wq_m[0], wk_m[0], wv_m[0], w_if[0], b_if[0], head_norm_g[0], skip_m[0], w_pa[0],
                 w_pb[0], w_out[0], rel_bias, norm_out_g)
    return out.reshape(b_sz, s_len, d_model)
```

```python
import functools
import math

import jax
import jax.numpy as jnp
from jax import lax
from jax.experimental import pallas as pl
from jax.experimental.pallas import tpu as pltpu

F32 = jnp.float32
BF16 = jnp.bfloat16

A_HEADS = 16
A_HEAD_DIM = 64
DILATED_PATTERNS = ((128, 1), (512, 4), (2048, 16))
MAX_DISTANCE = 2048
M_HEADS = 4
EPS = 1e-6
MASKED = -1e30
LOG2E = math.log2(math.e)


def _sigmoid(x):
    return 0.5 * jnp.tanh(0.5 * x) + 0.5


def _silu(x):
    h = 0.5 * x
    return h * jnp.tanh(h) + h


LANES = 128
MXU_DIM = 256
VMEM_LIMIT_BYTES = 56 * 1024 * 1024

MLSTM_CHUNK = 256
MLSTM_GROUP = 8
GATE_ROWS = 8
ATTN_PAIRS_PER_STEP = 2
ATTN_BLOCKS_PER_BODY = 8

T_INPROJ_ROWS = 2048
T_QKV_COLS = 768
T_REST_COLS = 1024
T_FRONT_ROWS = 512
FRONT_ROW_CHUNK = 256
T_MERGE_ROWS = 512


def _cparams(n_axes):
    return pltpu.CompilerParams(dimension_semantics=("arbitrary",) * n_axes,
                                vmem_limit_bytes=VMEM_LIMIT_BYTES)


def _norm_matmul_kernel(x_ref, g_ref, w_ref, o_ref, xn_ref):
    @pl.when(pl.program_id(1) == 0)
    def _():
        xf = x_ref[...]
        ms = jnp.mean(xf * xf, axis=-1, keepdims=True)
        xn_ref[...] = (xf * lax.rsqrt(ms + EPS) * g_ref[...]).astype(BF16)

    o = jnp.dot(xn_ref[...], w_ref[...], preferred_element_type=F32).astype(o_ref.dtype)
    if len(o_ref.shape) == 2:
        o_ref[...] = o
    else:
        for c in range(o_ref.shape[0]):
            o_ref[c] = o[:, LANES * c:LANES * (c + 1)]


def _norm_matmul(x2, g, w, w_col_block, nc, out_dtype, tm, tn, slabs=False):
    n, d = x2.shape
    if slabs:
        out_spec = pl.BlockSpec((tn // LANES, tm, LANES), lambda i, j: (j, i, 0))
        out_shape = jax.ShapeDtypeStruct((nc // LANES, n, LANES), out_dtype)
    else:
        out_spec = pl.BlockSpec((tm, tn), lambda i, j: (i, j))
        out_shape = jax.ShapeDtypeStruct((n, nc), out_dtype)
    return pl.pallas_call(
        _norm_matmul_kernel,
        grid=(n // tm, nc // tn),
        in_specs=[pl.BlockSpec((tm, d), lambda i, j: (i, 0)),
                  pl.BlockSpec((1, d), lambda i, j: (0, 0)),
                  pl.BlockSpec((d, tn), lambda i, j: (0, w_col_block(j)))],
        out_specs=out_spec,
        out_shape=out_shape,
        scratch_shapes=[pltpu.VMEM((tm, d), BF16)],
        compiler_params=_cparams(2),
        name="norm_inproj",
    )(x2, g, w)


def _t5_bucket(dist, n_buckets):
    max_exact = n_buckets // 2
    large = max_exact + (jnp.log(jnp.maximum(dist, max_exact).astype(F32) / max_exact)
                         / math.log(MAX_DISTANCE / max_exact) * (n_buckets - max_exact)).astype(jnp.int32)
    return jnp.where(dist < max_exact, dist, jnp.minimum(large, n_buckets - 1))


def _band_bias_rows(rel_bias, band, dilation):
    delta = jnp.arange(band + 1)
    vals = rel_bias.astype(F32)[_t5_bucket(delta * dilation, rel_bias.shape[0])] * LOG2E
    return jnp.concatenate([vals[::-1].T, jnp.full((rel_bias.shape[1], band - 1), MASKED, F32)], axis=1)


def _attn_kernel(q_ref, kp_ref, kc_ref, vp_ref, vc_ref, brow_ref, o_ref, kk_ref, vv_ref, po_ref, pl_ref, bias_ref,
                 *, patterns, scale):
    n_t, sb_len, _ = q_ref.shape
    first_sb = pl.program_id(2) == 0

    @pl.when(jnp.logical_and(pl.program_id(1) == 0, first_sb))
    def _():
        for t in range(n_t):
            for p, (window, d) in enumerate(patterns):
                band = window // d
                has_prev = lax.broadcasted_iota(jnp.int32, (band, 2 * band), 1) >= band
                for hh in range(2):
                    row = brow_ref[t, 2 * p + hh:2 * p + hh + 1, :]
                    table = pltpu.roll(jnp.broadcast_to(row, (band, 2 * band)), 0, 1, stride=1, stride_axis=0)
                    bias_ref[t, p, 1, hh] = table
                    bias_ref[t, p, 0, hh] = jnp.where(has_prev, table, MASKED)

    kk_ref[:, 0:sb_len, :] = kp_ref[...]
    kk_ref[:, sb_len:2 * sb_len, :] = kc_ref[...]
    vv_ref[:, 0:sb_len, :] = vp_ref[...]
    vv_ref[:, sb_len:2 * sb_len, :] = vc_ref[...]
    heads_per_tile = LANES // A_HEAD_DIM
    assert heads_per_tile == 2

    for p, (window, d) in enumerate(patterns):
        band = window // d
        blocks_per_residue = sb_len // window
        first_head = lax.broadcasted_iota(jnp.int32, (band, LANES), 1) < A_HEAD_DIM
        first_head_keys = lax.broadcasted_iota(jnp.int32, (2 * band, LANES), 1) < A_HEAD_DIM

        def rows(start, n, d=d):
            return pl.ds(start, n, stride=d) if d > 1 else pl.ds(start, n)

        def block(i, carry, p=p, d=d, band=band, blocks_per_residue=blocks_per_residue, first_head=first_head,
                  first_head_keys=first_head_keys, rows=rows):
            r = i // blocks_per_residue
            jb = i % blocks_per_residue
            start = r + jb * (band * d)
            variant = jnp.where(jnp.logical_and(first_sb, jb == 0), 0, 1)
            for t in range(n_t):
                q2 = (q_ref[t, rows(start, band), :] * (scale * LOG2E)).astype(BF16)
                k2 = kk_ref[t, rows(sb_len + start - band * d, 2 * band), :].astype(BF16)
                v2 = vv_ref[t, rows(sb_len + start - band * d, 2 * band), :].astype(BF16)
                ones = jnp.ones_like(v2)
                pvs, ms = [], []
                for hh in range(heads_per_tile):
                    mine = first_head if hh == 0 else jnp.logical_not(first_head)
                    qm = jnp.where(mine, q2, jnp.zeros_like(q2))
                    vx = jnp.where(first_head_keys, v2, ones) if hh == 0 else jnp.where(first_head_keys, ones, v2)
                    s = lax.dot_general(qm, k2, (((1,), (1,)), ((), ())), preferred_element_type=F32)
                    s = s + bias_ref[t, p, variant, hh]
                    m = jnp.max(s, axis=1, keepdims=True)
                    e = jnp.exp2(s - m)
                    pvs.append(jnp.dot(e.astype(BF16), vx, preferred_element_type=F32))
                    ms.append(m)
                num = jnp.where(first_head, pvs[0], pvs[1])
                den = pltpu.roll(jnp.where(first_head, pvs[1], pvs[0]), A_HEAD_DIM, axis=1)
                po_ref[t, p, rows(start, band), :] = num / den
                pl_ref[t, p, rows(start, band), :] = jnp.where(first_head, ms[0], ms[1]) + jnp.log(den) * LOG2E
            return carry

        lax.fori_loop(0, sb_len // band, block, 0, unroll=ATTN_BLOCKS_PER_BODY // n_t)

    chunk = 256
    for t in range(n_t):
        for c in range(sb_len // chunk):
            sl = slice(chunk * c, chunk * (c + 1))
            lse = [pl_ref[t, p, sl, :] for p in range(len(patterns))]
            top = functools.reduce(jnp.maximum, lse)
            wgt = [jnp.exp2(l - top) for l in lse]
            num = sum(w * po_ref[t, p, sl, :] for p, w in enumerate(wgt))
            o_ref[0, sl, LANES * t:LANES * (t + 1)] = (num / sum(wgt)).astype(o_ref.dtype)


def _attention(qkv, rel_bias, aw):
    _, b_sz, s_len, _ = qkv.shape
    sb_len = max(w for w, _ in DILATED_PATTERNS)
    band = DILATED_PATTERNS[0][0] // DILATED_PATTERNS[0][1]
    assert all(w // d == band and sb_len % w == 0 for w, d in DILATED_PATTERNS)
    assert s_len % sb_len == 0 and band % LANES == 0
    n_pairs = aw // LANES
    n_pat = len(DILATED_PATTERNS)
    n_t = ATTN_PAIRS_PER_STEP
    assert n_pairs % n_t == 0 and ATTN_BLOCKS_PER_BODY % n_t == 0
    steps_j = n_pairs // n_t
    brow = jnp.stack([_band_bias_rows(rel_bias, band, d) for _, d in DILATED_PATTERNS])
    brow = brow.reshape(n_pat, n_pairs, 2, 2 * band).transpose(1, 0, 2, 3).reshape(n_pairs, n_pat * 2, 2 * band)

    def blk(which, prev):
        if prev:
            return pl.BlockSpec((n_t, None, sb_len, LANES),
                                lambda j, b, s: (which * steps_j + j, b, jnp.maximum(s - 1, 0), 0))
        return pl.BlockSpec((n_t, None, sb_len, LANES), lambda j, b, s: (which * steps_j + j, b, s, 0))

    return pl.pallas_call(
        functools.partial(_attn_kernel, patterns=DILATED_PATTERNS, scale=A_HEAD_DIM ** -0.5),
        grid=(steps_j, b_sz, s_len // sb_len),
        in_specs=[blk(0, False), blk(1, True), blk(1, False), blk(2, True), blk(2, False),
                  pl.BlockSpec((n_t, n_pat * 2, 2 * band), lambda j, b, s: (j, 0, 0))],
        out_specs=pl.BlockSpec((1, sb_len, n_t * LANES), lambda j, b, s: (b, s, j)),
        out_shape=jax.ShapeDtypeStruct((b_sz, s_len, aw), BF16),
        scratch_shapes=[pltpu.VMEM((n_t, 2 * sb_len, LANES), F32), pltpu.VMEM((n_t, 2 * sb_len, LANES), F32),
                        pltpu.VMEM((n_t, n_pat, sb_len, LANES), F32), pltpu.VMEM((n_t, n_pat, sb_len, LANES), F32),
                        pltpu.VMEM((n_t, n_pat, 2, 2, band, 2 * band), F32)],
        compiler_params=_cparams(3),
        name="dilated_attn",
    )(qkv, qkv, qkv, qkv, qkv, brow)


def _block_diag_tiles(w):
    *lead, nblk, qb, _ = w.shape
    per_tile = MXU_DIM // qb
    wt = w.reshape(*lead, nblk // per_tile, per_tile, qb, 1, qb)
    eye = jnp.eye(per_tile, dtype=w.dtype).reshape(per_tile, 1, per_tile, 1)
    t = (wt * eye).astype(BF16)
    return t.reshape(*lead, nblk // per_tile, MXU_DIM, MXU_DIM)


def _mlstm_front_kernel(x_ref, halo_ref, cw_ref, cb_ref, wbd_ref, wif_ref, bif_ref,
                        q_ref, k_ref, v_ref, xc_ref, gate_ref, xe_ref, *, tiles_per_seq, k_scale):
    tm, width = x_ref.shape
    taps = cw_ref.shape[0]
    pad = halo_ref.shape[0]
    first = (pl.program_id(0) % tiles_per_seq) == 0
    halo = halo_ref[...].astype(F32)
    xe_ref[0:pad, :] = jnp.where(first, jnp.zeros_like(halo), halo)
    xe_ref[pad:pad + tm, :] = x_ref[...].astype(F32)
    n_tiles = width // MXU_DIM
    for r0 in range(0, tm, FRONT_ROW_CHUNK):
        rs = slice(r0, r0 + FRONT_ROW_CHUNK)
        xmb = x_ref[rs, :]
        y = cb_ref[...]
        for back in range(taps):
            y = y + (xe_ref[pad + r0 - back:pad + r0 - back + FRONT_ROW_CHUNK, :]
                     * cw_ref[taps - 1 - back:taps - back, :])
        xcb = _silu(y).astype(BF16)
        xc_ref[rs, :] = xcb
        qbs, kbs, vbs = [], [], []
        for j in range(n_tiles):
            sl = slice(MXU_DIM * j, MXU_DIM * (j + 1))
            qj = jnp.dot(xcb[:, sl], wbd_ref[0, j], preferred_element_type=F32)
            kj = jnp.dot(xcb[:, sl], wbd_ref[1, j], preferred_element_type=F32)
            vj = jnp.dot(xmb[:, sl], wbd_ref[2, j], preferred_element_type=F32)
            qbs.append(qj.astype(BF16))
            kbs.append(kj.astype(BF16))
            vbs.append(vj.astype(BF16))
            k_ref[rs, sl] = (kj * k_scale).astype(BF16)
        qb, kb, vb = (jnp.concatenate(t, axis=1) for t in (qbs, kbs, vbs))
        q_ref[rs, :] = qb
        v_ref[rs, :] = vb
        gate_ref[rs, :] = (bif_ref[...]
                           + jnp.dot(qb, wif_ref[0:width, :], preferred_element_type=F32)
                           + jnp.dot(kb, wif_ref[width:2 * width, :], preferred_element_type=F32)
                           + jnp.dot(vb, wif_ref[2 * width:3 * width, :], preferred_element_type=F32))


def _mlstm_front(rest2, conv_w, conv_b, wbd, wif_pad, bif_pad, s_len, width, tm):
    n = rest2.shape[0]
    halo = 16
    assert conv_w.shape[0] - 1 <= halo
    const = lambda *shape: pl.BlockSpec(shape, lambda i: (0,) * len(shape))
    tok = lambda w: pl.BlockSpec((tm, w), lambda i: (i, 0))
    return pl.pallas_call(
        functools.partial(_mlstm_front_kernel, tiles_per_seq=s_len // tm,
                          k_scale=(width // M_HEADS) ** -0.5),
        grid=(n // tm,),
        in_specs=[tok(width),
                  pl.BlockSpec((halo, width), lambda i: (jnp.maximum(i * (tm // halo) - 1, 0), 0)),
                  const(*conv_w.shape), const(1, width),
                  const(*wbd.shape), const(*wif_pad.shape), const(1, LANES)],
        out_specs=[tok(width), tok(width), tok(width), tok(width), tok(LANES)],
        out_shape=[jax.ShapeDtypeStruct((n, width), BF16)] * 4 + [jax.ShapeDtypeStruct((n, LANES), F32)],
        scratch_shapes=[pltpu.VMEM((tm + halo, width), F32)],
        compiler_params=_cparams(1),
        name="mlstm_front",
    )(rest2, rest2, conv_w, conv_b, wbd, wif_pad, bif_pad)


def _split3(x):
    hi = x.astype(BF16)
    r1 = x - hi.astype(F32)
    mid = r1.astype(BF16)
    lo = (r1 - mid.astype(F32)).astype(BF16)
    return hi, mid, lo


def _mlstm_core_kernel(q_ref, k_ref, v_ref, grow_ref, om_ref, zm_ref, xc_ref, hg_ref, skip_ref,
                       y_ref, c_ref, n_ref, m_ref):
    @pl.when(pl.program_id(2) == 0)
    def _():
        c_ref[...] = jnp.zeros_like(c_ref)
        n_ref[...] = jnp.zeros_like(n_ref)
        m_ref[...] = jnp.zeros_like(m_ref)

    lc = q_ref.shape[1]
    row_id = lax.broadcasted_iota(jnp.int32, (lc, lc), 0)
    col_id = lax.broadcasted_iota(jnp.int32, (lc, lc), 1)
    causal = row_id >= col_id
    upper = (row_id <= col_id).astype(BF16)
    gate_row = lax.broadcasted_iota(jnp.int32, (GATE_ROWS, lc), 0)
    for s in range(q_ref.shape[0]):
        q, k, v = q_ref[s], k_ref[s], v_ref[s]
        pre = grow_ref[s, 0]
        lf = jnp.minimum(pre, 0.0) - jnp.log1p(jnp.exp(-jnp.abs(pre)))
        csum = sum(jnp.dot(part, upper, preferred_element_type=F32) for part in _split3(lf))
        grow = jnp.where(gate_row == 0, pre, csum)
        gcol = grow.T
        i_row, b_row = grow[0:1, :], grow[1:2, :]
        i_col, b_col = gcol[:, 0:1], gcol[:, 1:2]
        m_prev = m_ref[s]
        g = b_row[:, lc - 1:lc]

        dmat = jnp.where(causal, b_col - b_row + i_row, MASKED)
        inter = b_col + m_prev
        m_t = jnp.maximum(inter, jnp.max(dmat, axis=1, keepdims=True))
        qk = lax.dot_general(q, k, (((1,), (1,)), ((), ())), preferred_element_type=F32) * jnp.exp(dmat - m_t)
        w_inter = jnp.exp(inter - m_t)
        c_old = c_ref[s]
        num = (w_inter * jnp.dot(q, c_old.astype(BF16), preferred_element_type=F32)
               + jnp.dot(qk.astype(BF16), v, preferred_element_type=F32))
        den = (w_inter * jnp.sum(q.astype(F32) * n_ref[s], axis=1, keepdims=True)
               + jnp.sum(qk, axis=1, keepdims=True))
        h = num / jnp.maximum(jnp.abs(den), jnp.exp(-m_t))

        m_new = jnp.maximum(g + m_prev, jnp.max(g - b_row + i_row, axis=1, keepdims=True))
        w_s = jnp.exp(g - b_col + i_col - m_new)
        decay = jnp.exp(g + m_prev - m_new)
        kw = (k.astype(F32) * w_s).astype(BF16)
        c_ref[s] = decay * c_old + lax.dot_general(kw, v, (((0,), (0,)), ((), ())), preferred_element_type=F32)
        n_ref[s] = decay * n_ref[s] + jnp.dot(jnp.ones((8, lc), BF16), kw, preferred_element_type=F32)[0:1]
        m_ref[s] = m_new

        hgated = _sigmoid(om_ref[s].astype(F32)) * h
        mu = jnp.mean(hgated, axis=1, keepdims=True)
        cen = hgated - mu
        var = jnp.mean(cen * cen, axis=1, keepdims=True)
        hn = cen * lax.rsqrt(var + EPS) * hg_ref[...]
        zm = zm_ref[s].astype(F32)
        y_ref[s] = ((hn + skip_ref[...] * xc_ref[s].astype(F32)) * _silu(zm)).astype(y_ref.dtype)


def _mlstm_core(q, k, v, grow, rest3, xc, head_norm_g, skip, lc, om_col0, zm_col0):
    b_sz, s_len, width = q.shape
    dh = width // M_HEADS
    grp = MLSTM_GROUP
    assert b_sz % grp == 0
    seq = lambda col0: pl.BlockSpec((grp, lc, dh), lambda b, h, c: (b, c, col0 + h))
    vec = pl.BlockSpec((1, dh), lambda b, h, c: (0, h))
    return pl.pallas_call(
        _mlstm_core_kernel,
        grid=(b_sz // grp, M_HEADS, s_len // lc),
        in_specs=[seq(0), seq(0), seq(0),
                  pl.BlockSpec((grp, 1, GATE_ROWS, lc), lambda b, h, c: (b, h, 0, c)),
                  seq(om_col0 // dh), seq(zm_col0 // dh), seq(0), vec, vec],
        out_specs=seq(0),
        out_shape=jax.ShapeDtypeStruct((b_sz, s_len, width), BF16),
        scratch_shapes=[pltpu.VMEM((grp, dh, dh), F32), pltpu.VMEM((grp, 1, dh), F32),
                        pltpu.VMEM((grp, 1, 1), F32)],
        compiler_params=_cparams(3),
        name="mlstm_core",
    )(q, k, v, grow, rest3, rest3, xc, head_norm_g, skip)


def _merge_out_kernel(ya_ref, za_ref, gates_ref, gb_ref, ym_ref, x_ref, wpa_ref, wpb_ref, wout_ref, gout_ref,
                      out_ref):
    d_model = x_ref.shape[1]
    za = za_ref[...].astype(F32)
    ya = jnp.dot((ya_ref[...].astype(F32) * _silu(za)).astype(BF16), wpa_ref[...],
                 preferred_element_type=F32)
    ym = jnp.dot(ym_ref[...], wpb_ref[...], preferred_element_type=F32)
    gate = _sigmoid(gates_ref[...].astype(F32) + gb_ref[...])
    merged = gate[:, :d_model] * ya + gate[:, d_model:] * ym
    hres = x_ref[...] + jnp.dot(merged.astype(BF16), wout_ref[...], preferred_element_type=F32)
    ms = jnp.mean(hres * hres, axis=-1, keepdims=True)
    out_ref[...] = hres * lax.rsqrt(ms + EPS) * gout_ref[...]


def _merge_out(ya2, proj2, gate_b, ym2, x2, wpa, wpb, wout, gout, za_col0, gates_col0, tm):
    n, d_model = x2.shape
    aw = ya2.shape[1]
    mw = ym2.shape[1]
    tok = lambda w, cb=0: pl.BlockSpec((tm, w), lambda i: (i, cb))
    const = lambda *shape: pl.BlockSpec(shape, lambda i: (0,) * len(shape))
    return pl.pallas_call(
        _merge_out_kernel,
        grid=(n // tm,),
        in_specs=[tok(aw), tok(aw, za_col0 // aw), tok(2 * d_model, gates_col0 // (2 * d_model)),
                  const(1, 2 * d_model), tok(mw), tok(d_model), const(aw, d_model), const(mw, d_model),
                  const(d_model, d_model), const(1, d_model)],
        out_specs=tok(d_model),
        out_shape=jax.ShapeDtypeStruct((n, d_model), F32),
        compiler_params=_cparams(1),
        name="merge_out",
    )(ya2, proj2, proj2, gate_b, ym2, x2, wpa, wpb, wout, gout)


def _layer(h2, b_sz, s_len, norm_in_g, w_in, gate_b, conv_w, conv_b, wq_m, wk_m, wv_m, w_if, b_if,
           head_norm_g, skip_m, w_pa, w_pb, w_out, rel_bias, gout):
    n, d_model = h2.shape
    aw = w_pa.shape[0]
    mw = w_pb.shape[0]
    assert aw == A_HEADS * A_HEAD_DIM and aw == d_model and mw == 2 * d_model
    wb = w_in.astype(BF16)
    xm_col0, zm_col0, om_col0, gates_col0, za_col0 = 0, mw, 2 * mw, 3 * mw, 3 * mw + 2 * d_model
    n_rest = za_col0 + aw
    assert 3 * aw % T_REST_COLS == 0 and aw == T_REST_COLS
    rest_first = 4 * aw // T_REST_COLS
    rest_tiles_before_za = (n_rest - aw) // T_REST_COLS
    za_block = 3 * aw // T_REST_COLS
    g_in = norm_in_g.reshape(1, d_model)
    qkv = _norm_matmul(h2, g_in, wb, lambda j: j, 3 * aw, F32, tm=T_INPROJ_ROWS, tn=T_QKV_COLS, slabs=True)
    rest2 = _norm_matmul(h2, g_in, wb, lambda j: jnp.where(j < rest_tiles_before_za, j + rest_first, za_block),
                         n_rest, BF16, tm=T_INPROJ_ROWS, tn=T_REST_COLS)
    to3 = lambda t: t.reshape(b_sz, s_len, t.shape[-1])

    ya = _attention(qkv.reshape(3 * aw // LANES, b_sz, s_len, LANES), rel_bias, aw)

    assert xm_col0 == 0
    wif_pad = jnp.pad(w_if, ((0, 0), (0, LANES - w_if.shape[1]))).astype(BF16)
    bif_pad = jnp.pad(b_if, (0, LANES - b_if.shape[0])).reshape(1, LANES)
    q, k, v, xc, gate_pre = _mlstm_front(
        rest2, conv_w, conv_b.reshape(1, mw), _block_diag_tiles(jnp.stack([wq_m, wk_m, wv_m])), wif_pad, bif_pad,
        s_len, mw, tm=T_FRONT_ROWS)
    grow = jnp.transpose(gate_pre[:, :2 * M_HEADS].reshape(b_sz, s_len, 2, M_HEADS), (0, 3, 2, 1))
    grow = jnp.pad(grow, ((0, 0), (0, 0), (0, GATE_ROWS - 2), (0, 0)))
    ym = _mlstm_core(to3(q), to3(k), to3(v), grow, to3(rest2), to3(xc),
                     head_norm_g.reshape(1, mw), skip_m.reshape(1, mw), MLSTM_CHUNK, om_col0, zm_col0)

    return _merge_out(ya.reshape(n, aw), rest2, gate_b.reshape(1, 2 * d_model), ym.reshape(n, mw), h2,
                      w_pa.astype(BF16), w_pb.astype(BF16), w_out.astype(BF16), gout.reshape(1, d_model),
                      za_col0, gates_col0, tm=T_MERGE_ROWS)


def kernel(x, norm_in_g, w_in, gate_b, conv_w, conv_b, wq_m, wk_m, wv_m, w_if, b_if, head_norm_g, skip_m,
           w_pa, w_pb, w_out, rel_bias, norm_out_g):
    b_sz, s_len, d_model = x.shape
    depth = w_in.shape[0]
    assert depth == 1
    out = _layer(x.reshape(b_sz * s_len, d_model), b_sz, s_len, norm_in_g[0], w_in[0], gate_b[0], conv_w[0],
                 conv_b[0], wq_m[0], wk_m[0], wv_m[0], w_if[0], b_if[0], head_norm_g[0], skip_m[0], w_pa[0],
                 w_pb[0], w_out[0], rel_bias, norm_out_g)
    return out.reshape(b_sz, s_len, d_model)
```

```python
import functools
import math

import jax
import jax.numpy as jnp
from jax import lax
from jax.experimental import pallas as pl
from jax.experimental.pallas import tpu as pltpu

F32 = jnp.float32
BF16 = jnp.bfloat16

A_HEADS = 16
A_HEAD_DIM = 64
DILATED_PATTERNS = ((128, 1), (512, 4), (2048, 16))
MAX_DISTANCE = 2048
M_HEADS = 4
EPS = 1e-6
MASKED = -1e30
LOG2E = math.log2(math.e)


def _sigmoid(x):
    return 0.5 * jnp.tanh(0.5 * x) + 0.5


def _silu(x):
    h = 0.5 * x
    return h * jnp.tanh(h) + h


LANES = 128
MXU_DIM = 256
VMEM_LIMIT_BYTES = 56 * 1024 * 1024

MLSTM_CHUNK = 256
MLSTM_GROUP = 8
GATE_ROWS = 8
ATTN_PAIRS_PER_STEP = 2
ATTN_BLOCKS_PER_BODY = 8

T_INPROJ_ROWS = 2048
T_INPROJ_COLS = 1024
T_FRONT_ROWS = 512
FRONT_ROW_CHUNK = 256
T_MERGE_ROWS = 512


def _cparams(n_axes):
    return pltpu.CompilerParams(dimension_semantics=("arbitrary",) * n_axes,
                                vmem_limit_bytes=VMEM_LIMIT_BYTES)


def _norm_inproj_kernel(x_ref, g_ref, w_ref, rest_ref, slab_ref, xn_ref, *, n_rest_tiles):
    j = pl.program_id(1)

    @pl.when(j == 0)
    def _():
        xf = x_ref[...]
        ms = jnp.mean(xf * xf, axis=-1, keepdims=True)
        xn_ref[...] = (xf * lax.rsqrt(ms + EPS) * g_ref[...]).astype(BF16)

    @pl.when(j < n_rest_tiles)
    def _():
        rest_ref[...] = jnp.dot(xn_ref[...], w_ref[...], preferred_element_type=F32).astype(rest_ref.dtype)

    @pl.when(j >= n_rest_tiles)
    def _():
        o = jnp.dot(xn_ref[...], w_ref[...], preferred_element_type=F32)
        for c in range(slab_ref.shape[0]):
            slab_ref[c] = o[:, LANES * c:LANES * (c + 1)]


def _norm_inproj(x2, g, w, w_col_block, n_rest, n_slab_cols, tm, tn):
    n, d = x2.shape
    n_rest_tiles = n_rest // tn
    n_slab_tiles = n_slab_cols // tn
    return pl.pallas_call(
        functools.partial(_norm_inproj_kernel, n_rest_tiles=n_rest_tiles),
        grid=(n // tm, n_rest_tiles + n_slab_tiles),
        in_specs=[pl.BlockSpec((tm, d), lambda i, j: (i, 0)),
                  pl.BlockSpec((1, d), lambda i, j: (0, 0)),
                  pl.BlockSpec((d, tn), lambda i, j: (0, w_col_block(j)))],
        out_specs=[pl.BlockSpec((tm, tn), lambda i, j: (i, jnp.minimum(j, n_rest_tiles - 1))),
                   pl.BlockSpec((tn // LANES, tm, LANES), lambda i, j: (jnp.maximum(j - n_rest_tiles, 0), i, 0))],
        out_shape=[jax.ShapeDtypeStruct((n, n_rest), BF16),
                   jax.ShapeDtypeStruct((n_slab_cols // LANES, n, LANES), F32)],
        scratch_shapes=[pltpu.VMEM((tm, d), BF16)],
        compiler_params=_cparams(2),
        name="norm_inproj",
    )(x2, g, w)


def _t5_bucket(dist, n_buckets):
    max_exact = n_buckets // 2
    large = max_exact + (jnp.log(jnp.maximum(dist, max_exact).astype(F32) / max_exact)
                         / math.log(MAX_DISTANCE / max_exact) * (n_buckets - max_exact)).astype(jnp.int32)
    return jnp.where(dist < max_exact, dist, jnp.minimum(large, n_buckets - 1))


def _band_bias_rows(rel_bias, band, dilation):
    delta = jnp.arange(band + 1)
    vals = rel_bias.astype(F32)[_t5_bucket(delta * dilation, rel_bias.shape[0])] * LOG2E
    return jnp.concatenate([vals[::-1].T, jnp.full((rel_bias.shape[1], band - 1), MASKED, F32)], axis=1)


def _attn_kernel(q_ref, kp_ref, kc_ref, vp_ref, vc_ref, brow_ref, o_ref, kk_ref, vv_ref, po_ref, pl_ref, bias_ref,
                 *, patterns, scale):
    n_t, sb_len, _ = q_ref.shape
    first_sb = pl.program_id(2) == 0

    @pl.when(jnp.logical_and(pl.program_id(1) == 0, first_sb))
    def _():
        for t in range(n_t):
            for p, (window, d) in enumerate(patterns):
                band = window // d
                has_prev = lax.broadcasted_iota(jnp.int32, (band, 2 * band), 1) >= band
                for hh in range(2):
                    row = brow_ref[t, 2 * p + hh:2 * p + hh + 1, :]
                    table = pltpu.roll(jnp.broadcast_to(row, (band, 2 * band)), 0, 1, stride=1, stride_axis=0)
                    bias_ref[t, p, 1, hh] = table
                    bias_ref[t, p, 0, hh] = jnp.where(has_prev, table, MASKED)

    kk_ref[:, 0:sb_len, :] = kp_ref[...]
    kk_ref[:, sb_len:2 * sb_len, :] = kc_ref[...]
    vv_ref[:, 0:sb_len, :] = vp_ref[...]
    vv_ref[:, sb_len:2 * sb_len, :] = vc_ref[...]
    heads_per_tile = LANES // A_HEAD_DIM
    assert heads_per_tile == 2

    for p, (window, d) in enumerate(patterns):
        band = window // d
        blocks_per_residue = sb_len // window
        first_head = lax.broadcasted_iota(jnp.int32, (band, LANES), 1) < A_HEAD_DIM
        first_head_keys = lax.broadcasted_iota(jnp.int32, (2 * band, LANES), 1) < A_HEAD_DIM

        def rows(start, n, d=d):
            return pl.ds(start, n, stride=d) if d > 1 else pl.ds(start, n)

        def block(i, carry, p=p, d=d, band=band, blocks_per_residue=blocks_per_residue, first_head=first_head,
                  first_head_keys=first_head_keys, rows=rows):
            r = i // blocks_per_residue
            jb = i % blocks_per_residue
            start = r + jb * (band * d)
            variant = jnp.where(jnp.logical_and(first_sb, jb == 0), 0, 1)
            for t in range(n_t):
                q2 = (q_ref[t, rows(start, band), :] * (scale * LOG2E)).astype(BF16)
                k2 = kk_ref[t, rows(sb_len + start - band * d, 2 * band), :].astype(BF16)
                v2 = vv_ref[t, rows(sb_len + start - band * d, 2 * band), :].astype(BF16)
                ones = jnp.ones_like(v2)
                pvs, ms = [], []
                for hh in range(heads_per_tile):
                    mine = first_head if hh == 0 else jnp.logical_not(first_head)
                    qm = jnp.where(mine, q2, jnp.zeros_like(q2))
                    vx = jnp.where(first_head_keys, v2, ones) if hh == 0 else jnp.where(first_head_keys, ones, v2)
                    s = lax.dot_general(qm, k2, (((1,), (1,)), ((), ())), preferred_element_type=F32)
                    s = s + bias_ref[t, p, variant, hh]
                    m = jnp.max(s, axis=1, keepdims=True)
                    e = jnp.exp2(s - m)
                    pvs.append(jnp.dot(e.astype(BF16), vx, preferred_element_type=F32))
                    ms.append(m)
                num = jnp.where(first_head, pvs[0], pvs[1])
                den = pltpu.roll(jnp.where(first_head, pvs[1], pvs[0]), A_HEAD_DIM, axis=1)
                po_ref[t, p, rows(start, band), :] = num / den
                pl_ref[t, p, rows(start, band), :] = jnp.where(first_head, ms[0], ms[1]) + jnp.log(den) * LOG2E
            return carry

        lax.fori_loop(0, sb_len // band, block, 0, unroll=ATTN_BLOCKS_PER_BODY // n_t)

    chunk = 256
    for t in range(n_t):
        for c in range(sb_len // chunk):
            sl = slice(chunk * c, chunk * (c + 1))
            lse = [pl_ref[t, p, sl, :] for p in range(len(patterns))]
            top = functools.reduce(jnp.maximum, lse)
            wgt = [jnp.exp2(l - top) for l in lse]
            num = sum(w * po_ref[t, p, sl, :] for p, w in enumerate(wgt))
            o_ref[0, sl, LANES * t:LANES * (t + 1)] = (num / sum(wgt)).astype(o_ref.dtype)


def _attention(qkv, rel_bias, aw):
    _, b_sz, s_len, _ = qkv.shape
    sb_len = max(w for w, _ in DILATED_PATTERNS)
    band = DILATED_PATTERNS[0][0] // DILATED_PATTERNS[0][1]
    assert all(w // d == band and sb_len % w == 0 for w, d in DILATED_PATTERNS)
    assert s_len % sb_len == 0 and band % LANES == 0
    n_pairs = aw // LANES
    n_pat = len(DILATED_PATTERNS)
    n_t = ATTN_PAIRS_PER_STEP
    assert n_pairs % n_t == 0 and ATTN_BLOCKS_PER_BODY % n_t == 0
    steps_j = n_pairs // n_t
    brow = jnp.stack([_band_bias_rows(rel_bias, band, d) for _, d in DILATED_PATTERNS])
    brow = brow.reshape(n_pat, n_pairs, 2, 2 * band).transpose(1, 0, 2, 3).reshape(n_pairs, n_pat * 2, 2 * band)

    def blk(which, prev):
        if prev:
            return pl.BlockSpec((n_t, None, sb_len, LANES),
                                lambda j, b, s: (which * steps_j + j, b, jnp.maximum(s - 1, 0), 0))
        return pl.BlockSpec((n_t, None, sb_len, LANES), lambda j, b, s: (which * steps_j + j, b, s, 0))

    return pl.pallas_call(
        functools.partial(_attn_kernel, patterns=DILATED_PATTERNS, scale=A_HEAD_DIM ** -0.5),
        grid=(steps_j, b_sz, s_len // sb_len),
        in_specs=[blk(0, False), blk(1, True), blk(1, False), blk(2, True), blk(2, False),
                  pl.BlockSpec((n_t, n_pat * 2, 2 * band), lambda j, b, s: (j, 0, 0))],
        out_specs=pl.BlockSpec((1, sb_len, n_t * LANES), lambda j, b, s: (b, s, j)),
        out_shape=jax.ShapeDtypeStruct((b_sz, s_len, aw), BF16),
        scratch_shapes=[pltpu.VMEM((n_t, 2 * sb_len, LANES), F32), pltpu.VMEM((n_t, 2 * sb_len, LANES), F32),
                        pltpu.VMEM((n_t, n_pat, sb_len, LANES), F32), pltpu.VMEM((n_t, n_pat, sb_len, LANES), F32),
                        pltpu.VMEM((n_t, n_pat, 2, 2, band, 2 * band), F32)],
        compiler_params=_cparams(3),
        name="dilated_attn",
    )(qkv, qkv, qkv, qkv, qkv, brow)


def _block_diag_tiles(w):
    *lead, nblk, qb, _ = w.shape
    per_tile = MXU_DIM // qb
    wt = w.reshape(*lead, nblk // per_tile, per_tile, qb, 1, qb)
    eye = jnp.eye(per_tile, dtype=w.dtype).reshape(per_tile, 1, per_tile, 1)
    t = (wt * eye).astype(BF16)
    return t.reshape(*lead, nblk // per_tile, MXU_DIM, MXU_DIM)


def _mlstm_front_kernel(x_ref, halo_ref, cw_ref, cb_ref, wbd_ref, wif_ref, bif_ref,
                        q_ref, k_ref, v_ref, xc_ref, gate_ref, xe_ref, *, tiles_per_seq, k_scale):
    tm, width = x_ref.shape
    taps = cw_ref.shape[0]
    pad = halo_ref.shape[0]
    first = (pl.program_id(0) % tiles_per_seq) == 0
    halo = halo_ref[...].astype(F32)
    xe_ref[0:pad, :] = jnp.where(first, jnp.zeros_like(halo), halo)
    xe_ref[pad:pad + tm, :] = x_ref[...].astype(F32)
    n_tiles = width // MXU_DIM
    for r0 in range(0, tm, FRONT_ROW_CHUNK):
        rs = slice(r0, r0 + FRONT_ROW_CHUNK)
        xmb = x_ref[rs, :]
        y = cb_ref[...]
        for back in range(taps):
            y = y + (xe_ref[pad + r0 - back:pad + r0 - back + FRONT_ROW_CHUNK, :]
                     * cw_ref[taps - 1 - back:taps - back, :])
        xcb = _silu(y).astype(BF16)
        xc_ref[rs, :] = xcb
        qbs, kbs, vbs = [], [], []
        for j in range(n_tiles):
            sl = slice(MXU_DIM * j, MXU_DIM * (j + 1))
            qj = jnp.dot(xcb[:, sl], wbd_ref[0, j], preferred_element_type=F32)
            kj = jnp.dot(xcb[:, sl], wbd_ref[1, j], preferred_element_type=F32)
            vj = jnp.dot(xmb[:, sl], wbd_ref[2, j], preferred_element_type=F32)
            qbs.append(qj.astype(BF16))
            kbs.append(kj.astype(BF16))
            vbs.append(vj.astype(BF16))
            k_ref[rs, sl] = (kj * k_scale).astype(BF16)
        qb, kb, vb = (jnp.concatenate(t, axis=1) for t in (qbs, kbs, vbs))
        q_ref[rs, :] = qb
        v_ref[rs, :] = vb
        gate_ref[rs, :] = (bif_ref[...]
                           + jnp.dot(qb, wif_ref[0:width, :], preferred_element_type=F32)
                           + jnp.dot(kb, wif_ref[width:2 * width, :], preferred_element_type=F32)
                           + jnp.dot(vb, wif_ref[2 * width:3 * width, :], preferred_element_type=F32))


def _mlstm_front(rest2, conv_w, conv_b, wbd, wif_pad, bif_pad, s_len, width, tm):
    n = rest2.shape[0]
    halo = 16
    assert conv_w.shape[0] - 1 <= halo
    const = lambda *shape: pl.BlockSpec(shape, lambda i: (0,) * len(shape))
    tok = lambda w: pl.BlockSpec((tm, w), lambda i: (i, 0))
    return pl.pallas_call(
        functools.partial(_mlstm_front_kernel, tiles_per_seq=s_len // tm,
                          k_scale=(width // M_HEADS) ** -0.5),
        grid=(n // tm,),
        in_specs=[tok(width),
                  pl.BlockSpec((halo, width), lambda i: (jnp.maximum(i * (tm // halo) - 1, 0), 0)),
                  const(*conv_w.shape), const(1, width),
                  const(*wbd.shape), const(*wif_pad.shape), const(1, LANES)],
        out_specs=[tok(width), tok(width), tok(width), tok(width), tok(LANES)],
        out_shape=[jax.ShapeDtypeStruct((n, width), BF16)] * 4 + [jax.ShapeDtypeStruct((n, LANES), F32)],
        scratch_shapes=[pltpu.VMEM((tm + halo, width), F32)],
        compiler_params=_cparams(1),
        name="mlstm_front",
    )(rest2, rest2, conv_w, conv_b, wbd, wif_pad, bif_pad)


def _split3(x):
    hi = x.astype(BF16)
    r1 = x - hi.astype(F32)
    mid = r1.astype(BF16)
    lo = (r1 - mid.astype(F32)).astype(BF16)
    return hi, mid, lo


def _mlstm_core_kernel(q_ref, k_ref, v_ref, grow_ref, om_ref, zm_ref, xc_ref, hg_ref, skip_ref,
                       y_ref, c_ref, n_ref, m_ref):
    @pl.when(pl.program_id(2) == 0)
    def _():
        c_ref[...] = jnp.zeros_like(c_ref)
        n_ref[...] = jnp.zeros_like(n_ref)
        m_ref[...] = jnp.zeros_like(m_ref)

    lc = q_ref.shape[1]
    row_id = lax.broadcasted_iota(jnp.int32, (lc, lc), 0)
    col_id = lax.broadcasted_iota(jnp.int32, (lc, lc), 1)
    causal = row_id >= col_id
    upper = (row_id <= col_id).astype(BF16)
    gate_row = lax.broadcasted_iota(jnp.int32, (GATE_ROWS, lc), 0)
    for s in range(q_ref.shape[0]):
        q, k, v = q_ref[s], k_ref[s], v_ref[s]
        pre = grow_ref[s, 0]
        lf = jnp.minimum(pre, 0.0) - jnp.log1p(jnp.exp(-jnp.abs(pre)))
        csum = sum(jnp.dot(part, upper, preferred_element_type=F32) for part in _split3(lf))
        grow = jnp.where(gate_row == 0, pre, csum)
        gcol = grow.T
        i_row, b_row = grow[0:1, :], grow[1:2, :]
        i_col, b_col = gcol[:, 0:1], gcol[:, 1:2]
        m_prev = m_ref[s]
        g = b_row[:, lc - 1:lc]

        dmat = jnp.where(causal, b_col - b_row + i_row, MASKED)
        inter = b_col + m_prev
        m_t = jnp.maximum(inter, jnp.max(dmat, axis=1, keepdims=True))
        qk = lax.dot_general(q, k, (((1,), (1,)), ((), ())), preferred_element_type=F32) * jnp.exp(dmat - m_t)
        w_inter = jnp.exp(inter - m_t)
        c_old = c_ref[s]
        num = (w_inter * jnp.dot(q, c_old.astype(BF16), preferred_element_type=F32)
               + jnp.dot(qk.astype(BF16), v, preferred_element_type=F32))
        den = (w_inter * jnp.sum(q.astype(F32) * n_ref[s], axis=1, keepdims=True)
               + jnp.sum(qk, axis=1, keepdims=True))
        h = num / jnp.maximum(jnp.abs(den), jnp.exp(-m_t))

        m_new = jnp.maximum(g + m_prev, jnp.max(g - b_row + i_row, axis=1, keepdims=True))
        w_s = jnp.exp(g - b_col + i_col - m_new)
        decay = jnp.exp(g + m_prev - m_new)
        kw = (k.astype(F32) * w_s).astype(BF16)
        c_ref[s] = decay * c_old + lax.dot_general(kw, v, (((0,), (0,)), ((), ())), preferred_element_type=F32)
        n_ref[s] = decay * n_ref[s] + jnp.dot(jnp.ones((8, lc), BF16), kw, preferred_element_type=F32)[0:1]
        m_ref[s] = m_new

        hgated = _sigmoid(om_ref[s].astype(F32)) * h
        mu = jnp.mean(hgated, axis=1, keepdims=True)
        cen = hgated - mu
        var = jnp.mean(cen * cen, axis=1, keepdims=True)
        hn = cen * lax.rsqrt(var + EPS) * hg_ref[...]
        zm = zm_ref[s].astype(F32)
        y_ref[s] = ((hn + skip_ref[...] * xc_ref[s].astype(F32)) * _silu(zm)).astype(y_ref.dtype)


def _mlstm_core(q, k, v, grow, rest3, xc, head_norm_g, skip, lc, om_col0, zm_col0):
    b_sz, s_len, width = q.shape
    dh = width // M_HEADS
    grp = MLSTM_GROUP
    assert b_sz % grp == 0
    seq = lambda col0: pl.BlockSpec((grp, lc, dh), lambda b, h, c: (b, c, col0 + h))
    vec = pl.BlockSpec((1, dh), lambda b, h, c: (0, h))
    return pl.pallas_call(
        _mlstm_core_kernel,
        grid=(b_sz // grp, M_HEADS, s_len // lc),
        in_specs=[seq(0), seq(0), seq(0),
                  pl.BlockSpec((grp, 1, GATE_ROWS, lc), lambda b, h, c: (b, h, 0, c)),
                  seq(om_col0 // dh), seq(zm_col0 // dh), seq(0), vec, vec],
        out_specs=seq(0),
        out_shape=jax.ShapeDtypeStruct((b_sz, s_len, width), BF16),
        scratch_shapes=[pltpu.VMEM((grp, dh, dh), F32), pltpu.VMEM((grp, 1, dh), F32),
                        pltpu.VMEM((grp, 1, 1), F32)],
        compiler_params=_cparams(3),
        name="mlstm_core",
    )(q, k, v, grow, rest3, rest3, xc, head_norm_g, skip)


def _merge_out_kernel(ya_ref, za_ref, gates_ref, gb_ref, ym_ref, x_ref, wpa_ref, wpb_ref, wout_ref, gout_ref,
                      out_ref):
    d_model = x_ref.shape[1]
    za = za_ref[...].astype(F32)
    ya = jnp.dot((ya_ref[...].astype(F32) * _silu(za)).astype(BF16), wpa_ref[...],
                 preferred_element_type=F32)
    ym = jnp.dot(ym_ref[...], wpb_ref[...], preferred_element_type=F32)
    gate = _sigmoid(gates_ref[...].astype(F32) + gb_ref[...])
    merged = gate[:, :d_model] * ya + gate[:, d_model:] * ym
    hres = x_ref[...] + jnp.dot(merged.astype(BF16), wout_ref[...], preferred_element_type=F32)
    ms = jnp.mean(hres * hres, axis=-1, keepdims=True)
    out_ref[...] = hres * lax.rsqrt(ms + EPS) * gout_ref[...]


def _merge_out(ya2, proj2, gate_b, ym2, x2, wpa, wpb, wout, gout, za_col0, gates_col0, tm):
    n, d_model = x2.shape
    aw = ya2.shape[1]
    mw = ym2.shape[1]
    tok = lambda w, cb=0: pl.BlockSpec((tm, w), lambda i: (i, cb))
    const = lambda *shape: pl.BlockSpec(shape, lambda i: (0,) * len(shape))
    return pl.pallas_call(
        _merge_out_kernel,
        grid=(n // tm,),
        in_specs=[tok(aw), tok(aw, za_col0 // aw), tok(2 * d_model, gates_col0 // (2 * d_model)),
                  const(1, 2 * d_model), tok(mw), tok(d_model), const(aw, d_model), const(mw, d_model),
                  const(d_model, d_model), const(1, d_model)],
        out_specs=tok(d_model),
        out_shape=jax.ShapeDtypeStruct((n, d_model), F32),
        compiler_params=_cparams(1),
        name="merge_out",
    )(ya2, proj2, proj2, gate_b, ym2, x2, wpa, wpb, wout, gout)


def _layer(h2, b_sz, s_len, norm_in_g, w_in, gate_b, conv_w, conv_b, wq_m, wk_m, wv_m, w_if, b_if,
           head_norm_g, skip_m, w_pa, w_pb, w_out, rel_bias, gout):
    n, d_model = h2.shape
    aw = w_pa.shape[0]
    mw = w_pb.shape[0]
    assert aw == A_HEADS * A_HEAD_DIM and aw == d_model and mw == 2 * d_model
    wb = w_in.astype(BF16)
    xm_col0, zm_col0, om_col0, gates_col0, za_col0 = 0, mw, 2 * mw, 3 * mw, 3 * mw + 2 * d_model
    n_rest = za_col0 + aw
    tn = T_INPROJ_COLS
    assert aw == tn
    rest_first = 4 * aw // tn
    rest_tiles_before_za = (n_rest - aw) // tn
    n_rest_tiles = n_rest // tn
    za_block = 3 * aw // tn

    def w_col_block(j):
        return jnp.where(j < rest_tiles_before_za, j + rest_first,
                         jnp.where(j < n_rest_tiles, za_block, j - n_rest_tiles))

    rest2, qkv = _norm_inproj(h2, norm_in_g.reshape(1, d_model), wb, w_col_block, n_rest, 3 * aw,
                              tm=T_INPROJ_ROWS, tn=tn)
    to3 = lambda t: t.reshape(b_sz, s_len, t.shape[-1])

    ya = _attention(qkv.reshape(3 * aw // LANES, b_sz, s_len, LANES), rel_bias, aw)

    assert xm_col0 == 0
    wif_pad = jnp.pad(w_if, ((0, 0), (0, LANES - w_if.shape[1]))).astype(BF16)
    bif_pad = jnp.pad(b_if, (0, LANES - b_if.shape[0])).reshape(1, LANES)
    q, k, v, xc, gate_pre = _mlstm_front(
        rest2, conv_w, conv_b.reshape(1, mw), _block_diag_tiles(jnp.stack([wq_m, wk_m, wv_m])), wif_pad, bif_pad,
        s_len, mw, tm=T_FRONT_ROWS)
    grow = jnp.transpose(gate_pre[:, :2 * M_HEADS].reshape(b_sz, s_len, 2, M_HEADS), (0, 3, 2, 1))
    grow = jnp.pad(grow, ((0, 0), (0, 0), (0, GATE_ROWS - 2), (0, 0)))
    ym = _mlstm_core(to3(q), to3(k), to3(v), grow, to3(rest2), to3(xc),
                     head_norm_g.reshape(1, mw), skip_m.reshape(1, mw), MLSTM_CHUNK, om_col0, zm_col0)

    return _merge_out(ya.reshape(n, aw), rest2, gate_b.reshape(1, 2 * d_model), ym.reshape(n, mw), h2,
                      w_pa.astype(BF16), w_pb.astype(BF16), w_out.astype(BF16), gout.reshape(1, d_model),
                      za_col0, gates_col0, tm=T_MERGE_ROWS)


def kernel(x, norm_in_g, w_in, gate_b, conv_w, conv_b, wq_m, wk_m, wv_m, w_if, b_if, head_norm_g, skip_m,
           w_pa, w_pb, w_out, rel_bias, norm_out_g):
    b_sz, s_len, d_model = x.shape
    depth = w_in.shape[0]
    assert depth == 1
    out = _layer(x.reshape(b_sz * s_len, d_model), b_sz, s_len, norm_in_g[0], w_in[0], gate_b[0], conv_w[0],
                 conv_b[0], wq_m[0], wk_m[0], wv_m[0], w_if[0], b_if[0], head_norm_g[0], skip_m[0], w_pa[0],
                 w_pb[0], w_out[0], rel_bias, norm_out_g)
    return out.reshape(b_sz, s_len, d_model)
```

```python
import functools
import math

import jax
import jax.numpy as jnp
from jax import lax
from jax.experimental import pallas as pl
from jax.experimental.pallas import tpu as pltpu

F32 = jnp.float32
BF16 = jnp.bfloat16

A_HEADS = 16
A_HEAD_DIM = 64
DILATED_PATTERNS = ((128, 1), (512, 4), (2048, 16))
MAX_DISTANCE = 2048
M_HEADS = 4
EPS = 1e-6
MASKED = -1e30
LOG2E = math.log2(math.e)


def _sigmoid(x):
    return 0.5 * jnp.tanh(0.5 * x) + 0.5


def _silu(x):
    h = 0.5 * x
    return h * jnp.tanh(h) + h


LANES = 128
MXU_DIM = 256
VMEM_LIMIT_BYTES = 56 * 1024 * 1024

MLSTM_CHUNK = 256
MLSTM_GROUP = 8
GATE_ROWS = 8
ATTN_PAIRS_PER_STEP = 2
ATTN_BLOCKS_PER_BODY = 8

T_INPROJ_ROWS = 2048
T_INPROJ_COLS = 1024
T_FRONT_ROWS = 512
FRONT_ROW_CHUNK = 256
T_MERGE_ROWS = 512


def _cparams(n_axes):
    return pltpu.CompilerParams(dimension_semantics=("arbitrary",) * n_axes,
                                vmem_limit_bytes=VMEM_LIMIT_BYTES)


def _norm_inproj_kernel(x_ref, g_ref, w_ref, rest_ref, slab_ref, xn_ref, *, n_rest_tiles):
    j = pl.program_id(1)

    @pl.when(j == 0)
    def _():
        xf = x_ref[...]
        ms = jnp.mean(xf * xf, axis=-1, keepdims=True)
        xn_ref[...] = (xf * lax.rsqrt(ms + EPS) * g_ref[...]).astype(BF16)

    @pl.when(j < n_rest_tiles)
    def _():
        rest_ref[...] = jnp.dot(xn_ref[...], w_ref[...], preferred_element_type=F32).astype(rest_ref.dtype)

    @pl.when(j >= n_rest_tiles)
    def _():
        o = jnp.dot(xn_ref[...], w_ref[...], preferred_element_type=F32)
        for c in range(slab_ref.shape[0]):
            slab_ref[c] = o[:, LANES * c:LANES * (c + 1)]


def _norm_inproj(x2, g, w, w_col_block, n_rest, n_slab_cols, tm, tn):
    n, d = x2.shape
    n_rest_tiles = n_rest // tn
    n_slab_tiles = n_slab_cols // tn
    return pl.pallas_call(
        functools.partial(_norm_inproj_kernel, n_rest_tiles=n_rest_tiles),
        grid=(n // tm, n_rest_tiles + n_slab_tiles),
        in_specs=[pl.BlockSpec((tm, d), lambda i, j: (i, 0)),
                  pl.BlockSpec((1, d), lambda i, j: (0, 0)),
                  pl.BlockSpec((d, tn), lambda i, j: (0, w_col_block(j)))],
        out_specs=[pl.BlockSpec((tm, tn), lambda i, j: (i, jnp.minimum(j, n_rest_tiles - 1))),
                   pl.BlockSpec((tn // LANES, tm, LANES), lambda i, j: (jnp.maximum(j - n_rest_tiles, 0), i, 0))],
        out_shape=[jax.ShapeDtypeStruct((n, n_rest), BF16),
                   jax.ShapeDtypeStruct((n_slab_cols // LANES, n, LANES), F32)],
        scratch_shapes=[pltpu.VMEM((tm, d), BF16)],
        compiler_params=_cparams(2),
        name="norm_inproj",
    )(x2, g, w)


def _t5_bucket(dist, n_buckets):
    max_exact = n_buckets // 2
    large = max_exact + (jnp.log(jnp.maximum(dist, max_exact).astype(F32) / max_exact)
                         / math.log(MAX_DISTANCE / max_exact) * (n_buckets - max_exact)).astype(jnp.int32)
    return jnp.where(dist < max_exact, dist, jnp.minimum(large, n_buckets - 1))


def _band_bias_rows(rel_bias, band, dilation):
    delta = jnp.arange(band + 1)
    vals = rel_bias.astype(F32)[_t5_bucket(delta * dilation, rel_bias.shape[0])] * LOG2E
    return jnp.concatenate([vals[::-1].T, jnp.full((rel_bias.shape[1], band - 1), MASKED, F32)], axis=1)


def _attn_kernel(q_ref, kp_ref, kc_ref, vp_ref, vc_ref, brow_ref, o_ref, kk_ref, vv_ref, po_ref, pl_ref, bias_ref,
                 *, patterns, scale):
    n_t, sb_len, _ = q_ref.shape
    first_sb = pl.program_id(2) == 0

    @pl.when(jnp.logical_and(pl.program_id(1) == 0, first_sb))
    def _():
        for t in range(n_t):
            for p, (window, d) in enumerate(patterns):
                band = window // d
                has_prev = lax.broadcasted_iota(jnp.int32, (band, 2 * band), 1) >= band
                for hh in range(2):
                    row = brow_ref[t, 2 * p + hh:2 * p + hh + 1, :]
                    table = pltpu.roll(jnp.broadcast_to(row, (band, 2 * band)), 0, 1, stride=1, stride_axis=0)
                    bias_ref[t, p, 1, hh] = table
                    bias_ref[t, p, 0, hh] = jnp.where(has_prev, table, MASKED)

    kk_ref[:, 0:sb_len, :] = kp_ref[...]
    kk_ref[:, sb_len:2 * sb_len, :] = kc_ref[...]
    vv_ref[:, 0:sb_len, :] = vp_ref[...]
    vv_ref[:, sb_len:2 * sb_len, :] = vc_ref[...]
    heads_per_tile = LANES // A_HEAD_DIM
    assert heads_per_tile == 2

    for p, (window, d) in enumerate(patterns):
        band = window // d
        blocks_per_residue = sb_len // window
        first_head = lax.broadcasted_iota(jnp.int32, (band, LANES), 1) < A_HEAD_DIM
        first_head_keys = lax.broadcasted_iota(jnp.int32, (2 * band, LANES), 1) < A_HEAD_DIM

        def rows(start, n, d=d):
            return pl.ds(start, n, stride=d) if d > 1 else pl.ds(start, n)

        def block(i, carry, p=p, d=d, band=band, blocks_per_residue=blocks_per_residue, first_head=first_head,
                  first_head_keys=first_head_keys, rows=rows):
            r = i // blocks_per_residue
            jb = i % blocks_per_residue
            start = r + jb * (band * d)
            variant = jnp.where(jnp.logical_and(first_sb, jb == 0), 0, 1)
            for t in range(n_t):
                q2 = (q_ref[t, rows(start, band), :] * (scale * LOG2E)).astype(BF16)
                k2 = kk_ref[t, rows(sb_len + start - band * d, 2 * band), :].astype(BF16)
                v2 = vv_ref[t, rows(sb_len + start - band * d, 2 * band), :].astype(BF16)
                ones = jnp.ones_like(v2)
                pvs, ms = [], []
                for hh in range(heads_per_tile):
                    mine = first_head if hh == 0 else jnp.logical_not(first_head)
                    qm = jnp.where(mine, q2, jnp.zeros_like(q2))
                    vx = jnp.where(first_head_keys, v2, ones) if hh == 0 else jnp.where(first_head_keys, ones, v2)
                    s = lax.dot_general(qm, k2, (((1,), (1,)), ((), ())), preferred_element_type=F32)
                    s = s + bias_ref[t, p, variant, hh]
                    m = jnp.max(s, axis=1, keepdims=True)
                    e = jnp.exp2(s - m)
                    pvs.append(jnp.dot(e.astype(BF16), vx, preferred_element_type=F32))
                    ms.append(m)
                num = jnp.where(first_head, pvs[0], pvs[1])
                den = pltpu.roll(jnp.where(first_head, pvs[1], pvs[0]), A_HEAD_DIM, axis=1)
                po_ref[t, p, rows(start, band), :] = num / den
                pl_ref[t, p, rows(start, band), :] = jnp.where(first_head, ms[0], ms[1]) + jnp.log(den) * LOG2E
            return carry

        lax.fori_loop(0, sb_len // band, block, 0, unroll=ATTN_BLOCKS_PER_BODY // n_t)

    chunk = 256
    for t in range(n_t):
        for c in range(sb_len // chunk):
            sl = slice(chunk * c, chunk * (c + 1))
            lse = [pl_ref[t, p, sl, :] for p in range(len(patterns))]
            top = functools.reduce(jnp.maximum, lse)
            wgt = [jnp.exp2(l - top) for l in lse]
            num = sum(w * po_ref[t, p, sl, :] for p, w in enumerate(wgt))
            o_ref[0, sl, LANES * t:LANES * (t + 1)] = (num / sum(wgt)).astype(o_ref.dtype)


def _attention(qkv, rel_bias, aw):
    _, b_sz, s_len, _ = qkv.shape
    sb_len = max(w for w, _ in DILATED_PATTERNS)
    band = DILATED_PATTERNS[0][0] // DILATED_PATTERNS[0][1]
    assert all(w // d == band and sb_len % w == 0 for w, d in DILATED_PATTERNS)
    assert s_len % sb_len == 0 and band % LANES == 0
    n_pairs = aw // LANES
    n_pat = len(DILATED_PATTERNS)
    n_t = ATTN_PAIRS_PER_STEP
    assert n_pairs % n_t == 0 and ATTN_BLOCKS_PER_BODY % n_t == 0
    steps_j = n_pairs // n_t
    brow = jnp.stack([_band_bias_rows(rel_bias, band, d) for _, d in DILATED_PATTERNS])
    brow = brow.reshape(n_pat, n_pairs, 2, 2 * band).transpose(1, 0, 2, 3).reshape(n_pairs, n_pat * 2, 2 * band)

    def blk(which, prev):
        if prev:
            return pl.BlockSpec((n_t, None, sb_len, LANES),
                                lambda j, b, s: (which * steps_j + j, b, jnp.maximum(s - 1, 0), 0))
        return pl.BlockSpec((n_t, None, sb_len, LANES), lambda j, b, s: (which * steps_j + j, b, s, 0))

    return pl.pallas_call(
        functools.partial(_attn_kernel, patterns=DILATED_PATTERNS, scale=A_HEAD_DIM ** -0.5),
        grid=(steps_j, b_sz, s_len // sb_len),
        in_specs=[blk(0, False), blk(1, True), blk(1, False), blk(2, True), blk(2, False),
                  pl.BlockSpec((n_t, n_pat * 2, 2 * band), lambda j, b, s: (j, 0, 0))],
        out_specs=pl.BlockSpec((1, sb_len, n_t * LANES), lambda j, b, s: (b, s, j)),
        out_shape=jax.ShapeDtypeStruct((b_sz, s_len, aw), BF16),
        scratch_shapes=[pltpu.VMEM((n_t, 2 * sb_len, LANES), F32), pltpu.VMEM((n_t, 2 * sb_len, LANES), F32),
                        pltpu.VMEM((n_t, n_pat, sb_len, LANES), F32), pltpu.VMEM((n_t, n_pat, sb_len, LANES), F32),
                        pltpu.VMEM((n_t, n_pat, 2, 2, band, 2 * band), F32)],
        compiler_params=_cparams(3),
        name="dilated_attn",
    )(qkv, qkv, qkv, qkv, qkv, brow)


def _mlstm_front_kernel(x_ref, halo_ref, cw_ref, cb_ref, wblk_ref, wif_ref, bif_ref,
                        q_ref, k_ref, v_ref, xc_ref, gate_ref, xe_ref, wbd_ref, *, tiles_per_seq, k_scale, blk):
    tm, width = x_ref.shape
    taps = cw_ref.shape[0]
    pad = halo_ref.shape[0]

    @pl.when(pl.program_id(0) == 0)
    def _():
        r_id = lax.broadcasted_iota(jnp.int32, (MXU_DIM, MXU_DIM), 0)
        c_id = lax.broadcasted_iota(jnp.int32, (MXU_DIM, MXU_DIM), 1)
        same_block = (r_id // blk) == (c_id // blk)
        o_id = lax.broadcasted_iota(jnp.int32, (LANES, MXU_DIM), 0)
        spread = (o_id == lax.broadcasted_iota(jnp.int32, (LANES, MXU_DIM), 1) % blk).astype(BF16)
        for kind in range(wblk_ref.shape[0]):
            for j in range(width // MXU_DIM):
                rows = wblk_ref[kind, MXU_DIM * j:MXU_DIM * (j + 1), :].astype(BF16)
                tile = jnp.dot(rows, spread, preferred_element_type=F32)
                wbd_ref[kind, j] = jnp.where(same_block, tile, 0.0).astype(BF16)

    first = (pl.program_id(0) % tiles_per_seq) == 0
    halo = halo_ref[...].astype(F32)
    xe_ref[0:pad, :] = jnp.where(first, jnp.zeros_like(halo), halo)
    xe_ref[pad:pad + tm, :] = x_ref[...].astype(F32)
    n_tiles = width // MXU_DIM
    for r0 in range(0, tm, FRONT_ROW_CHUNK):
        rs = slice(r0, r0 + FRONT_ROW_CHUNK)
        xmb = x_ref[rs, :]
        y = cb_ref[...]
        for back in range(taps):
            y = y + (xe_ref[pad + r0 - back:pad + r0 - back + FRONT_ROW_CHUNK, :]
                     * cw_ref[taps - 1 - back:taps - back, :])
        xcb = _silu(y).astype(BF16)
        xc_ref[rs, :] = xcb
        qbs, kbs, vbs = [], [], []
        for j in range(n_tiles):
            sl = slice(MXU_DIM * j, MXU_DIM * (j + 1))
            qj = jnp.dot(xcb[:, sl], wbd_ref[0, j], preferred_element_type=F32)
            kj = jnp.dot(xcb[:, sl], wbd_ref[1, j], preferred_element_type=F32)
            vj = jnp.dot(xmb[:, sl], wbd_ref[2, j], preferred_element_type=F32)
            qbs.append(qj.astype(BF16))
            kbs.append(kj.astype(BF16))
            vbs.append(vj.astype(BF16))
            k_ref[rs, sl] = (kj * k_scale).astype(BF16)
        qb, kb, vb = (jnp.concatenate(t, axis=1) for t in (qbs, kbs, vbs))
        q_ref[rs, :] = qb
        v_ref[rs, :] = vb
        gate_ref[rs, :] = (bif_ref[...]
                           + jnp.dot(qb, wif_ref[0:width, :], preferred_element_type=F32)
                           + jnp.dot(kb, wif_ref[width:2 * width, :], preferred_element_type=F32)
                           + jnp.dot(vb, wif_ref[2 * width:3 * width, :], preferred_element_type=F32))


def _mlstm_front(rest2, conv_w, conv_b, w_blocks, wif_pad, bif_pad, s_len, width, tm):
    n = rest2.shape[0]
    halo = 16
    n_kinds, n_blocks, blk, _ = w_blocks.shape
    assert conv_w.shape[0] - 1 <= halo and n_blocks * blk == width and MXU_DIM % blk == 0
    wblk = jnp.pad(w_blocks.reshape(n_kinds, width, blk), ((0, 0), (0, 0), (0, LANES - blk)))
    const = lambda *shape: pl.BlockSpec(shape, lambda i: (0,) * len(shape))
    tok = lambda w: pl.BlockSpec((tm, w), lambda i: (i, 0))
    return pl.pallas_call(
        functools.partial(_mlstm_front_kernel, tiles_per_seq=s_len // tm,
                          k_scale=(width // M_HEADS) ** -0.5, blk=blk),
        grid=(n // tm,),
        in_specs=[tok(width),
                  pl.BlockSpec((halo, width), lambda i: (jnp.maximum(i * (tm // halo) - 1, 0), 0)),
                  const(*conv_w.shape), const(1, width),
                  const(*wblk.shape), const(*wif_pad.shape), const(1, LANES)],
        out_specs=[tok(width), tok(width), tok(width), tok(width), tok(LANES)],
        out_shape=[jax.ShapeDtypeStruct((n, width), BF16)] * 4 + [jax.ShapeDtypeStruct((n, LANES), F32)],
        scratch_shapes=[pltpu.VMEM((tm + halo, width), F32),
                        pltpu.VMEM((n_kinds, width // MXU_DIM, MXU_DIM, MXU_DIM), BF16)],
        compiler_params=_cparams(1),
        name="mlstm_front",
    )(rest2, rest2, conv_w, conv_b, wblk, wif_pad, bif_pad)


def _split3(x):
    hi = x.astype(BF16)
    r1 = x - hi.astype(F32)
    mid = r1.astype(BF16)
    lo = (r1 - mid.astype(F32)).astype(BF16)
    return hi, mid, lo


def _mlstm_core_kernel(q_ref, k_ref, v_ref, grow_ref, om_ref, zm_ref, xc_ref, hg_ref, skip_ref,
                       y_ref, c_ref, n_ref, m_ref):
    @pl.when(pl.program_id(2) == 0)
    def _():
        c_ref[...] = jnp.zeros_like(c_ref)
        n_ref[...] = jnp.zeros_like(n_ref)
        m_ref[...] = jnp.zeros_like(m_ref)

    lc = q_ref.shape[1]
    row_id = lax.broadcasted_iota(jnp.int32, (lc, lc), 0)
    col_id = lax.broadcasted_iota(jnp.int32, (lc, lc), 1)
    causal = row_id >= col_id
    upper = (row_id <= col_id).astype(BF16)
    gate_row = lax.broadcasted_iota(jnp.int32, (GATE_ROWS, lc), 0)
    for s in range(q_ref.shape[0]):
        q, k, v = q_ref[s], k_ref[s], v_ref[s]
        pre = grow_ref[s, 0]
        lf = jnp.minimum(pre, 0.0) - jnp.log1p(jnp.exp(-jnp.abs(pre)))
        csum = sum(jnp.dot(part, upper, preferred_element_type=F32) for part in _split3(lf))
        grow = jnp.where(gate_row == 0, pre, csum)
        gcol = grow.T
        i_row, b_row = grow[0:1, :], grow[1:2, :]
        i_col, b_col = gcol[:, 0:1], gcol[:, 1:2]
        m_prev = m_ref[s]
        g = b_row[:, lc - 1:lc]

        dmat = jnp.where(causal, b_col - b_row + i_row, MASKED)
        inter = b_col + m_prev
        m_t = jnp.maximum(inter, jnp.max(dmat, axis=1, keepdims=True))
        qk = lax.dot_general(q, k, (((1,), (1,)), ((), ())), preferred_element_type=F32) * jnp.exp(dmat - m_t)
        w_inter = jnp.exp(inter - m_t)
        c_old = c_ref[s]
        num = (w_inter * jnp.dot(q, c_old.astype(BF16), preferred_element_type=F32)
               + jnp.dot(qk.astype(BF16), v, preferred_element_type=F32))
        den = (w_inter * jnp.sum(q.astype(F32) * n_ref[s], axis=1, keepdims=True)
               + jnp.sum(qk, axis=1, keepdims=True))
        h = num / jnp.maximum(jnp.abs(den), jnp.exp(-m_t))

        m_new = jnp.maximum(g + m_prev, jnp.max(g - b_row + i_row, axis=1, keepdims=True))
        w_s = jnp.exp(g - b_col + i_col - m_new)
        decay = jnp.exp(g + m_prev - m_new)
        kw = (k.astype(F32) * w_s).astype(BF16)
        c_ref[s] = decay * c_old + lax.dot_general(kw, v, (((0,), (0,)), ((), ())), preferred_element_type=F32)
        n_ref[s] = decay * n_ref[s] + jnp.dot(jnp.ones((8, lc), BF16), kw, preferred_element_type=F32)[0:1]
        m_ref[s] = m_new

        hgated = _sigmoid(om_ref[s].astype(F32)) * h
        mu = jnp.mean(hgated, axis=1, keepdims=True)
        cen = hgated - mu
        var = jnp.mean(cen * cen, axis=1, keepdims=True)
        hn = cen * lax.rsqrt(var + EPS) * hg_ref[...]
        zm = zm_ref[s].astype(F32)
        y_ref[s] = ((hn + skip_ref[...] * xc_ref[s].astype(F32)) * _silu(zm)).astype(y_ref.dtype)


def _mlstm_core(q, k, v, grow, rest3, xc, head_norm_g, skip, lc, om_col0, zm_col0):
    b_sz, s_len, width = q.shape
    dh = width // M_HEADS
    grp = MLSTM_GROUP
    assert b_sz % grp == 0
    seq = lambda col0: pl.BlockSpec((grp, lc, dh), lambda b, h, c: (b, c, col0 + h))
    vec = pl.BlockSpec((1, dh), lambda b, h, c: (0, h))
    return pl.pallas_call(
        _mlstm_core_kernel,
        grid=(b_sz // grp, M_HEADS, s_len // lc),
        in_specs=[seq(0), seq(0), seq(0),
                  pl.BlockSpec((grp, 1, GATE_ROWS, lc), lambda b, h, c: (b, h, 0, c)),
                  seq(om_col0 // dh), seq(zm_col0 // dh), seq(0), vec, vec],
        out_specs=seq(0),
        out_shape=jax.ShapeDtypeStruct((b_sz, s_len, width), BF16),
        scratch_shapes=[pltpu.VMEM((grp, dh, dh), F32), pltpu.VMEM((grp, 1, dh), F32),
                        pltpu.VMEM((grp, 1, 1), F32)],
        compiler_params=_cparams(3),
        name="mlstm_core",
    )(q, k, v, grow, rest3, rest3, xc, head_norm_g, skip)


def _merge_out_kernel(ya_ref, za_ref, gates_ref, gb_ref, ym_ref, x_ref, wpa_ref, wpb_ref, wout_ref, gout_ref,
                      out_ref):
    d_model = x_ref.shape[1]
    za = za_ref[...].astype(F32)
    ya = jnp.dot((ya_ref[...].astype(F32) * _silu(za)).astype(BF16), wpa_ref[...],
                 preferred_element_type=F32)
    ym = jnp.dot(ym_ref[...], wpb_ref[...], preferred_element_type=F32)
    gate = _sigmoid(gates_ref[...].astype(F32) + gb_ref[...])
    merged = gate[:, :d_model] * ya + gate[:, d_model:] * ym
    hres = x_ref[...] + jnp.dot(merged.astype(BF16), wout_ref[...], preferred_element_type=F32)
    ms = jnp.mean(hres * hres, axis=-1, keepdims=True)
    out_ref[...] = hres * lax.rsqrt(ms + EPS) * gout_ref[...]


def _merge_out(ya2, proj2, gate_b, ym2, x2, wpa, wpb, wout, gout, za_col0, gates_col0, tm):
    n, d_model = x2.shape
    aw = ya2.shape[1]
    mw = ym2.shape[1]
    tok = lambda w, cb=0: pl.BlockSpec((tm, w), lambda i: (i, cb))
    const = lambda *shape: pl.BlockSpec(shape, lambda i: (0,) * len(shape))
    return pl.pallas_call(
        _merge_out_kernel,
        grid=(n // tm,),
        in_specs=[tok(aw), tok(aw, za_col0 // aw), tok(2 * d_model, gates_col0 // (2 * d_model)),
                  const(1, 2 * d_model), tok(mw), tok(d_model), const(aw, d_model), const(mw, d_model),
                  const(d_model, d_model), const(1, d_model)],
        out_specs=tok(d_model),
        out_shape=jax.ShapeDtypeStruct((n, d_model), F32),
        compiler_params=_cparams(1),
        name="merge_out",
    )(ya2, proj2, proj2, gate_b, ym2, x2, wpa, wpb, wout, gout)


def _layer(h2, b_sz, s_len, norm_in_g, w_in, gate_b, conv_w, conv_b, wq_m, wk_m, wv_m, w_if, b_if,
           head_norm_g, skip_m, w_pa, w_pb, w_out, rel_bias, gout):
    n, d_model = h2.shape
    aw = w_pa.shape[0]
    mw = w_pb.shape[0]
    assert aw == A_HEADS * A_HEAD_DIM and aw == d_model and mw == 2 * d_model
    wb = w_in.astype(BF16)
    xm_col0, zm_col0, om_col0, gates_col0, za_col0 = 0, mw, 2 * mw, 3 * mw, 3 * mw + 2 * d_model
    n_rest = za_col0 + aw
    tn = T_INPROJ_COLS
    assert aw == tn
    rest_first = 4 * aw // tn
    rest_tiles_before_za = (n_rest - aw) // tn
    n_rest_tiles = n_rest // tn
    za_block = 3 * aw // tn

    def w_col_block(j):
        return jnp.where(j < rest_tiles_before_za, j + rest_first,
                         jnp.where(j < n_rest_tiles, za_block, j - n_rest_tiles))

    rest2, qkv = _norm_inproj(h2, norm_in_g.reshape(1, d_model), wb, w_col_block, n_rest, 3 * aw,
                              tm=T_INPROJ_ROWS, tn=tn)
    to3 = lambda t: t.reshape(b_sz, s_len, t.shape[-1])

    ya = _attention(qkv.reshape(3 * aw // LANES, b_sz, s_len, LANES), rel_bias, aw)

    assert xm_col0 == 0
    wif_pad = jnp.pad(w_if, ((0, 0), (0, LANES - w_if.shape[1]))).astype(BF16)
    bif_pad = jnp.pad(b_if, (0, LANES - b_if.shape[0])).reshape(1, LANES)
    q, k, v, xc, gate_pre = _mlstm_front(
        rest2, conv_w, conv_b.reshape(1, mw), jnp.stack([wq_m, wk_m, wv_m]), wif_pad, bif_pad,
        s_len, mw, tm=T_FRONT_ROWS)
    grow = jnp.transpose(gate_pre[:, :2 * M_HEADS].reshape(b_sz, s_len, 2, M_HEADS), (0, 3, 2, 1))
    grow = jnp.pad(grow, ((0, 0), (0, 0), (0, GATE_ROWS - 2), (0, 0)))
    ym = _mlstm_core(to3(q), to3(k), to3(v), grow, to3(rest2), to3(xc),
                     head_norm_g.reshape(1, mw), skip_m.reshape(1, mw), MLSTM_CHUNK, om_col0, zm_col0)

    return _merge_out(ya.reshape(n, aw), rest2, gate_b.reshape(1, 2 * d_model), ym.reshape(n, mw), h2,
                      w_pa.astype(BF16), w_pb.astype(BF16), w_out.astype(BF16), gout.reshape(1, d_model),
                      za_col0, gates_col0, tm=T_MERGE_ROWS)


def kernel(x, norm_in_g, w_in, gate_b, conv_w, conv_b, wq_m, wk_m, wv_m, w_if, b_if, head_norm_g, skip_m,
           w_pa, w_pb, w_out, rel_bias, norm_out_g):
    b_sz, s_len, d_model = x.shape
    depth = w_in.shape[0]
    assert depth == 1
    out = _layer(x.reshape(b_sz * s_len, d_model), b_sz, s_len, norm_in_g[0], w_in[0], gate_b[0], conv_w[0],
                 conv_b[0], wq_m[0], wk_m[0], wv_m[0], w_if[0], b_if[0], head_norm_g[0], skip_m[0], w_pa[0],
                 w_pb[0], w_out[0], rel_bias, norm_out_g)
    return out.reshape(b_sz, s_len, d_model)
```

```python
import functools
import math

import jax
import jax.numpy as jnp
from jax import lax
from jax.experimental import pallas as pl
from jax.experimental.pallas import tpu as pltpu

F32 = jnp.float32
BF16 = jnp.bfloat16

A_HEADS = 16
A_HEAD_DIM = 64
DILATED_PATTERNS = ((128, 1), (512, 4), (2048, 16))
MAX_DISTANCE = 2048
M_HEADS = 4
EPS = 1e-6
MASKED = -1e30
LOG2E = math.log2(math.e)


def _sigmoid(x):
    return 0.5 * jnp.tanh(0.5 * x) + 0.5


def _silu(x):
    h = 0.5 * x
    return h * jnp.tanh(h) + h


LANES = 128
MXU_DIM = 256
VMEM_LIMIT_BYTES = 56 * 1024 * 1024

MLSTM_CHUNK = 256
MLSTM_GROUP = 8
GATE_ROWS = 8
ATTN_PAIRS_PER_STEP = 2
ATTN_BLOCKS_PER_BODY = 8

T_INPROJ_ROWS = 2048
T_INPROJ_COLS = 1024
T_FRONT_ROWS = 512
FRONT_ROW_CHUNK = 256
T_MERGE_ROWS = 512


def _cparams(n_axes):
    return pltpu.CompilerParams(dimension_semantics=("arbitrary",) * n_axes,
                                vmem_limit_bytes=VMEM_LIMIT_BYTES)


def _norm_inproj_kernel(x_ref, g_ref, w_ref, rest_ref, slab_ref, xn_ref, *, n_rest_tiles):
    j = pl.program_id(1)

    @pl.when(j == 0)
    def _():
        xf = x_ref[...]
        ms = jnp.mean(xf * xf, axis=-1, keepdims=True)
        xn_ref[...] = (xf * lax.rsqrt(ms + EPS) * g_ref[...]).astype(BF16)

    @pl.when(j < n_rest_tiles)
    def _():
        rest_ref[...] = jnp.dot(xn_ref[...], w_ref[...], preferred_element_type=F32).astype(rest_ref.dtype)

    @pl.when(j >= n_rest_tiles)
    def _():
        o = jnp.dot(xn_ref[...], w_ref[...], preferred_element_type=F32)
        for c in range(slab_ref.shape[0]):
            slab_ref[c] = o[:, LANES * c:LANES * (c + 1)]


def _norm_inproj(x2, g, w, w_col_block, n_rest, n_slab_cols, tm, tn):
    n, d = x2.shape
    n_rest_tiles = n_rest // tn
    n_slab_tiles = n_slab_cols // tn
    return pl.pallas_call(
        functools.partial(_norm_inproj_kernel, n_rest_tiles=n_rest_tiles),
        grid=(n // tm, n_rest_tiles + n_slab_tiles),
        in_specs=[pl.BlockSpec((tm, d), lambda i, j: (i, 0)),
                  pl.BlockSpec((1, d), lambda i, j: (0, 0)),
                  pl.BlockSpec((d, tn), lambda i, j: (0, w_col_block(j)))],
        out_specs=[pl.BlockSpec((tm, tn), lambda i, j: (i, jnp.minimum(j, n_rest_tiles - 1))),
                   pl.BlockSpec((tn // LANES, tm, LANES), lambda i, j: (jnp.maximum(j - n_rest_tiles, 0), i, 0))],
        out_shape=[jax.ShapeDtypeStruct((n, n_rest), BF16),
                   jax.ShapeDtypeStruct((n_slab_cols // LANES, n, LANES), F32)],
        scratch_shapes=[pltpu.VMEM((tm, d), BF16)],
        compiler_params=_cparams(2),
        name="norm_inproj",
    )(x2, g, w)


def _t5_bucket(dist, n_buckets):
    max_exact = n_buckets // 2
    large = max_exact + (jnp.log(jnp.maximum(dist, max_exact).astype(F32) / max_exact)
                         / math.log(MAX_DISTANCE / max_exact) * (n_buckets - max_exact)).astype(jnp.int32)
    return jnp.where(dist < max_exact, dist, jnp.minimum(large, n_buckets - 1))


def _band_bias_rows(rel_bias, band, dilation):
    delta = jnp.arange(band + 1)
    vals = rel_bias.astype(F32)[_t5_bucket(delta * dilation, rel_bias.shape[0])] * LOG2E
    return jnp.concatenate([vals[::-1].T, jnp.full((rel_bias.shape[1], band - 1), MASKED, F32)], axis=1)


def _attn_kernel(q_ref, kp_ref, kc_ref, vp_ref, vc_ref, brow_ref, o_ref, kk_ref, vv_ref, po_ref, pl_ref, bias_ref,
                 *, patterns, scale):
    n_t, sb_len, _ = q_ref.shape
    first_sb = pl.program_id(2) == 0

    @pl.when(jnp.logical_and(pl.program_id(1) == 0, first_sb))
    def _():
        for t in range(n_t):
            for p, (window, d) in enumerate(patterns):
                band = window // d
                has_prev = lax.broadcasted_iota(jnp.int32, (band, 2 * band), 1) >= band
                for hh in range(2):
                    row = brow_ref[t, 2 * p + hh:2 * p + hh + 1, :]
                    table = pltpu.roll(jnp.broadcast_to(row, (band, 2 * band)), 0, 1, stride=1, stride_axis=0)
                    bias_ref[t, p, 1, hh] = table
                    bias_ref[t, p, 0, hh] = jnp.where(has_prev, table, MASKED)

    kk_ref[:, 0:sb_len, :] = kp_ref[...]
    kk_ref[:, sb_len:2 * sb_len, :] = kc_ref[...]
    vv_ref[:, 0:sb_len, :] = vp_ref[...]
    vv_ref[:, sb_len:2 * sb_len, :] = vc_ref[...]
    heads_per_tile = LANES // A_HEAD_DIM
    assert heads_per_tile == 2

    for p, (window, d) in enumerate(patterns):
        band = window // d
        blocks_per_residue = sb_len // window
        first_head = lax.broadcasted_iota(jnp.int32, (band, LANES), 1) < A_HEAD_DIM
        first_head_keys = lax.broadcasted_iota(jnp.int32, (2 * band, LANES), 1) < A_HEAD_DIM

        def rows(start, n, d=d):
            return pl.ds(start, n, stride=d) if d > 1 else pl.ds(start, n)

        def block(i, carry, p=p, d=d, band=band, blocks_per_residue=blocks_per_residue, first_head=first_head,
                  first_head_keys=first_head_keys, rows=rows):
            r = i // blocks_per_residue
            jb = i % blocks_per_residue
            start = r + jb * (band * d)
            variant = jnp.where(jnp.logical_and(first_sb, jb == 0), 0, 1)
            for t in range(n_t):
                q2 = (q_ref[t, rows(start, band), :] * (scale * LOG2E)).astype(BF16)
                k2 = kk_ref[t, rows(sb_len + start - band * d, 2 * band), :].astype(BF16)
                v2 = vv_ref[t, rows(sb_len + start - band * d, 2 * band), :].astype(BF16)
                zeros = jnp.zeros_like(v2)
                es, ms = [], []
                for hh in range(heads_per_tile):
                    mine = first_head if hh == 0 else jnp.logical_not(first_head)
                    qm = jnp.where(mine, q2, jnp.zeros_like(q2))
                    s = lax.dot_general(qm, k2, (((1,), (1,)), ((), ())), preferred_element_type=F32)
                    s = s + bias_ref[t, p, variant, hh]
                    m = jnp.max(s, axis=1, keepdims=True)
                    es.append(jnp.exp2(s - m).astype(BF16))
                    ms.append(m)
                own0 = first_head_keys.astype(BF16)
                rhs = jnp.concatenate([jnp.concatenate([jnp.where(first_head_keys, v2, zeros), own0], axis=1),
                                       jnp.concatenate([jnp.where(first_head_keys, zeros, v2), 1 - own0], axis=1)],
                                      axis=0)
                both = jnp.dot(jnp.concatenate(es, axis=1), rhs, preferred_element_type=F32)
                num, den = both[:, :LANES], both[:, LANES:]
                po_ref[t, p, rows(start, band), :] = num / den
                pl_ref[t, p, rows(start, band), :] = jnp.where(first_head, ms[0], ms[1]) + jnp.log(den) * LOG2E
            return carry

        lax.fori_loop(0, sb_len // band, block, 0, unroll=ATTN_BLOCKS_PER_BODY // n_t)

    chunk = 256
    for t in range(n_t):
        for c in range(sb_len // chunk):
            sl = slice(chunk * c, chunk * (c + 1))
            lse = [pl_ref[t, p, sl, :] for p in range(len(patterns))]
            top = functools.reduce(jnp.maximum, lse)
            wgt = [jnp.exp2(l - top) for l in lse]
            num = sum(w * po_ref[t, p, sl, :] for p, w in enumerate(wgt))
            o_ref[0, sl, LANES * t:LANES * (t + 1)] = (num / sum(wgt)).astype(o_ref.dtype)


def _attention(qkv, rel_bias, aw):
    _, b_sz, s_len, _ = qkv.shape
    sb_len = max(w for w, _ in DILATED_PATTERNS)
    band = DILATED_PATTERNS[0][0] // DILATED_PATTERNS[0][1]
    assert all(w // d == band and sb_len % w == 0 for w, d in DILATED_PATTERNS)
    assert s_len % sb_len == 0 and band % LANES == 0
    n_pairs = aw // LANES
    n_pat = len(DILATED_PATTERNS)
    n_t = ATTN_PAIRS_PER_STEP
    assert n_pairs % n_t == 0 and ATTN_BLOCKS_PER_BODY % n_t == 0
    steps_j = n_pairs // n_t
    brow = jnp.stack([_band_bias_rows(rel_bias, band, d) for _, d in DILATED_PATTERNS])
    brow = brow.reshape(n_pat, n_pairs, 2, 2 * band).transpose(1, 0, 2, 3).reshape(n_pairs, n_pat * 2, 2 * band)

    def blk(which, prev):
        if prev:
            return pl.BlockSpec((n_t, None, sb_len, LANES),
                                lambda j, b, s: (which * steps_j + j, b, jnp.maximum(s - 1, 0), 0))
        return pl.BlockSpec((n_t, None, sb_len, LANES), lambda j, b, s: (which * steps_j + j, b, s, 0))

    return pl.pallas_call(
        functools.partial(_attn_kernel, patterns=DILATED_PATTERNS, scale=A_HEAD_DIM ** -0.5),
        grid=(steps_j, b_sz, s_len // sb_len),
        in_specs=[blk(0, False), blk(1, True), blk(1, False), blk(2, True), blk(2, False),
                  pl.BlockSpec((n_t, n_pat * 2, 2 * band), lambda j, b, s: (j, 0, 0))],
        out_specs=pl.BlockSpec((1, sb_len, n_t * LANES), lambda j, b, s: (b, s, j)),
        out_shape=jax.ShapeDtypeStruct((b_sz, s_len, aw), BF16),
        scratch_shapes=[pltpu.VMEM((n_t, 2 * sb_len, LANES), F32), pltpu.VMEM((n_t, 2 * sb_len, LANES), F32),
                        pltpu.VMEM((n_t, n_pat, sb_len, LANES), F32), pltpu.VMEM((n_t, n_pat, sb_len, LANES), F32),
                        pltpu.VMEM((n_t, n_pat, 2, 2, band, 2 * band), F32)],
        compiler_params=_cparams(3),
        name="dilated_attn",
    )(qkv, qkv, qkv, qkv, qkv, brow)


def _mlstm_front_kernel(x_ref, halo_ref, cw_ref, cb_ref, wblk_ref, wif_ref, bif_ref,
                        q_ref, k_ref, v_ref, xc_ref, gate_ref, xe_ref, wbd_ref, *, tiles_per_seq, k_scale, blk):
    tm, width = x_ref.shape
    taps = cw_ref.shape[0]
    pad = halo_ref.shape[0]

    @pl.when(pl.program_id(0) == 0)
    def _():
        r_id = lax.broadcasted_iota(jnp.int32, (MXU_DIM, MXU_DIM), 0)
        c_id = lax.broadcasted_iota(jnp.int32, (MXU_DIM, MXU_DIM), 1)
        same_block = (r_id // blk) == (c_id // blk)
        o_id = lax.broadcasted_iota(jnp.int32, (LANES, MXU_DIM), 0)
        spread = (o_id == lax.broadcasted_iota(jnp.int32, (LANES, MXU_DIM), 1) % blk).astype(BF16)
        for kind in range(wblk_ref.shape[0]):
            for j in range(width // MXU_DIM):
                rows = wblk_ref[kind, MXU_DIM * j:MXU_DIM * (j + 1), :].astype(BF16)
                tile = jnp.dot(rows, spread, preferred_element_type=F32)
                wbd_ref[kind, j] = jnp.where(same_block, tile, 0.0).astype(BF16)

    first = (pl.program_id(0) % tiles_per_seq) == 0
    halo = halo_ref[...].astype(F32)
    xe_ref[0:pad, :] = jnp.where(first, jnp.zeros_like(halo), halo)
    xe_ref[pad:pad + tm, :] = x_ref[...].astype(F32)
    n_tiles = width // MXU_DIM
    for r0 in range(0, tm, FRONT_ROW_CHUNK):
        rs = slice(r0, r0 + FRONT_ROW_CHUNK)
        xmb = x_ref[rs, :]
        y = cb_ref[...]
        for back in range(taps):
            y = y + (xe_ref[pad + r0 - back:pad + r0 - back + FRONT_ROW_CHUNK, :]
                     * cw_ref[taps - 1 - back:taps - back, :])
        xcb = _silu(y).astype(BF16)
        xc_ref[rs, :] = xcb
        qbs, kbs, vbs = [], [], []
        for j in range(n_tiles):
            sl = slice(MXU_DIM * j, MXU_DIM * (j + 1))
            qj = jnp.dot(xcb[:, sl], wbd_ref[0, j], preferred_element_type=F32)
            kj = jnp.dot(xcb[:, sl], wbd_ref[1, j], preferred_element_type=F32)
            vj = jnp.dot(xmb[:, sl], wbd_ref[2, j], preferred_element_type=F32)
            qbs.append(qj.astype(BF16))
            kbs.append(kj.astype(BF16))
            vbs.append(vj.astype(BF16))
            k_ref[rs, sl] = (kj * k_scale).astype(BF16)
        qb, kb, vb = (jnp.concatenate(t, axis=1) for t in (qbs, kbs, vbs))
        q_ref[rs, :] = qb
        v_ref[rs, :] = vb
        gate_ref[rs, :] = (bif_ref[...]
                           + jnp.dot(qb, wif_ref[0:width, :], preferred_element_type=F32)
                           + jnp.dot(kb, wif_ref[width:2 * width, :], preferred_element_type=F32)
                           + jnp.dot(vb, wif_ref[2 * width:3 * width, :], preferred_element_type=F32))


def _mlstm_front(rest2, conv_w, conv_b, w_blocks, wif_pad, bif_pad, s_len, width, tm):
    n = rest2.shape[0]
    halo = 16
    n_kinds, n_blocks, blk, _ = w_blocks.shape
    assert conv_w.shape[0] - 1 <= halo and n_blocks * blk == width and MXU_DIM % blk == 0
    wblk = jnp.pad(w_blocks.reshape(n_kinds, width, blk), ((0, 0), (0, 0), (0, LANES - blk)))
    const = lambda *shape: pl.BlockSpec(shape, lambda i: (0,) * len(shape))
    tok = lambda w: pl.BlockSpec((tm, w), lambda i: (i, 0))
    return pl.pallas_call(
        functools.partial(_mlstm_front_kernel, tiles_per_seq=s_len // tm,
                          k_scale=(width // M_HEADS) ** -0.5, blk=blk),
        grid=(n // tm,),
        in_specs=[tok(width),
                  pl.BlockSpec((halo, width), lambda i: (jnp.maximum(i * (tm // halo) - 1, 0), 0)),
                  const(*conv_w.shape), const(1, width),
                  const(*wblk.shape), const(*wif_pad.shape), const(1, LANES)],
        out_specs=[tok(width), tok(width), tok(width), tok(width), tok(LANES)],
        out_shape=[jax.ShapeDtypeStruct((n, width), BF16)] * 4 + [jax.ShapeDtypeStruct((n, LANES), F32)],
        scratch_shapes=[pltpu.VMEM((tm + halo, width), F32),
                        pltpu.VMEM((n_kinds, width // MXU_DIM, MXU_DIM, MXU_DIM), BF16)],
        compiler_params=_cparams(1),
        name="mlstm_front",
    )(rest2, rest2, conv_w, conv_b, wblk, wif_pad, bif_pad)


def _split3(x):
    hi = x.astype(BF16)
    r1 = x - hi.astype(F32)
    mid = r1.astype(BF16)
    lo = (r1 - mid.astype(F32)).astype(BF16)
    return hi, mid, lo


def _mlstm_core_kernel(q_ref, k_ref, v_ref, grow_ref, om_ref, zm_ref, xc_ref, hg_ref, skip_ref,
                       y_ref, c_ref, n_ref, m_ref):
    @pl.when(pl.program_id(2) == 0)
    def _():
        c_ref[...] = jnp.zeros_like(c_ref)
        n_ref[...] = jnp.zeros_like(n_ref)
        m_ref[...] = jnp.zeros_like(m_ref)

    lc = q_ref.shape[1]
    row_id = lax.broadcasted_iota(jnp.int32, (lc, lc), 0)
    col_id = lax.broadcasted_iota(jnp.int32, (lc, lc), 1)
    causal = row_id >= col_id
    upper = (row_id <= col_id).astype(BF16)
    gate_row = lax.broadcasted_iota(jnp.int32, (GATE_ROWS, lc), 0)
    for s in range(q_ref.shape[0]):
        q, k, v = q_ref[s], k_ref[s], v_ref[s]
        pre = grow_ref[s, 0]
        lf = jnp.minimum(pre, 0.0) - jnp.log1p(jnp.exp(-jnp.abs(pre)))
        csum = sum(jnp.dot(part, upper, preferred_element_type=F32) for part in _split3(lf))
        grow = jnp.where(gate_row == 0, pre, csum)
        gcol = grow.T
        i_row, b_row = grow[0:1, :], grow[1:2, :]
        i_col, b_col = gcol[:, 0:1], gcol[:, 1:2]
        m_prev = m_ref[s]
        g = b_row[:, lc - 1:lc]

        dmat = jnp.where(causal, b_col - b_row + i_row, MASKED)
        inter = b_col + m_prev
        m_t = jnp.maximum(inter, jnp.max(dmat, axis=1, keepdims=True))
        qk = lax.dot_general(q, k, (((1,), (1,)), ((), ())), preferred_element_type=F32) * jnp.exp(dmat - m_t)
        w_inter = jnp.exp(inter - m_t)
        c_old = c_ref[s]
        num = (w_inter * jnp.dot(q, c_old.astype(BF16), preferred_element_type=F32)
               + jnp.dot(qk.astype(BF16), v, preferred_element_type=F32))
        den = (w_inter * jnp.sum(q.astype(F32) * n_ref[s], axis=1, keepdims=True)
               + jnp.sum(qk, axis=1, keepdims=True))
        h = num / jnp.maximum(jnp.abs(den), jnp.exp(-m_t))

        m_new = jnp.maximum(g + m_prev, jnp.max(g - b_row + i_row, axis=1, keepdims=True))
        w_s = jnp.exp(g - b_col + i_col - m_new)
        decay = jnp.exp(g + m_prev - m_new)
        kw = (k.astype(F32) * w_s).astype(BF16)
        c_ref[s] = decay * c_old + lax.dot_general(kw, v, (((0,), (0,)), ((), ())), preferred_element_type=F32)
        n_ref[s] = decay * n_ref[s] + jnp.dot(jnp.ones((8, lc), BF16), kw, preferred_element_type=F32)[0:1]
        m_ref[s] = m_new

        hgated = _sigmoid(om_ref[s].astype(F32)) * h
        mu = jnp.mean(hgated, axis=1, keepdims=True)
        cen = hgated - mu
        var = jnp.mean(cen * cen, axis=1, keepdims=True)
        hn = cen * lax.rsqrt(var + EPS) * hg_ref[...]
        zm = zm_ref[s].astype(F32)
        y_ref[s] = ((hn + skip_ref[...] * xc_ref[s].astype(F32)) * _silu(zm)).astype(y_ref.dtype)


def _mlstm_core(q, k, v, grow, rest3, xc, head_norm_g, skip, lc, om_col0, zm_col0):
    b_sz, s_len, width = q.shape
    dh = width // M_HEADS
    grp = MLSTM_GROUP
    assert b_sz % grp == 0
    seq = lambda col0: pl.BlockSpec((grp, lc, dh), lambda b, h, c: (b, c, col0 + h))
    vec = pl.BlockSpec((1, dh), lambda b, h, c: (0, h))
    return pl.pallas_call(
        _mlstm_core_kernel,
        grid=(b_sz // grp, M_HEADS, s_len // lc),
        in_specs=[seq(0), seq(0), seq(0),
                  pl.BlockSpec((grp, 1, GATE_ROWS, lc), lambda b, h, c: (b, h, 0, c)),
                  seq(om_col0 // dh), seq(zm_col0 // dh), seq(0), vec, vec],
        out_specs=seq(0),
        out_shape=jax.ShapeDtypeStruct((b_sz, s_len, width), BF16),
        scratch_shapes=[pltpu.VMEM((grp, dh, dh), F32), pltpu.VMEM((grp, 1, dh), F32),
                        pltpu.VMEM((grp, 1, 1), F32)],
        compiler_params=_cparams(3),
        name="mlstm_core",
    )(q, k, v, grow, rest3, rest3, xc, head_norm_g, skip)


def _merge_out_kernel(ya_ref, za_ref, gates_ref, gb_ref, ym_ref, x_ref, wpa_ref, wpb_ref, wout_ref, gout_ref,
                      out_ref):
    d_model = x_ref.shape[1]
    za = za_ref[...].astype(F32)
    ya = jnp.dot((ya_ref[...].astype(F32) * _silu(za)).astype(BF16), wpa_ref[...],
                 preferred_element_type=F32)
    ym = jnp.dot(ym_ref[...], wpb_ref[...], preferred_element_type=F32)
    gate = _sigmoid(gates_ref[...].astype(F32) + gb_ref[...])
    merged = gate[:, :d_model] * ya + gate[:, d_model:] * ym
    hres = x_ref[...] + jnp.dot(merged.astype(BF16), wout_ref[...], preferred_element_type=F32)
    ms = jnp.mean(hres * hres, axis=-1, keepdims=True)
    out_ref[...] = hres * lax.rsqrt(ms + EPS) * gout_ref[...]


def _merge_out(ya2, proj2, gate_b, ym2, x2, wpa, wpb, wout, gout, za_col0, gates_col0, tm):
    n, d_model = x2.shape
    aw = ya2.shape[1]
    mw = ym2.shape[1]
    tok = lambda w, cb=0: pl.BlockSpec((tm, w), lambda i: (i, cb))
    const = lambda *shape: pl.BlockSpec(shape, lambda i: (0,) * len(shape))
    return pl.pallas_call(
        _merge_out_kernel,
        grid=(n // tm,),
        in_specs=[tok(aw), tok(aw, za_col0 // aw), tok(2 * d_model, gates_col0 // (2 * d_model)),
                  const(1, 2 * d_model), tok(mw), tok(d_model), const(aw, d_model), const(mw, d_model),
                  const(d_model, d_model), const(1, d_model)],
        out_specs=tok(d_model),
        out_shape=jax.ShapeDtypeStruct((n, d_model), F32),
        compiler_params=_cparams(1),
        name="merge_out",
    )(ya2, proj2, proj2, gate_b, ym2, x2, wpa, wpb, wout, gout)


def _layer(h2, b_sz, s_len, norm_in_g, w_in, gate_b, conv_w, conv_b, wq_m, wk_m, wv_m, w_if, b_if,
           head_norm_g, skip_m, w_pa, w_pb, w_out, rel_bias, gout):
    n, d_model = h2.shape
    aw = w_pa.shape[0]
    mw = w_pb.shape[0]
    assert aw == A_HEADS * A_HEAD_DIM and aw == d_model and mw == 2 * d_model
    wb = w_in.astype(BF16)
    xm_col0, zm_col0, om_col0, gates_col0, za_col0 = 0, mw, 2 * mw, 3 * mw, 3 * mw + 2 * d_model
    n_rest = za_col0 + aw
    tn = T_INPROJ_COLS
    assert aw == tn
    rest_first = 4 * aw // tn
    rest_tiles_before_za = (n_rest - aw) // tn
    n_rest_tiles = n_rest // tn
    za_block = 3 * aw // tn

    def w_col_block(j):
        return jnp.where(j < rest_tiles_before_za, j + rest_first,
                         jnp.where(j < n_rest_tiles, za_block, j - n_rest_tiles))

    rest2, qkv = _norm_inproj(h2, norm_in_g.reshape(1, d_model), wb, w_col_block, n_rest, 3 * aw,
                              tm=T_INPROJ_ROWS, tn=tn)
    to3 = lambda t: t.reshape(b_sz, s_len, t.shape[-1])

    ya = _attention(qkv.reshape(3 * aw // LANES, b_sz, s_len, LANES), rel_bias, aw)

    assert xm_col0 == 0
    wif_pad = jnp.pad(w_if, ((0, 0), (0, LANES - w_if.shape[1]))).astype(BF16)
    bif_pad = jnp.pad(b_if, (0, LANES - b_if.shape[0])).reshape(1, LANES)
    q, k, v, xc, gate_pre = _mlstm_front(
        rest2, conv_w, conv_b.reshape(1, mw), jnp.stack([wq_m, wk_m, wv_m]), wif_pad, bif_pad,
        s_len, mw, tm=T_FRONT_ROWS)
    grow = jnp.transpose(gate_pre[:, :2 * M_HEADS].reshape(b_sz, s_len, 2, M_HEADS), (0, 3, 2, 1))
    grow = jnp.pad(grow, ((0, 0), (0, 0), (0, GATE_ROWS - 2), (0, 0)))
    ym = _mlstm_core(to3(q), to3(k), to3(v), grow, to3(rest2), to3(xc),
                     head_norm_g.reshape(1, mw), skip_m.reshape(1, mw), MLSTM_CHUNK, om_col0, zm_col0)

    return _merge_out(ya.reshape(n, aw), rest2, gate_b.reshape(1, 2 * d_model), ym.reshape(n, mw), h2,
                      w_pa.astype(BF16), w_pb.astype(BF16), w_out.astype(BF16), gout.reshape(1, d_model),
                      za_col0, gates_col0, tm=T_MERGE_ROWS)


def kernel(x, norm_in_g, w_in, gate_b, conv_w, conv_b, wq_m, wk_m, wv_m, w_if, b_if, head_norm_g, skip_m,
           w_pa, w_pb, w_out, rel_bias, norm_out_g):
    b_sz, s_len, d_model = x.shape
    depth = w_in.shape[0]
    assert depth == 1
    out = _layer(x.reshape(b_sz * s_len, d_model), b_sz, s_len, norm_in_g[0], w_in[0], gate_b[0], conv_w[0],
                 conv_b[0], wq_m[0], wk_m[0], wv_m[0], w_if[0], b_if[0], head_norm_g[0], skip_m[0], w_pa[0],
                 w_pb[0], w_out[0], rel_bias, norm_out_g)
    return out.reshape(b_sz, s_len, d_model)
```

```python
import functools
import math

import jax
import jax.numpy as jnp
from jax import lax
from jax.experimental import pallas as pl
from jax.experimental.pallas import tpu as pltpu

F32 = jnp.float32
BF16 = jnp.bfloat16

A_HEADS = 16
A_HEAD_DIM = 64
DILATED_PATTERNS = ((128, 1), (512, 4), (2048, 16))
MAX_DISTANCE = 2048
M_HEADS = 4
EPS = 1e-6
MASKED = -1e30
LOG2E = math.log2(math.e)


def _sigmoid(x):
    return 0.5 * jnp.tanh(0.5 * x) + 0.5


def _silu(x):
    h = 0.5 * x
    return h * jnp.tanh(h) + h


LANES = 128
MXU_DIM = 256
VMEM_LIMIT_BYTES = 56 * 1024 * 1024

MLSTM_CHUNK = 256
MLSTM_GROUP = 8
GATE_ROWS = 8
ATTN_PAIRS_PER_STEP = 2
ATTN_BLOCKS_PER_BODY = 16

T_INPROJ_ROWS = 2048
T_INPROJ_COLS = 1024
T_FRONT_ROWS = 512
FRONT_ROW_CHUNK = 256
T_MERGE_ROWS = 512


def _cparams(n_axes):
    return pltpu.CompilerParams(dimension_semantics=("arbitrary",) * n_axes,
                                vmem_limit_bytes=VMEM_LIMIT_BYTES)


def _norm_inproj_kernel(x_ref, g_ref, w_ref, rest_ref, slab_ref, xn_ref, *, n_rest_tiles):
    j = pl.program_id(1)

    @pl.when(j == 0)
    def _():
        xf = x_ref[...]
        ms = jnp.mean(xf * xf, axis=-1, keepdims=True)
        xn_ref[...] = (xf * lax.rsqrt(ms + EPS) * g_ref[...]).astype(BF16)

    @pl.when(j < n_rest_tiles)
    def _():
        rest_ref[...] = jnp.dot(xn_ref[...], w_ref[...], preferred_element_type=F32).astype(rest_ref.dtype)

    @pl.when(j >= n_rest_tiles)
    def _():
        o = jnp.dot(xn_ref[...], w_ref[...], preferred_element_type=F32)
        for c in range(slab_ref.shape[0]):
            slab_ref[c] = o[:, LANES * c:LANES * (c + 1)]


def _norm_inproj(x2, g, w, w_col_block, n_rest, n_slab_cols, tm, tn):
    n, d = x2.shape
    n_rest_tiles = n_rest // tn
    n_slab_tiles = n_slab_cols // tn
    return pl.pallas_call(
        functools.partial(_norm_inproj_kernel, n_rest_tiles=n_rest_tiles),
        grid=(n // tm, n_rest_tiles + n_slab_tiles),
        in_specs=[pl.BlockSpec((tm, d), lambda i, j: (i, 0)),
                  pl.BlockSpec((1, d), lambda i, j: (0, 0)),
                  pl.BlockSpec((d, tn), lambda i, j: (0, w_col_block(j)))],
        out_specs=[pl.BlockSpec((tm, tn), lambda i, j: (i, jnp.minimum(j, n_rest_tiles - 1))),
                   pl.BlockSpec((tn // LANES, tm, LANES), lambda i, j: (jnp.maximum(j - n_rest_tiles, 0), i, 0))],
        out_shape=[jax.ShapeDtypeStruct((n, n_rest), BF16),
                   jax.ShapeDtypeStruct((n_slab_cols // LANES, n, LANES), F32)],
        scratch_shapes=[pltpu.VMEM((tm, d), BF16)],
        compiler_params=_cparams(2),
        name="norm_inproj",
    )(x2, g, w)


def _t5_bucket(dist, n_buckets):
    max_exact = n_buckets // 2
    large = max_exact + (jnp.log(jnp.maximum(dist, max_exact).astype(F32) / max_exact)
                         / math.log(MAX_DISTANCE / max_exact) * (n_buckets - max_exact)).astype(jnp.int32)
    return jnp.where(dist < max_exact, dist, jnp.minimum(large, n_buckets - 1))


def _band_bias_rows(rel_bias, band, dilation):
    delta = jnp.arange(band + 1)
    vals = rel_bias.astype(F32)[_t5_bucket(delta * dilation, rel_bias.shape[0])] * LOG2E
    return jnp.concatenate([vals[::-1].T, jnp.full((rel_bias.shape[1], band - 1), MASKED, F32)], axis=1)


def _attn_kernel(q_ref, kp_ref, kc_ref, vp_ref, vc_ref, brow_ref, o_ref, kk_ref, vv_ref, po_ref, pl_ref, bias_ref,
                 *, patterns, scale):
    n_t, sb_len, _ = q_ref.shape
    first_sb = pl.program_id(2) == 0

    @pl.when(jnp.logical_and(pl.program_id(1) == 0, first_sb))
    def _():
        for t in range(n_t):
            for p, (window, d) in enumerate(patterns):
                band = window // d
                has_prev = lax.broadcasted_iota(jnp.int32, (band, 2 * band), 1) >= band
                for hh in range(2):
                    row = brow_ref[t, 2 * p + hh:2 * p + hh + 1, :]
                    table = pltpu.roll(jnp.broadcast_to(row, (band, 2 * band)), 0, 1, stride=1, stride_axis=0)
                    bias_ref[t, p, 1, hh] = table
                    bias_ref[t, p, 0, hh] = jnp.where(has_prev, table, MASKED)

    kk_ref[:, 0:sb_len, :] = kp_ref[...]
    kk_ref[:, sb_len:2 * sb_len, :] = kc_ref[...]
    vv_ref[:, 0:sb_len, :] = vp_ref[...]
    vv_ref[:, sb_len:2 * sb_len, :] = vc_ref[...]
    heads_per_tile = LANES // A_HEAD_DIM
    assert heads_per_tile == 2

    for p, (window, d) in enumerate(patterns):
        band = window // d
        blocks_per_residue = sb_len // window
        first_head = lax.broadcasted_iota(jnp.int32, (band, LANES), 1) < A_HEAD_DIM
        first_head_keys = lax.broadcasted_iota(jnp.int32, (2 * band, LANES), 1) < A_HEAD_DIM

        def rows(start, n, d=d):
            return pl.ds(start, n, stride=d) if d > 1 else pl.ds(start, n)

        def block(i, carry, p=p, d=d, band=band, blocks_per_residue=blocks_per_residue, first_head=first_head,
                  first_head_keys=first_head_keys, rows=rows):
            r = i // blocks_per_residue
            jb = i % blocks_per_residue
            start = r + jb * (band * d)
            variant = jnp.where(jnp.logical_and(first_sb, jb == 0), 0, 1)
            for t in range(n_t):
                q2 = (q_ref[t, rows(start, band), :] * (scale * LOG2E)).astype(BF16)
                k2 = kk_ref[t, rows(sb_len + start - band * d, 2 * band), :].astype(BF16)
                v2 = vv_ref[t, rows(sb_len + start - band * d, 2 * band), :].astype(BF16)
                zeros = jnp.zeros_like(v2)
                es, ms = [], []
                for hh in range(heads_per_tile):
                    mine = first_head if hh == 0 else jnp.logical_not(first_head)
                    qm = jnp.where(mine, q2, jnp.zeros_like(q2))
                    s = lax.dot_general(qm, k2, (((1,), (1,)), ((), ())), preferred_element_type=F32)
                    s = s + bias_ref[t, p, variant, hh]
                    m = jnp.max(s, axis=1, keepdims=True)
                    es.append(jnp.exp2(s - m).astype(BF16))
                    ms.append(m)
                own0 = first_head_keys.astype(BF16)
                rhs = jnp.concatenate([jnp.concatenate([jnp.where(first_head_keys, v2, zeros), own0], axis=1),
                                       jnp.concatenate([jnp.where(first_head_keys, zeros, v2), 1 - own0], axis=1)],
                                      axis=0)
                both = jnp.dot(jnp.concatenate(es, axis=1), rhs, preferred_element_type=F32)
                num, den = both[:, :LANES], both[:, LANES:]
                po_ref[t, p, rows(start, band), :] = num / den
                pl_ref[t, p, rows(start, band), :] = jnp.where(first_head, ms[0], ms[1]) + jnp.log(den) * LOG2E
            return carry

        lax.fori_loop(0, sb_len // band, block, 0, unroll=ATTN_BLOCKS_PER_BODY // n_t)

    chunk = 256
    for t in range(n_t):
        for c in range(sb_len // chunk):
            sl = slice(chunk * c, chunk * (c + 1))
            lse = [pl_ref[t, p, sl, :] for p in range(len(patterns))]
            top = functools.reduce(jnp.maximum, lse)
            wgt = [jnp.exp2(l - top) for l in lse]
            num = sum(w * po_ref[t, p, sl, :] for p, w in enumerate(wgt))
            o_ref[0, sl, LANES * t:LANES * (t + 1)] = (num / sum(wgt)).astype(o_ref.dtype)


def _attention(qkv, rel_bias, aw):
    _, b_sz, s_len, _ = qkv.shape
    sb_len = max(w for w, _ in DILATED_PATTERNS)
    band = DILATED_PATTERNS[0][0] // DILATED_PATTERNS[0][1]
    assert all(w // d == band and sb_len % w == 0 for w, d in DILATED_PATTERNS)
    assert s_len % sb_len == 0 and band % LANES == 0
    n_pairs = aw // LANES
    n_pat = len(DILATED_PATTERNS)
    n_t = ATTN_PAIRS_PER_STEP
    assert n_pairs % n_t == 0 and ATTN_BLOCKS_PER_BODY % n_t == 0
    steps_j = n_pairs // n_t
    brow = jnp.stack([_band_bias_rows(rel_bias, band, d) for _, d in DILATED_PATTERNS])
    brow = brow.reshape(n_pat, n_pairs, 2, 2 * band).transpose(1, 0, 2, 3).reshape(n_pairs, n_pat * 2, 2 * band)

    def blk(which, prev):
        if prev:
            return pl.BlockSpec((n_t, None, sb_len, LANES),
                                lambda j, b, s: (which * steps_j + j, b, jnp.maximum(s - 1, 0), 0))
        return pl.BlockSpec((n_t, None, sb_len, LANES), lambda j, b, s: (which * steps_j + j, b, s, 0))

    return pl.pallas_call(
        functools.partial(_attn_kernel, patterns=DILATED_PATTERNS, scale=A_HEAD_DIM ** -0.5),
        grid=(steps_j, b_sz, s_len // sb_len),
        in_specs=[blk(0, False), blk(1, True), blk(1, False), blk(2, True), blk(2, False),
                  pl.BlockSpec((n_t, n_pat * 2, 2 * band), lambda j, b, s: (j, 0, 0))],
        out_specs=pl.BlockSpec((1, sb_len, n_t * LANES), lambda j, b, s: (b, s, j)),
        out_shape=jax.ShapeDtypeStruct((b_sz, s_len, aw), BF16),
        scratch_shapes=[pltpu.VMEM((n_t, 2 * sb_len, LANES), F32), pltpu.VMEM((n_t, 2 * sb_len, LANES), F32),
                        pltpu.VMEM((n_t, n_pat, sb_len, LANES), F32), pltpu.VMEM((n_t, n_pat, sb_len, LANES), F32),
                        pltpu.VMEM((n_t, n_pat, 2, 2, band, 2 * band), F32)],
        compiler_params=_cparams(3),
        name="dilated_attn",
    )(qkv, qkv, qkv, qkv, qkv, brow)


def _mlstm_front_kernel(x_ref, halo_ref, cw_ref, cb_ref, wblk_ref, wif_ref, bif_ref,
                        q_ref, k_ref, v_ref, xc_ref, gate_ref, xe_ref, wbd_ref, *, tiles_per_seq, k_scale, blk):
    tm, width = x_ref.shape
    taps = cw_ref.shape[0]
    pad = halo_ref.shape[0]

    @pl.when(pl.program_id(0) == 0)
    def _():
        r_id = lax.broadcasted_iota(jnp.int32, (MXU_DIM, MXU_DIM), 0)
        c_id = lax.broadcasted_iota(jnp.int32, (MXU_DIM, MXU_DIM), 1)
        same_block = (r_id // blk) == (c_id // blk)
        o_id = lax.broadcasted_iota(jnp.int32, (LANES, MXU_DIM), 0)
        spread = (o_id == lax.broadcasted_iota(jnp.int32, (LANES, MXU_DIM), 1) % blk).astype(BF16)
        for kind in range(wblk_ref.shape[0]):
            for j in range(width // MXU_DIM):
                rows = wblk_ref[kind, MXU_DIM * j:MXU_DIM * (j + 1), :].astype(BF16)
                tile = jnp.dot(rows, spread, preferred_element_type=F32)
                wbd_ref[kind, j] = jnp.where(same_block, tile, 0.0).astype(BF16)

    first = (pl.program_id(0) % tiles_per_seq) == 0
    halo = halo_ref[...].astype(F32)
    xe_ref[0:pad, :] = jnp.where(first, jnp.zeros_like(halo), halo)
    xe_ref[pad:pad + tm, :] = x_ref[...].astype(F32)
    n_tiles = width // MXU_DIM
    for r0 in range(0, tm, FRONT_ROW_CHUNK):
        rs = slice(r0, r0 + FRONT_ROW_CHUNK)
        xmb = x_ref[rs, :]
        y = cb_ref[...]
        for back in range(taps):
            y = y + (xe_ref[pad + r0 - back:pad + r0 - back + FRONT_ROW_CHUNK, :]
                     * cw_ref[taps - 1 - back:taps - back, :])
        xcb = _silu(y).astype(BF16)
        xc_ref[rs, :] = xcb
        qbs, kbs, vbs = [], [], []
        for j in range(n_tiles):
            sl = slice(MXU_DIM * j, MXU_DIM * (j + 1))
            qj = jnp.dot(xcb[:, sl], wbd_ref[0, j], preferred_element_type=F32)
            kj = jnp.dot(xcb[:, sl], wbd_ref[1, j], preferred_element_type=F32)
            vj = jnp.dot(xmb[:, sl], wbd_ref[2, j], preferred_element_type=F32)
            qbs.append(qj.astype(BF16))
            kbs.append(kj.astype(BF16))
            vbs.append(vj.astype(BF16))
            k_ref[rs, sl] = (kj * k_scale).astype(BF16)
        qb, kb, vb = (jnp.concatenate(t, axis=1) for t in (qbs, kbs, vbs))
        q_ref[rs, :] = qb
        v_ref[rs, :] = vb
        gate_ref[rs, :] = (bif_ref[...]
                           + jnp.dot(qb, wif_ref[0:width, :], preferred_element_type=F32)
                           + jnp.dot(kb, wif_ref[width:2 * width, :], preferred_element_type=F32)
                           + jnp.dot(vb, wif_ref[2 * width:3 * width, :], preferred_element_type=F32))


def _mlstm_front(rest2, conv_w, conv_b, w_blocks, wif_pad, bif_pad, s_len, width, tm):
    n = rest2.shape[0]
    halo = 16
    n_kinds, n_blocks, blk, _ = w_blocks.shape
    assert conv_w.shape[0] - 1 <= halo and n_blocks * blk == width and MXU_DIM % blk == 0
    wblk = jnp.pad(w_blocks.reshape(n_kinds, width, blk), ((0, 0), (0, 0), (0, LANES - blk)))
    const = lambda *shape: pl.BlockSpec(shape, lambda i: (0,) * len(shape))
    tok = lambda w: pl.BlockSpec((tm, w), lambda i: (i, 0))
    return pl.pallas_call(
        functools.partial(_mlstm_front_kernel, tiles_per_seq=s_len // tm,
                          k_scale=(width // M_HEADS) ** -0.5, blk=blk),
        grid=(n // tm,),
        in_specs=[tok(width),
                  pl.BlockSpec((halo, width), lambda i: (jnp.maximum(i * (tm // halo) - 1, 0), 0)),
                  const(*conv_w.shape), const(1, width),
                  const(*wblk.shape), const(*wif_pad.shape), const(1, LANES)],
        out_specs=[tok(width), tok(width), tok(width), tok(width), tok(LANES)],
        out_shape=[jax.ShapeDtypeStruct((n, width), BF16)] * 4 + [jax.ShapeDtypeStruct((n, LANES), F32)],
        scratch_shapes=[pltpu.VMEM((tm + halo, width), F32),
                        pltpu.VMEM((n_kinds, width // MXU_DIM, MXU_DIM, MXU_DIM), BF16)],
        compiler_params=_cparams(1),
        name="mlstm_front",
    )(rest2, rest2, conv_w, conv_b, wblk, wif_pad, bif_pad)


def _split3(x):
    hi = x.astype(BF16)
    r1 = x - hi.astype(F32)
    mid = r1.astype(BF16)
    lo = (r1 - mid.astype(F32)).astype(BF16)
    return hi, mid, lo


def _mlstm_core_kernel(q_ref, k_ref, v_ref, grow_ref, om_ref, zm_ref, xc_ref, hg_ref, skip_ref,
                       y_ref, c_ref, n_ref, m_ref):
    @pl.when(pl.program_id(2) == 0)
    def _():
        c_ref[...] = jnp.zeros_like(c_ref)
        n_ref[...] = jnp.zeros_like(n_ref)
        m_ref[...] = jnp.zeros_like(m_ref)

    lc = q_ref.shape[1]
    row_id = lax.broadcasted_iota(jnp.int32, (lc, lc), 0)
    col_id = lax.broadcasted_iota(jnp.int32, (lc, lc), 1)
    causal = row_id >= col_id
    upper = (row_id <= col_id).astype(BF16)
    gate_row = lax.broadcasted_iota(jnp.int32, (GATE_ROWS, lc), 0)
    for s in range(q_ref.shape[0]):
        q, k, v = q_ref[s], k_ref[s], v_ref[s]
        pre = grow_ref[s, 0]
        lf = jnp.minimum(pre, 0.0) - jnp.log1p(jnp.exp(-jnp.abs(pre)))
        csum = sum(jnp.dot(part, upper, preferred_element_type=F32) for part in _split3(lf))
        grow = jnp.where(gate_row == 0, pre, csum)
        gcol = grow.T
        i_row, b_row = grow[0:1, :], grow[1:2, :]
        i_col, b_col = gcol[:, 0:1], gcol[:, 1:2]
        m_prev = m_ref[s]
        g = b_row[:, lc - 1:lc]

        dmat = jnp.where(causal, b_col - b_row + i_row, MASKED)
        inter = b_col + m_prev
        m_t = jnp.maximum(inter, jnp.max(dmat, axis=1, keepdims=True))
        qk = lax.dot_general(q, k, (((1,), (1,)), ((), ())), preferred_element_type=F32) * jnp.exp(dmat - m_t)
        w_inter = jnp.exp(inter - m_t)
        c_old = c_ref[s]
        num = (w_inter * jnp.dot(q, c_old.astype(BF16), preferred_element_type=F32)
               + jnp.dot(qk.astype(BF16), v, preferred_element_type=F32))
        den = (w_inter * jnp.sum(q.astype(F32) * n_ref[s], axis=1, keepdims=True)
               + jnp.sum(qk, axis=1, keepdims=True))
        h = num / jnp.maximum(jnp.abs(den), jnp.exp(-m_t))

        m_new = jnp.maximum(g + m_prev, jnp.max(g - b_row + i_row, axis=1, keepdims=True))
        w_s = jnp.exp(g - b_col + i_col - m_new)
        decay = jnp.exp(g + m_prev - m_new)
        kw = (k.astype(F32) * w_s).astype(BF16)
        c_ref[s] = decay * c_old + lax.dot_general(kw, v, (((0,), (0,)), ((), ())), preferred_element_type=F32)
        n_ref[s] = decay * n_ref[s] + jnp.dot(jnp.ones((8, lc), BF16), kw, preferred_element_type=F32)[0:1]
        m_ref[s] = m_new

        hgated = _sigmoid(om_ref[s].astype(F32)) * h
        mu = jnp.mean(hgated, axis=1, keepdims=True)
        cen = hgated - mu
        var = jnp.mean(cen * cen, axis=1, keepdims=True)
        hn = cen * lax.rsqrt(var + EPS) * hg_ref[...]
        zm = zm_ref[s].astype(F32)
        y_ref[s] = ((hn + skip_ref[...] * xc_ref[s].astype(F32)) * _silu(zm)).astype(y_ref.dtype)


def _mlstm_core(q, k, v, grow, rest3, xc, head_norm_g, skip, lc, om_col0, zm_col0):
    b_sz, s_len, width = q.shape
    dh = width // M_HEADS
    grp = MLSTM_GROUP
    assert b_sz % grp == 0
    seq = lambda col0: pl.BlockSpec((grp, lc, dh), lambda b, h, c: (b, c, col0 + h))
    vec = pl.BlockSpec((1, dh), lambda b, h, c: (0, h))
    return pl.pallas_call(
        _mlstm_core_kernel,
        grid=(b_sz // grp, M_HEADS, s_len // lc),
        in_specs=[seq(0), seq(0), seq(0),
                  pl.BlockSpec((grp, 1, GATE_ROWS, lc), lambda b, h, c: (b, h, 0, c)),
                  seq(om_col0 // dh), seq(zm_col0 // dh), seq(0), vec, vec],
        out_specs=seq(0),
        out_shape=jax.ShapeDtypeStruct((b_sz, s_len, width), BF16),
        scratch_shapes=[pltpu.VMEM((grp, dh, dh), F32), pltpu.VMEM((grp, 1, dh), F32),
                        pltpu.VMEM((grp, 1, 1), F32)],
        compiler_params=_cparams(3),
        name="mlstm_core",
    )(q, k, v, grow, rest3, rest3, xc, head_norm_g, skip)


def _merge_out_kernel(ya_ref, za_ref, gates_ref, gb_ref, ym_ref, x_ref, wpa_ref, wpb_ref, wout_ref, gout_ref,
                      out_ref):
    d_model = x_ref.shape[1]
    za = za_ref[...].astype(F32)
    ya = jnp.dot((ya_ref[...].astype(F32) * _silu(za)).astype(BF16), wpa_ref[...],
                 preferred_element_type=F32)
    ym = jnp.dot(ym_ref[...], wpb_ref[...], preferred_element_type=F32)
    gate = _sigmoid(gates_ref[...].astype(F32) + gb_ref[...])
    merged = gate[:, :d_model] * ya + gate[:, d_model:] * ym
    hres = x_ref[...] + jnp.dot(merged.astype(BF16), wout_ref[...], preferred_element_type=F32)
    ms = jnp.mean(hres * hres, axis=-1, keepdims=True)
    out_ref[...] = hres * lax.rsqrt(ms + EPS) * gout_ref[...]


def _merge_out(ya2, proj2, gate_b, ym2, x2, wpa, wpb, wout, gout, za_col0, gates_col0, tm):
    n, d_model = x2.shape
    aw = ya2.shape[1]
    mw = ym2.shape[1]
    tok = lambda w, cb=0: pl.BlockSpec((tm, w), lambda i: (i, cb))
    const = lambda *shape: pl.BlockSpec(shape, lambda i: (0,) * len(shape))
    return pl.pallas_call(
        _merge_out_kernel,
        grid=(n // tm,),
        in_specs=[tok(aw), tok(aw, za_col0 // aw), tok(2 * d_model, gates_col0 // (2 * d_model)),
                  const(1, 2 * d_model), tok(mw), tok(d_model), const(aw, d_model), const(mw, d_model),
                  const(d_model, d_model), const(1, d_model)],
        out_specs=tok(d_model),
        out_shape=jax.ShapeDtypeStruct((n, d_model), F32),
        compiler_params=_cparams(1),
        name="merge_out",
    )(ya2, proj2, proj2, gate_b, ym2, x2, wpa, wpb, wout, gout)


def _layer(h2, b_sz, s_len, norm_in_g, w_in, gate_b, conv_w, conv_b, wq_m, wk_m, wv_m, w_if, b_if,
           head_norm_g, skip_m, w_pa, w_pb, w_out, rel_bias, gout):
    n, d_model = h2.shape
    aw = w_pa.shape[0]
    mw = w_pb.shape[0]
    assert aw == A_HEADS * A_HEAD_DIM and aw == d_model and mw == 2 * d_model
    wb = w_in.astype(BF16)
    xm_col0, zm_col0, om_col0, gates_col0, za_col0 = 0, mw, 2 * mw, 3 * mw, 3 * mw + 2 * d_model
    n_rest = za_col0 + aw
    tn = T_INPROJ_COLS
    assert aw == tn
    rest_first = 4 * aw // tn
    rest_tiles_before_za = (n_rest - aw) // tn
    n_rest_tiles = n_rest // tn
    za_block = 3 * aw // tn

    def w_col_block(j):
        return jnp.where(j < rest_tiles_before_za, j + rest_first,
                         jnp.where(j < n_rest_tiles, za_block, j - n_rest_tiles))

    rest2, qkv = _norm_inproj(h2, norm_in_g.reshape(1, d_model), wb, w_col_block, n_rest, 3 * aw,
                              tm=T_INPROJ_ROWS, tn=tn)
    to3 = lambda t: t.reshape(b_sz, s_len, t.shape[-1])

    ya = _attention(qkv.reshape(3 * aw // LANES, b_sz, s_len, LANES), rel_bias, aw)

    assert xm_col0 == 0
    wif_pad = jnp.pad(w_if, ((0, 0), (0, LANES - w_if.shape[1]))).astype(BF16)
    bif_pad = jnp.pad(b_if, (0, LANES - b_if.shape[0])).reshape(1, LANES)
    q, k, v, xc, gate_pre = _mlstm_front(
        rest2, conv_w, conv_b.reshape(1, mw), jnp.stack([wq_m, wk_m, wv_m]), wif_pad, bif_pad,
        s_len, mw, tm=T_FRONT_ROWS)
    grow = jnp.transpose(gate_pre[:, :2 * M_HEADS].reshape(b_sz, s_len, 2, M_HEADS), (0, 3, 2, 1))
    grow = jnp.pad(grow, ((0, 0), (0, 0), (0, GATE_ROWS - 2), (0, 0)))
    ym = _mlstm_core(to3(q), to3(k), to3(v), grow, to3(rest2), to3(xc),
                     head_norm_g.reshape(1, mw), skip_m.reshape(1, mw), MLSTM_CHUNK, om_col0, zm_col0)

    return _merge_out(ya.reshape(n, aw), rest2, gate_b.reshape(1, 2 * d_model), ym.reshape(n, mw), h2,
                      w_pa.astype(BF16), w_pb.astype(BF16), w_out.astype(BF16), gout.reshape(1, d_model),
                      za_col0, gates_col0, tm=T_MERGE_ROWS)


def kernel(x, norm_in_g, w_in, gate_b, conv_w, conv_b, wq_m, wk_m, wv_m, w_if, b_if, head_norm_g, skip_m,
           w_pa, w_pb, w_out, rel_bias, norm_out_g):
    b_sz, s_len, d_model = x.shape
    depth = w_in.shape[0]
    assert depth == 1
    out = _layer(x.reshape(b_sz * s_len, d_model), b_sz, s_len, norm_in_g[0], w_in[0], gate_b[0], conv_w[0],
                 conv_b[0], wq_m[0], wk_m[0], wv_m[0], w_if[0], b_if[0], head_norm_g[0], skip_m[0], w_pa[0],
                 w_pb[0], w_out[0], rel_bias, norm_out_g)
    return out.reshape(b_sz, s_len, d_model)
```

```python
import functools
import math

import jax
import jax.numpy as jnp
from jax import lax
from jax.experimental import pallas as pl
from jax.experimental.pallas import tpu as pltpu

F32 = jnp.float32
BF16 = jnp.bfloat16

A_HEADS = 16
A_HEAD_DIM = 64
DILATED_PATTERNS = ((128, 1), (512, 4), (2048, 16))
MAX_DISTANCE = 2048
M_HEADS = 4
EPS = 1e-6
MASKED = -1e30
LOG2E = math.log2(math.e)


def _sigmoid(x):
    return 0.5 * jnp.tanh(0.5 * x) + 0.5


def _silu(x):
    h = 0.5 * x
    return h * jnp.tanh(h) + h


LANES = 128
MXU_DIM = 256
VMEM_LIMIT_BYTES = 56 * 1024 * 1024

MLSTM_CHUNK = 256
MLSTM_GROUP = 8
GATE_ROWS = 8
ATTN_PAIRS_PER_STEP = 2
ATTN_RESIDUE_PARTS = 4
ATTN_BLOCKS_PER_BODY = 16

T_INPROJ_ROWS = 2048
T_INPROJ_COLS = 1024
T_FRONT_ROWS = 512
FRONT_ROW_CHUNK = 256
T_MERGE_ROWS = 512


def _cparams(n_axes):
    return pltpu.CompilerParams(dimension_semantics=("arbitrary",) * n_axes,
                                vmem_limit_bytes=VMEM_LIMIT_BYTES)


def _norm_inproj_kernel(x_ref, g_ref, w_ref, rest_ref, slab_ref, xn_ref, *, n_rest_tiles):
    j = pl.program_id(1)

    @pl.when(j == 0)
    def _():
        xf = x_ref[...]
        ms = jnp.mean(xf * xf, axis=-1, keepdims=True)
        xn_ref[...] = (xf * lax.rsqrt(ms + EPS) * g_ref[...]).astype(BF16)

    @pl.when(j < n_rest_tiles)
    def _():
        rest_ref[...] = jnp.dot(xn_ref[...], w_ref[...], preferred_element_type=F32).astype(rest_ref.dtype)

    @pl.when(j >= n_rest_tiles)
    def _():
        o = jnp.dot(xn_ref[...], w_ref[...], preferred_element_type=F32)
        for c in range(slab_ref.shape[0]):
            slab_ref[c] = o[:, LANES * c:LANES * (c + 1)]


def _norm_inproj(x2, g, w, w_col_block, n_rest, n_slab_cols, tm, tn):
    n, d = x2.shape
    n_rest_tiles = n_rest // tn
    n_slab_tiles = n_slab_cols // tn
    return pl.pallas_call(
        functools.partial(_norm_inproj_kernel, n_rest_tiles=n_rest_tiles),
        grid=(n // tm, n_rest_tiles + n_slab_tiles),
        in_specs=[pl.BlockSpec((tm, d), lambda i, j: (i, 0)),
                  pl.BlockSpec((1, d), lambda i, j: (0, 0)),
                  pl.BlockSpec((d, tn), lambda i, j: (0, w_col_block(j)))],
        out_specs=[pl.BlockSpec((tm, tn), lambda i, j: (i, jnp.minimum(j, n_rest_tiles - 1))),
                   pl.BlockSpec((tn // LANES, tm, LANES), lambda i, j: (jnp.maximum(j - n_rest_tiles, 0), i, 0))],
        out_shape=[jax.ShapeDtypeStruct((n, n_rest), BF16),
                   jax.ShapeDtypeStruct((n_slab_cols // LANES, n, LANES), F32)],
        scratch_shapes=[pltpu.VMEM((tm, d), BF16)],
        compiler_params=_cparams(2),
        name="norm_inproj",
    )(x2, g, w)


def _t5_bucket(dist, n_buckets):
    max_exact = n_buckets // 2
    large = max_exact + (jnp.log(jnp.maximum(dist, max_exact).astype(F32) / max_exact)
                         / math.log(MAX_DISTANCE / max_exact) * (n_buckets - max_exact)).astype(jnp.int32)
    return jnp.where(dist < max_exact, dist, jnp.minimum(large, n_buckets - 1))


def _band_bias_rows(rel_bias, band, dilation):
    delta = jnp.arange(band + 1)
    vals = rel_bias.astype(F32)[_t5_bucket(delta * dilation, rel_bias.shape[0])] * LOG2E
    return jnp.concatenate([vals[::-1].T, jnp.full((rel_bias.shape[1], band - 1), MASKED, F32)], axis=1)


def _attn_kernel(q_ref, kp_ref, kc_ref, vp_ref, vc_ref, brow_ref, o_ref, kk_ref, vv_ref, po_ref, pl_ref, bias_ref,
                 qq_ref, kq_ref, vq_ref, *, patterns, scale):
    n_t, sb_len, _ = q_ref.shape
    first_sb = pl.program_id(2) == 0

    @pl.when(jnp.logical_and(pl.program_id(1) == 0, first_sb))
    def _():
        for t in range(n_t):
            for p, (window, d) in enumerate(patterns):
                band = window // d
                has_prev = lax.broadcasted_iota(jnp.int32, (band, 2 * band), 1) >= band
                for hh in range(2):
                    row = brow_ref[t, 2 * p + hh:2 * p + hh + 1, :]
                    table = pltpu.roll(jnp.broadcast_to(row, (band, 2 * band)), 0, 1, stride=1, stride_axis=0)
                    bias_ref[t, p, 1, hh] = table
                    bias_ref[t, p, 0, hh] = jnp.where(has_prev, table, MASKED)

    kk_ref[:, 0:sb_len, :] = kp_ref[...]
    kk_ref[:, sb_len:2 * sb_len, :] = kc_ref[...]
    vv_ref[:, 0:sb_len, :] = vp_ref[...]
    vv_ref[:, sb_len:2 * sb_len, :] = vc_ref[...]
    nq = ATTN_RESIDUE_PARTS
    sbq = sb_len // nq
    for t in range(n_t):
        for c in range(nq):
            qq_ref[t, c] = q_ref[t, pl.ds(c, sbq, stride=nq), :]
            kq_ref[t, c, 0:sbq, :] = kp_ref[t, pl.ds(c, sbq, stride=nq), :]
            kq_ref[t, c, sbq:2 * sbq, :] = kc_ref[t, pl.ds(c, sbq, stride=nq), :]
            vq_ref[t, c, 0:sbq, :] = vp_ref[t, pl.ds(c, sbq, stride=nq), :]
            vq_ref[t, c, sbq:2 * sbq, :] = vc_ref[t, pl.ds(c, sbq, stride=nq), :]
    heads_per_tile = LANES // A_HEAD_DIM
    assert heads_per_tile == 2

    for p, (window, d) in enumerate(patterns):
        band = window // d
        blocks_per_residue = sb_len // window
        first_head = lax.broadcasted_iota(jnp.int32, (band, LANES), 1) < A_HEAD_DIM
        first_head_keys = lax.broadcasted_iota(jnp.int32, (2 * band, LANES), 1) < A_HEAD_DIM

        def rows(start, n, d=d):
            return pl.ds(start, n, stride=d) if d > 1 else pl.ds(start, n)

        def block(i, carry, p=p, d=d, band=band, blocks_per_residue=blocks_per_residue, first_head=first_head,
                  first_head_keys=first_head_keys, rows=rows, nq=nq, sbq=sbq):
            r = i // blocks_per_residue
            jb = i % blocks_per_residue
            start = r + jb * (band * d)
            variant = jnp.where(jnp.logical_and(first_sb, jb == 0), 0, 1)
            for t in range(n_t):
                if d % nq == 0:
                    e = d // nq
                    part, first_row = r % nq, r // nq + jb * (band * e)
                    rows_e = lambda a, n: pl.ds(a, n, stride=e) if e > 1 else pl.ds(a, n)
                    qf = qq_ref[t, part, rows_e(first_row, band), :]
                    kf = kq_ref[t, part, rows_e(sbq + first_row - band * e, 2 * band), :]
                    vf = vq_ref[t, part, rows_e(sbq + first_row - band * e, 2 * band), :]
                else:
                    qf = q_ref[t, rows(start, band), :]
                    kf = kk_ref[t, rows(sb_len + start - band * d, 2 * band), :]
                    vf = vv_ref[t, rows(sb_len + start - band * d, 2 * band), :]
                q2 = (qf * (scale * LOG2E)).astype(BF16)
                k2 = kf.astype(BF16)
                v2 = vf.astype(BF16)
                zeros = jnp.zeros_like(v2)
                es, ms = [], []
                for hh in range(heads_per_tile):
                    mine = first_head if hh == 0 else jnp.logical_not(first_head)
                    qm = jnp.where(mine, q2, jnp.zeros_like(q2))
                    s = lax.dot_general(qm, k2, (((1,), (1,)), ((), ())), preferred_element_type=F32)
                    s = s + bias_ref[t, p, variant, hh]
                    m = jnp.max(s, axis=1, keepdims=True)
                    es.append(jnp.exp2(s - m).astype(BF16))
                    ms.append(m)
                own0 = first_head_keys.astype(BF16)
                rhs = jnp.concatenate([jnp.concatenate([jnp.where(first_head_keys, v2, zeros), own0], axis=1),
                                       jnp.concatenate([jnp.where(first_head_keys, zeros, v2), 1 - own0], axis=1)],
                                      axis=0)
                both = jnp.dot(jnp.concatenate(es, axis=1), rhs, preferred_element_type=F32)
                num, den = both[:, :LANES], both[:, LANES:]
                po_ref[t, p, rows(start, band), :] = num / den
                pl_ref[t, p, rows(start, band), :] = jnp.where(first_head, ms[0], ms[1]) + jnp.log(den) * LOG2E
            return carry

        lax.fori_loop(0, sb_len // band, block, 0, unroll=ATTN_BLOCKS_PER_BODY // n_t)

    chunk = 256
    for t in range(n_t):
        for c in range(sb_len // chunk):
            sl = slice(chunk * c, chunk * (c + 1))
            lse = [pl_ref[t, p, sl, :] for p in range(len(patterns))]
            top = functools.reduce(jnp.maximum, lse)
            wgt = [jnp.exp2(l - top) for l in lse]
            num = sum(w * po_ref[t, p, sl, :] for p, w in enumerate(wgt))
            o_ref[0, sl, LANES * t:LANES * (t + 1)] = (num / sum(wgt)).astype(o_ref.dtype)


def _attention(qkv, rel_bias, aw):
    _, b_sz, s_len, _ = qkv.shape
    sb_len = max(w for w, _ in DILATED_PATTERNS)
    band = DILATED_PATTERNS[0][0] // DILATED_PATTERNS[0][1]
    assert all(w // d == band and sb_len % w == 0 for w, d in DILATED_PATTERNS)
    assert s_len % sb_len == 0 and band % LANES == 0
    n_pairs = aw // LANES
    n_pat = len(DILATED_PATTERNS)
    n_t = ATTN_PAIRS_PER_STEP
    assert n_pairs % n_t == 0 and ATTN_BLOCKS_PER_BODY % n_t == 0
    steps_j = n_pairs // n_t
    brow = jnp.stack([_band_bias_rows(rel_bias, band, d) for _, d in DILATED_PATTERNS])
    brow = brow.reshape(n_pat, n_pairs, 2, 2 * band).transpose(1, 0, 2, 3).reshape(n_pairs, n_pat * 2, 2 * band)

    def blk(which, prev):
        if prev:
            return pl.BlockSpec((n_t, None, sb_len, LANES),
                                lambda j, b, s: (which * steps_j + j, b, jnp.maximum(s - 1, 0), 0))
        return pl.BlockSpec((n_t, None, sb_len, LANES), lambda j, b, s: (which * steps_j + j, b, s, 0))

    return pl.pallas_call(
        functools.partial(_attn_kernel, patterns=DILATED_PATTERNS, scale=A_HEAD_DIM ** -0.5),
        grid=(steps_j, b_sz, s_len // sb_len),
        in_specs=[blk(0, False), blk(1, True), blk(1, False), blk(2, True), blk(2, False),
                  pl.BlockSpec((n_t, n_pat * 2, 2 * band), lambda j, b, s: (j, 0, 0))],
        out_specs=pl.BlockSpec((1, sb_len, n_t * LANES), lambda j, b, s: (b, s, j)),
        out_shape=jax.ShapeDtypeStruct((b_sz, s_len, aw), BF16),
        scratch_shapes=[pltpu.VMEM((n_t, 2 * sb_len, LANES), F32), pltpu.VMEM((n_t, 2 * sb_len, LANES), F32),
                        pltpu.VMEM((n_t, n_pat, sb_len, LANES), F32), pltpu.VMEM((n_t, n_pat, sb_len, LANES), F32),
                        pltpu.VMEM((n_t, n_pat, 2, 2, band, 2 * band), F32),
                        pltpu.VMEM((n_t, ATTN_RESIDUE_PARTS, sb_len // ATTN_RESIDUE_PARTS, LANES), F32),
                        pltpu.VMEM((n_t, ATTN_RESIDUE_PARTS, 2 * sb_len // ATTN_RESIDUE_PARTS, LANES), F32),
                        pltpu.VMEM((n_t, ATTN_RESIDUE_PARTS, 2 * sb_len // ATTN_RESIDUE_PARTS, LANES), F32)],
        compiler_params=_cparams(3),
        name="dilated_attn",
    )(qkv, qkv, qkv, qkv, qkv, brow)


def _mlstm_front_kernel(x_ref, halo_ref, cw_ref, cb_ref, wblk_ref, wif_ref, bif_ref,
                        q_ref, k_ref, v_ref, xc_ref, gate_ref, xe_ref, wbd_ref, *, tiles_per_seq, k_scale, blk):
    tm, width = x_ref.shape
    taps = cw_ref.shape[0]
    pad = halo_ref.shape[0]

    @pl.when(pl.program_id(0) == 0)
    def _():
        r_id = lax.broadcasted_iota(jnp.int32, (MXU_DIM, MXU_DIM), 0)
        c_id = lax.broadcasted_iota(jnp.int32, (MXU_DIM, MXU_DIM), 1)
        same_block = (r_id // blk) == (c_id // blk)
        o_id = lax.broadcasted_iota(jnp.int32, (LANES, MXU_DIM), 0)
        spread = (o_id == lax.broadcasted_iota(jnp.int32, (LANES, MXU_DIM), 1) % blk).astype(BF16)
        for kind in range(wblk_ref.shape[0]):
            for j in range(width // MXU_DIM):
                rows = wblk_ref[kind, MXU_DIM * j:MXU_DIM * (j + 1), :].astype(BF16)
                tile = jnp.dot(rows, spread, preferred_element_type=F32)
                wbd_ref[kind, j] = jnp.where(same_block, tile, 0.0).astype(BF16)

    first = (pl.program_id(0) % tiles_per_seq) == 0
    halo = halo_ref[...].astype(F32)
    xe_ref[0:pad, :] = jnp.where(first, jnp.zeros_like(halo), halo)
    xe_ref[pad:pad + tm, :] = x_ref[...].astype(F32)
    n_tiles = width // MXU_DIM
    for r0 in range(0, tm, FRONT_ROW_CHUNK):
        rs = slice(r0, r0 + FRONT_ROW_CHUNK)
        xmb = x_ref[rs, :]
        y = cb_ref[...]
        for back in range(taps):
            y = y + (xe_ref[pad + r0 - back:pad + r0 - back + FRONT_ROW_CHUNK, :]
                     * cw_ref[taps - 1 - back:taps - back, :])
        xcb = _silu(y).astype(BF16)
        xc_ref[rs, :] = xcb
        qbs, kbs, vbs = [], [], []
        for j in range(n_tiles):
            sl = slice(MXU_DIM * j, MXU_DIM * (j + 1))
            qj = jnp.dot(xcb[:, sl], wbd_ref[0, j], preferred_element_type=F32)
            kj = jnp.dot(xcb[:, sl], wbd_ref[1, j], preferred_element_type=F32)
            vj = jnp.dot(xmb[:, sl], wbd_ref[2, j], preferred_element_type=F32)
            qbs.append(qj.astype(BF16))
            kbs.append(kj.astype(BF16))
            vbs.append(vj.astype(BF16))
            k_ref[rs, sl] = (kj * k_scale).astype(BF16)
        qb, kb, vb = (jnp.concatenate(t, axis=1) for t in (qbs, kbs, vbs))
        q_ref[rs, :] = qb
        v_ref[rs, :] = vb
        gate_ref[rs, :] = (bif_ref[...]
                           + jnp.dot(qb, wif_ref[0:width, :], preferred_element_type=F32)
                           + jnp.dot(kb, wif_ref[width:2 * width, :], preferred_element_type=F32)
                           + jnp.dot(vb, wif_ref[2 * width:3 * width, :], preferred_element_type=F32))


def _mlstm_front(rest2, conv_w, conv_b, w_blocks, wif_pad, bif_pad, s_len, width, tm):
    n = rest2.shape[0]
    halo = 16
    n_kinds, n_blocks, blk, _ = w_blocks.shape
    assert conv_w.shape[0] - 1 <= halo and n_blocks * blk == width and MXU_DIM % blk == 0
    wblk = jnp.pad(w_blocks.reshape(n_kinds, width, blk), ((0, 0), (0, 0), (0, LANES - blk)))
    const = lambda *shape: pl.BlockSpec(shape, lambda i: (0,) * len(shape))
    tok = lambda w: pl.BlockSpec((tm, w), lambda i: (i, 0))
    return pl.pallas_call(
        functools.partial(_mlstm_front_kernel, tiles_per_seq=s_len // tm,
                          k_scale=(width // M_HEADS) ** -0.5, blk=blk),
        grid=(n // tm,),
        in_specs=[tok(width),
                  pl.BlockSpec((halo, width), lambda i: (jnp.maximum(i * (tm // halo) - 1, 0), 0)),
                  const(*conv_w.shape), const(1, width),
                  const(*wblk.shape), const(*wif_pad.shape), const(1, LANES)],
        out_specs=[tok(width), tok(width), tok(width), tok(width), tok(LANES)],
        out_shape=[jax.ShapeDtypeStruct((n, width), BF16)] * 4 + [jax.ShapeDtypeStruct((n, LANES), F32)],
        scratch_shapes=[pltpu.VMEM((tm + halo, width), F32),
                        pltpu.VMEM((n_kinds, width // MXU_DIM, MXU_DIM, MXU_DIM), BF16)],
        compiler_params=_cparams(1),
        name="mlstm_front",
    )(rest2, rest2, conv_w, conv_b, wblk, wif_pad, bif_pad)


def _split3(x):
    hi = x.astype(BF16)
    r1 = x - hi.astype(F32)
    mid = r1.astype(BF16)
    lo = (r1 - mid.astype(F32)).astype(BF16)
    return hi, mid, lo


def _mlstm_core_kernel(q_ref, k_ref, v_ref, grow_ref, om_ref, zm_ref, xc_ref, hg_ref, skip_ref,
                       y_ref, c_ref, n_ref, m_ref):
    @pl.when(pl.program_id(2) == 0)
    def _():
        c_ref[...] = jnp.zeros_like(c_ref)
        n_ref[...] = jnp.zeros_like(n_ref)
        m_ref[...] = jnp.zeros_like(m_ref)

    lc = q_ref.shape[1]
    row_id = lax.broadcasted_iota(jnp.int32, (lc, lc), 0)
    col_id = lax.broadcasted_iota(jnp.int32, (lc, lc), 1)
    causal = row_id >= col_id
    upper = (row_id <= col_id).astype(BF16)
    gate_row = lax.broadcasted_iota(jnp.int32, (GATE_ROWS, lc), 0)
    for s in range(q_ref.shape[0]):
        q, k, v = q_ref[s], k_ref[s], v_ref[s]
        pre = grow_ref[s, 0]
        lf = jnp.minimum(pre, 0.0) - jnp.log1p(jnp.exp(-jnp.abs(pre)))
        csum = sum(jnp.dot(part, upper, preferred_element_type=F32) for part in _split3(lf))
        grow = jnp.where(gate_row == 0, pre, csum)
        gcol = grow.T
        i_row, b_row = grow[0:1, :], grow[1:2, :]
        i_col, b_col = gcol[:, 0:1], gcol[:, 1:2]
        m_prev = m_ref[s]
        g = b_row[:, lc - 1:lc]

        dmat = jnp.where(causal, b_col - b_row + i_row, MASKED)
        inter = b_col + m_prev
        m_t = jnp.maximum(inter, jnp.max(dmat, axis=1, keepdims=True))
        qk = lax.dot_general(q, k, (((1,), (1,)), ((), ())), preferred_element_type=F32) * jnp.exp(dmat - m_t)
        w_inter = jnp.exp(inter - m_t)
        c_old = c_ref[s]
        num = (w_inter * jnp.dot(q, c_old.astype(BF16), preferred_element_type=F32)
               + jnp.dot(qk.astype(BF16), v, preferred_element_type=F32))
        den = (w_inter * jnp.sum(q.astype(F32) * n_ref[s], axis=1, keepdims=True)
               + jnp.sum(qk, axis=1, keepdims=True))
        h = num / jnp.maximum(jnp.abs(den), jnp.exp(-m_t))

        m_new = jnp.maximum(g + m_prev, jnp.max(g - b_row + i_row, axis=1, keepdims=True))
        w_s = jnp.exp(g - b_col + i_col - m_new)
        decay = jnp.exp(g + m_prev - m_new)
        kw = (k.astype(F32) * w_s).astype(BF16)
        c_ref[s] = decay * c_old + lax.dot_general(kw, v, (((0,), (0,)), ((), ())), preferred_element_type=F32)
        n_ref[s] = decay * n_ref[s] + jnp.dot(jnp.ones((8, lc), BF16), kw, preferred_element_type=F32)[0:1]
        m_ref[s] = m_new

        hgated = _sigmoid(om_ref[s].astype(F32)) * h
        mu = jnp.mean(hgated, axis=1, keepdims=True)
        cen = hgated - mu
        var = jnp.mean(cen * cen, axis=1, keepdims=True)
        hn = cen * lax.rsqrt(var + EPS) * hg_ref[...]
        zm = zm_ref[s].astype(F32)
        y_ref[s] = ((hn + skip_ref[...] * xc_ref[s].astype(F32)) * _silu(zm)).astype(y_ref.dtype)


def _mlstm_core(q, k, v, grow, rest3, xc, head_norm_g, skip, lc, om_col0, zm_col0):
    b_sz, s_len, width = q.shape
    dh = width // M_HEADS
    grp = MLSTM_GROUP
    assert b_sz % grp == 0
    seq = lambda col0: pl.BlockSpec((grp, lc, dh), lambda b, h, c: (b, c, col0 + h))
    vec = pl.BlockSpec((1, dh), lambda b, h, c: (0, h))
    return pl.pallas_call(
        _mlstm_core_kernel,
        grid=(b_sz // grp, M_HEADS, s_len // lc),
        in_specs=[seq(0), seq(0), seq(0),
                  pl.BlockSpec((grp, 1, GATE_ROWS, lc), lambda b, h, c: (b, h, 0, c)),
                  seq(om_col0 // dh), seq(zm_col0 // dh), seq(0), vec, vec],
        out_specs=seq(0),
        out_shape=jax.ShapeDtypeStruct((b_sz, s_len, width), BF16),
        scratch_shapes=[pltpu.VMEM((grp, dh, dh), F32), pltpu.VMEM((grp, 1, dh), F32),
                        pltpu.VMEM((grp, 1, 1), F32)],
        compiler_params=_cparams(3),
        name="mlstm_core",
    )(q, k, v, grow, rest3, rest3, xc, head_norm_g, skip)


def _merge_out_kernel(ya_ref, za_ref, gates_ref, gb_ref, ym_ref, x_ref, wpa_ref, wpb_ref, wout_ref, gout_ref,
                      out_ref):
    d_model = x_ref.shape[1]
    za = za_ref[...].astype(F32)
    ya = jnp.dot((ya_ref[...].astype(F32) * _silu(za)).astype(BF16), wpa_ref[...],
                 preferred_element_type=F32)
    ym = jnp.dot(ym_ref[...], wpb_ref[...], preferred_element_type=F32)
    gate = _sigmoid(gates_ref[...].astype(F32) + gb_ref[...])
    merged = gate[:, :d_model] * ya + gate[:, d_model:] * ym
    hres = x_ref[...] + jnp.dot(merged.astype(BF16), wout_ref[...], preferred_element_type=F32)
    ms = jnp.mean(hres * hres, axis=-1, keepdims=True)
    out_ref[...] = hres * lax.rsqrt(ms + EPS) * gout_ref[...]


def _merge_out(ya2, proj2, gate_b, ym2, x2, wpa, wpb, wout, gout, za_col0, gates_col0, tm):
    n, d_model = x2.shape
    aw = ya2.shape[1]
    mw = ym2.shape[1]
    tok = lambda w, cb=0: pl.BlockSpec((tm, w), lambda i: (i, cb))
    const = lambda *shape: pl.BlockSpec(shape, lambda i: (0,) * len(shape))
    return pl.pallas_call(
        _merge_out_kernel,
        grid=(n // tm,),
        in_specs=[tok(aw), tok(aw, za_col0 // aw), tok(2 * d_model, gates_col0 // (2 * d_model)),
                  const(1, 2 * d_model), tok(mw), tok(d_model), const(aw, d_model), const(mw, d_model),
                  const(d_model, d_model), const(1, d_model)],
        out_specs=tok(d_model),
        out_shape=jax.ShapeDtypeStruct((n, d_model), F32),
        compiler_params=_cparams(1),
        name="merge_out",
    )(ya2, proj2, proj2, gate_b, ym2, x2, wpa, wpb, wout, gout)


def _layer(h2, b_sz, s_len, norm_in_g, w_in, gate_b, conv_w, conv_b, wq_m, wk_m, wv_m, w_if, b_if,
           head_norm_g, skip_m, w_pa, w_pb, w_out, rel_bias, gout):
    n, d_model = h2.shape
    aw = w_pa.shape[0]
    mw = w_pb.shape[0]
    assert aw == A_HEADS * A_HEAD_DIM and aw == d_model and mw == 2 * d_model
    wb = w_in.astype(BF16)
    xm_col0, zm_col0, om_col0, gates_col0, za_col0 = 0, mw, 2 * mw, 3 * mw, 3 * mw + 2 * d_model
    n_rest = za_col0 + aw
    tn = T_INPROJ_COLS
    assert aw == tn
    rest_first = 4 * aw // tn
    rest_tiles_before_za = (n_rest - aw) // tn
    n_rest_tiles = n_rest // tn
    za_block = 3 * aw // tn

    def w_col_block(j):
        return jnp.where(j < rest_tiles_before_za, j + rest_first,
                         jnp.where(j < n_rest_tiles, za_block, j - n_rest_tiles))

    rest2, qkv = _norm_inproj(h2, norm_in_g.reshape(1, d_model), wb, w_col_block, n_rest, 3 * aw,
                              tm=T_INPROJ_ROWS, tn=tn)
    to3 = lambda t: t.reshape(b_sz, s_len, t.shape[-1])

    ya = _attention(qkv.reshape(3 * aw // LANES, b_sz, s_len, LANES), rel_bias, aw)

    assert xm_col0 == 0
    wif_pad = jnp.pad(w_if, ((0, 0), (0, LANES - w_if.shape[1]))).astype(BF16)
    bif_pad = jnp.pad(b_if, (0, LANES - b_if.shape[0])).reshape(1, LANES)
    q, k, v, xc, gate_pre = _mlstm_front(
        rest2, conv_w, conv_b.reshape(1, mw), jnp.stack([wq_m, wk_m, wv_m]), wif_pad, bif_pad,
        s_len, mw, tm=T_FRONT_ROWS)
    grow = jnp.transpose(gate_pre[:, :2 * M_HEADS].reshape(b_sz, s_len, 2, M_HEADS), (0, 3, 2, 1))
    grow = jnp.pad(grow, ((0, 0), (0, 0), (0, GATE_ROWS - 2), (0, 0)))
    ym = _mlstm_core(to3(q), to3(k), to3(v), grow, to3(rest2), to3(xc),
                     head_norm_g.reshape(1, mw), skip_m.reshape(1, mw), MLSTM_CHUNK, om_col0, zm_col0)

    return _merge_out(ya.reshape(n, aw), rest2, gate_b.reshape(1, 2 * d_model), ym.reshape(n, mw), h2,
                      w_pa.astype(BF16), w_pb.astype(BF16), w_out.astype(BF16), gout.reshape(1, d_model),
                      za_col0, gates_col0, tm=T_MERGE_ROWS)


def kernel(x, norm_in_g, w_in, gate_b, conv_w, conv_b, wq_m, wk_m, wv_m, w_if, b_if, head_norm_g, skip_m,
           w_pa, w_pb, w_out, rel_bias, norm_out_g):
    b_sz, s_len, d_model = x.shape
    depth = w_in.shape[0]
    assert depth == 1
    out = _layer(x.reshape(b_sz * s_len, d_model), b_sz, s_len, norm_in_g[0], w_in[0], gate_b[0], conv_w[0],
                 conv_b[0], wq_m[0], wk_m[0], wv_m[0], w_if[0], b_if[0], head_norm_g[0], skip_m[0], w_pa[0],
                 w_pb[0], w_out[0], rel_bias, norm_out_g)
    return out.reshape(b_sz, s_len, d_model)
```

```python
import functools
import math

import jax
import jax.numpy as jnp
from jax import lax
from jax.experimental import pallas as pl
from jax.experimental.pallas import tpu as pltpu

F32 = jnp.float32
BF16 = jnp.bfloat16

A_HEADS = 16
A_HEAD_DIM = 64
DILATED_PATTERNS = ((128, 1), (512, 4), (2048, 16))
MAX_DISTANCE = 2048
M_HEADS = 4
EPS = 1e-6
MASKED = -1e30
LOG2E = math.log2(math.e)


def _sigmoid(x):
    return 0.5 * jnp.tanh(0.5 * x) + 0.5


def _silu(x):
    h = 0.5 * x
    return h * jnp.tanh(h) + h


LANES = 128
MXU_DIM = 256
VMEM_LIMIT_BYTES = 56 * 1024 * 1024

MLSTM_CHUNK = 256
MLSTM_GROUP = 8
GATE_ROWS = 8
ATTN_PAIRS_PER_STEP = 2
ATTN_RESIDUE_PARTS = 4
ATTN_BLOCKS_PER_BODY = 16

T_INPROJ_ROWS = 2048
T_INPROJ_COLS = 1024
T_FRONT_ROWS = 512
FRONT_ROW_CHUNK = 256
T_MERGE_ROWS = 512


def _cparams(n_axes):
    return pltpu.CompilerParams(dimension_semantics=("arbitrary",) * n_axes,
                                vmem_limit_bytes=VMEM_LIMIT_BYTES)


def _norm_inproj_kernel(x_ref, g_ref, w_ref, rest_ref, slab_ref, xn_ref, *, n_rest_tiles):
    j = pl.program_id(1)

    @pl.when(j == 0)
    def _():
        xf = x_ref[...]
        ms = jnp.mean(xf * xf, axis=-1, keepdims=True)
        xn_ref[...] = (xf * lax.rsqrt(ms + EPS) * g_ref[...]).astype(BF16)

    @pl.when(j < n_rest_tiles)
    def _():
        rest_ref[...] = jnp.dot(xn_ref[...], w_ref[...], preferred_element_type=F32).astype(rest_ref.dtype)

    @pl.when(j >= n_rest_tiles)
    def _():
        o = jnp.dot(xn_ref[...], w_ref[...], preferred_element_type=F32)
        for c in range(slab_ref.shape[0]):
            slab_ref[c] = o[:, LANES * c:LANES * (c + 1)]


def _norm_inproj(x2, g, w, w_col_block, n_rest, n_slab_cols, tm, tn):
    n, d = x2.shape
    n_rest_tiles = n_rest // tn
    n_slab_tiles = n_slab_cols // tn
    return pl.pallas_call(
        functools.partial(_norm_inproj_kernel, n_rest_tiles=n_rest_tiles),
        grid=(n // tm, n_rest_tiles + n_slab_tiles),
        in_specs=[pl.BlockSpec((tm, d), lambda i, j: (i, 0)),
                  pl.BlockSpec((1, d), lambda i, j: (0, 0)),
                  pl.BlockSpec((d, tn), lambda i, j: (0, w_col_block(j)))],
        out_specs=[pl.BlockSpec((tm, tn), lambda i, j: (i, jnp.minimum(j, n_rest_tiles - 1))),
                   pl.BlockSpec((tn // LANES, tm, LANES), lambda i, j: (jnp.maximum(j - n_rest_tiles, 0), i, 0))],
        out_shape=[jax.ShapeDtypeStruct((n, n_rest), BF16),
                   jax.ShapeDtypeStruct((n_slab_cols // LANES, n, LANES), F32)],
        scratch_shapes=[pltpu.VMEM((tm, d), BF16)],
        compiler_params=_cparams(2),
        name="norm_inproj",
    )(x2, g, w)


def _t5_bucket(dist, n_buckets):
    max_exact = n_buckets // 2
    large = max_exact + (jnp.log(jnp.maximum(dist, max_exact).astype(F32) / max_exact)
                         / math.log(MAX_DISTANCE / max_exact) * (n_buckets - max_exact)).astype(jnp.int32)
    return jnp.where(dist < max_exact, dist, jnp.minimum(large, n_buckets - 1))


def _band_bias_rows(rel_bias, band, dilation):
    delta = jnp.arange(band + 1)
    vals = rel_bias.astype(F32)[_t5_bucket(delta * dilation, rel_bias.shape[0])] * LOG2E
    return jnp.concatenate([vals[::-1].T, jnp.full((rel_bias.shape[1], band - 1), MASKED, F32)], axis=1)


def _attn_kernel(q_ref, kp_ref, kc_ref, vp_ref, vc_ref, brow_ref, o_ref, po_ref, pl_ref, bias_ref,
                 qq_ref, kq_ref, vq_ref, *, patterns, scale):
    n_t, sb_len, _ = q_ref.shape
    first_sb = pl.program_id(2) == 0

    @pl.when(jnp.logical_and(pl.program_id(1) == 0, first_sb))
    def _():
        for t in range(n_t):
            for p, (window, d) in enumerate(patterns):
                band = window // d
                has_prev = lax.broadcasted_iota(jnp.int32, (band, 2 * band), 1) >= band
                for hh in range(2):
                    row = brow_ref[t, 2 * p + hh:2 * p + hh + 1, :]
                    table = pltpu.roll(jnp.broadcast_to(row, (band, 2 * band)), 0, 1, stride=1, stride_axis=0)
                    bias_ref[t, p, 1, hh] = table
                    bias_ref[t, p, 0, hh] = jnp.where(has_prev, table, MASKED)

    nq = ATTN_RESIDUE_PARTS
    sbq = sb_len // nq
    for t in range(n_t):
        for c in range(nq):
            qq_ref[t, c] = q_ref[t, pl.ds(c, sbq, stride=nq), :]
            kq_ref[t, c, 0:sbq, :] = kp_ref[t, pl.ds(c, sbq, stride=nq), :]
            kq_ref[t, c, sbq:2 * sbq, :] = kc_ref[t, pl.ds(c, sbq, stride=nq), :]
            vq_ref[t, c, 0:sbq, :] = vp_ref[t, pl.ds(c, sbq, stride=nq), :]
            vq_ref[t, c, sbq:2 * sbq, :] = vc_ref[t, pl.ds(c, sbq, stride=nq), :]
    heads_per_tile = LANES // A_HEAD_DIM
    assert heads_per_tile == 2

    for p, (window, d) in enumerate(patterns):
        band = window // d
        blocks_per_residue = sb_len // window
        first_head = lax.broadcasted_iota(jnp.int32, (band, LANES), 1) < A_HEAD_DIM
        first_head_keys = lax.broadcasted_iota(jnp.int32, (2 * band, LANES), 1) < A_HEAD_DIM

        def rows(start, n, d=d):
            return pl.ds(start, n, stride=d) if d > 1 else pl.ds(start, n)

        def block(i, carry, p=p, d=d, band=band, blocks_per_residue=blocks_per_residue, first_head=first_head,
                  first_head_keys=first_head_keys, rows=rows, nq=nq, sbq=sbq):
            r = i // blocks_per_residue
            jb = i % blocks_per_residue
            start = r + jb * (band * d)
            variant = jnp.where(jnp.logical_and(first_sb, jb == 0), 0, 1)
            for t in range(n_t):
                if d % nq == 0:
                    e = d // nq
                    part, first_row = r % nq, r // nq + jb * (band * e)
                    rows_e = lambda a, n: pl.ds(a, n, stride=e) if e > 1 else pl.ds(a, n)
                    qf = qq_ref[t, part, rows_e(first_row, band), :]
                    kf = kq_ref[t, part, rows_e(sbq + first_row - band * e, 2 * band), :]
                    vf = vq_ref[t, part, rows_e(sbq + first_row - band * e, 2 * band), :]
                else:
                    assert d == 1
                    qf = q_ref[t, pl.ds(start, band), :]
                    inside = pl.ds(jnp.maximum(start - band, 0), band)
                    kf = jnp.concatenate([jnp.where(jb == 0, kp_ref[t, sb_len - band:sb_len, :], kc_ref[t, inside, :]),
                                          kc_ref[t, pl.ds(start, band), :]], axis=0)
                    vf = jnp.concatenate([jnp.where(jb == 0, vp_ref[t, sb_len - band:sb_len, :], vc_ref[t, inside, :]),
                                          vc_ref[t, pl.ds(start, band), :]], axis=0)
                q2 = (qf * (scale * LOG2E)).astype(BF16)
                k2 = kf.astype(BF16)
                v2 = vf.astype(BF16)
                zeros = jnp.zeros_like(v2)
                es, ms = [], []
                for hh in range(heads_per_tile):
                    mine = first_head if hh == 0 else jnp.logical_not(first_head)
                    qm = jnp.where(mine, q2, jnp.zeros_like(q2))
                    s = lax.dot_general(qm, k2, (((1,), (1,)), ((), ())), preferred_element_type=F32)
                    s = s + bias_ref[t, p, variant, hh]
                    m = jnp.max(s, axis=1, keepdims=True)
                    es.append(jnp.exp2(s - m).astype(BF16))
                    ms.append(m)
                own0 = first_head_keys.astype(BF16)
                rhs = jnp.concatenate([jnp.concatenate([jnp.where(first_head_keys, v2, zeros), own0], axis=1),
                                       jnp.concatenate([jnp.where(first_head_keys, zeros, v2), 1 - own0], axis=1)],
                                      axis=0)
                both = jnp.dot(jnp.concatenate(es, axis=1), rhs, preferred_element_type=F32)
                num, den = both[:, :LANES], both[:, LANES:]
                po_ref[t, p, rows(start, band), :] = num / den
                pl_ref[t, p, rows(start, band), :] = jnp.where(first_head, ms[0], ms[1]) + jnp.log(den) * LOG2E
            return carry

        lax.fori_loop(0, sb_len // band, block, 0, unroll=ATTN_BLOCKS_PER_BODY // n_t)

    chunk = 256
    for t in range(n_t):
        for c in range(sb_len // chunk):
            sl = slice(chunk * c, chunk * (c + 1))
            lse = [pl_ref[t, p, sl, :] for p in range(len(patterns))]
            top = functools.reduce(jnp.maximum, lse)
            wgt = [jnp.exp2(l - top) for l in lse]
            num = sum(w * po_ref[t, p, sl, :] for p, w in enumerate(wgt))
            o_ref[0, sl, LANES * t:LANES * (t + 1)] = (num / sum(wgt)).astype(o_ref.dtype)


def _attention(qkv, rel_bias, aw):
    _, b_sz, s_len, _ = qkv.shape
    sb_len = max(w for w, _ in DILATED_PATTERNS)
    band = DILATED_PATTERNS[0][0] // DILATED_PATTERNS[0][1]
    assert all(w // d == band and sb_len % w == 0 for w, d in DILATED_PATTERNS)
    assert s_len % sb_len == 0 and band % LANES == 0
    n_pairs = aw // LANES
    n_pat = len(DILATED_PATTERNS)
    n_t = ATTN_PAIRS_PER_STEP
    assert n_pairs % n_t == 0 and ATTN_BLOCKS_PER_BODY % n_t == 0
    steps_j = n_pairs // n_t
    brow = jnp.stack([_band_bias_rows(rel_bias, band, d) for _, d in DILATED_PATTERNS])
    brow = brow.reshape(n_pat, n_pairs, 2, 2 * band).transpose(1, 0, 2, 3).reshape(n_pairs, n_pat * 2, 2 * band)

    def blk(which, prev):
        if prev:
            return pl.BlockSpec((n_t, None, sb_len, LANES),
                                lambda j, b, s: (which * steps_j + j, b, jnp.maximum(s - 1, 0), 0))
        return pl.BlockSpec((n_t, None, sb_len, LANES), lambda j, b, s: (which * steps_j + j, b, s, 0))

    return pl.pallas_call(
        functools.partial(_attn_kernel, patterns=DILATED_PATTERNS, scale=A_HEAD_DIM ** -0.5),
        grid=(steps_j, b_sz, s_len // sb_len),
        in_specs=[blk(0, False), blk(1, True), blk(1, False), blk(2, True), blk(2, False),
                  pl.BlockSpec((n_t, n_pat * 2, 2 * band), lambda j, b, s: (j, 0, 0))],
        out_specs=pl.BlockSpec((1, sb_len, n_t * LANES), lambda j, b, s: (b, s, j)),
        out_shape=jax.ShapeDtypeStruct((b_sz, s_len, aw), BF16),
        scratch_shapes=[pltpu.VMEM((n_t, n_pat, sb_len, LANES), F32), pltpu.VMEM((n_t, n_pat, sb_len, LANES), F32),
                        pltpu.VMEM((n_t, n_pat, 2, 2, band, 2 * band), F32),
                        pltpu.VMEM((n_t, ATTN_RESIDUE_PARTS, sb_len // ATTN_RESIDUE_PARTS, LANES), F32),
                        pltpu.VMEM((n_t, ATTN_RESIDUE_PARTS, 2 * sb_len // ATTN_RESIDUE_PARTS, LANES), F32),
                        pltpu.VMEM((n_t, ATTN_RESIDUE_PARTS, 2 * sb_len // ATTN_RESIDUE_PARTS, LANES), F32)],
        compiler_params=_cparams(3),
        name="dilated_attn",
    )(qkv, qkv, qkv, qkv, qkv, brow)


def _mlstm_front_kernel(x_ref, halo_ref, cw_ref, cb_ref, wblk_ref, wif_ref, bif_ref,
                        q_ref, k_ref, v_ref, xc_ref, gate_ref, xe_ref, wbd_ref, *, tiles_per_seq, k_scale, blk):
    tm, width = x_ref.shape
    taps = cw_ref.shape[0]
    pad = halo_ref.shape[0]

    @pl.when(pl.program_id(0) == 0)
    def _():
        r_id = lax.broadcasted_iota(jnp.int32, (MXU_DIM, MXU_DIM), 0)
        c_id = lax.broadcasted_iota(jnp.int32, (MXU_DIM, MXU_DIM), 1)
        same_block = (r_id // blk) == (c_id // blk)
        o_id = lax.broadcasted_iota(jnp.int32, (LANES, MXU_DIM), 0)
        spread = (o_id == lax.broadcasted_iota(jnp.int32, (LANES, MXU_DIM), 1) % blk).astype(BF16)
        for kind in range(wblk_ref.shape[0]):
            for j in range(width // MXU_DIM):
                rows = wblk_ref[kind, MXU_DIM * j:MXU_DIM * (j + 1), :].astype(BF16)
                tile = jnp.dot(rows, spread, preferred_element_type=F32)
                wbd_ref[kind, j] = jnp.where(same_block, tile, 0.0).astype(BF16)

    first = (pl.program_id(0) % tiles_per_seq) == 0
    halo = halo_ref[...].astype(F32)
    xe_ref[0:pad, :] = jnp.where(first, jnp.zeros_like(halo), halo)
    xe_ref[pad:pad + tm, :] = x_ref[...].astype(F32)
    n_tiles = width // MXU_DIM
    for r0 in range(0, tm, FRONT_ROW_CHUNK):
        rs = slice(r0, r0 + FRONT_ROW_CHUNK)
        xmb = x_ref[rs, :]
        y = cb_ref[...]
        for back in range(taps):
            y = y + (xe_ref[pad + r0 - back:pad + r0 - back + FRONT_ROW_CHUNK, :]
                     * cw_ref[taps - 1 - back:taps - back, :])
        xcb = _silu(y).astype(BF16)
        xc_ref[rs, :] = xcb
        qbs, kbs, vbs = [], [], []
        for j in range(n_tiles):
            sl = slice(MXU_DIM * j, MXU_DIM * (j + 1))
            qj = jnp.dot(xcb[:, sl], wbd_ref[0, j], preferred_element_type=F32)
            kj = jnp.dot(xcb[:, sl], wbd_ref[1, j], preferred_element_type=F32)
            vj = jnp.dot(xmb[:, sl], wbd_ref[2, j], preferred_element_type=F32)
            qbs.append(qj.astype(BF16))
            kbs.append(kj.astype(BF16))
            vbs.append(vj.astype(BF16))
            k_ref[rs, sl] = (kj * k_scale).astype(BF16)
        qb, kb, vb = (jnp.concatenate(t, axis=1) for t in (qbs, kbs, vbs))
        q_ref[rs, :] = qb
        v_ref[rs, :] = vb
        gate_ref[rs, :] = (bif_ref[...]
                           + jnp.dot(qb, wif_ref[0:width, :], preferred_element_type=F32)
                           + jnp.dot(kb, wif_ref[width:2 * width, :], preferred_element_type=F32)
                           + jnp.dot(vb, wif_ref[2 * width:3 * width, :], preferred_element_type=F32))


def _mlstm_front(rest2, conv_w, conv_b, w_blocks, wif_pad, bif_pad, s_len, width, tm):
    n = rest2.shape[0]
    halo = 16
    n_kinds, n_blocks, blk, _ = w_blocks.shape
    assert conv_w.shape[0] - 1 <= halo and n_blocks * blk == width and MXU_DIM % blk == 0
    wblk = jnp.pad(w_blocks.reshape(n_kinds, width, blk), ((0, 0), (0, 0), (0, LANES - blk)))
    const = lambda *shape: pl.BlockSpec(shape, lambda i: (0,) * len(shape))
    tok = lambda w: pl.BlockSpec((tm, w), lambda i: (i, 0))
    return pl.pallas_call(
        functools.partial(_mlstm_front_kernel, tiles_per_seq=s_len // tm,
                          k_scale=(width // M_HEADS) ** -0.5, blk=blk),
        grid=(n // tm,),
        in_specs=[tok(width),
                  pl.BlockSpec((halo, width), lambda i: (jnp.maximum(i * (tm // halo) - 1, 0), 0)),
                  const(*conv_w.shape), const(1, width),
                  const(*wblk.shape), const(*wif_pad.shape), const(1, LANES)],
        out_specs=[tok(width), tok(width), tok(width), tok(width), tok(LANES)],
        out_shape=[jax.ShapeDtypeStruct((n, width), BF16)] * 4 + [jax.ShapeDtypeStruct((n, LANES), F32)],
        scratch_shapes=[pltpu.VMEM((tm + halo, width), F32),
                        pltpu.VMEM((n_kinds, width // MXU_DIM, MXU_DIM, MXU_DIM), BF16)],
        compiler_params=_cparams(1),
        name="mlstm_front",
    )(rest2, rest2, conv_w, conv_b, wblk, wif_pad, bif_pad)


def _split3(x):
    hi = x.astype(BF16)
    r1 = x - hi.astype(F32)
    mid = r1.astype(BF16)
    lo = (r1 - mid.astype(F32)).astype(BF16)
    return hi, mid, lo


def _mlstm_core_kernel(q_ref, k_ref, v_ref, grow_ref, om_ref, zm_ref, xc_ref, hg_ref, skip_ref,
                       y_ref, c_ref, n_ref, m_ref):
    @pl.when(pl.program_id(2) == 0)
    def _():
        c_ref[...] = jnp.zeros_like(c_ref)
        n_ref[...] = jnp.zeros_like(n_ref)
        m_ref[...] = jnp.zeros_like(m_ref)

    lc = q_ref.shape[1]
    row_id = lax.broadcasted_iota(jnp.int32, (lc, lc), 0)
    col_id = lax.broadcasted_iota(jnp.int32, (lc, lc), 1)
    causal = row_id >= col_id
    upper = (row_id <= col_id).astype(BF16)
    gate_row = lax.broadcasted_iota(jnp.int32, (GATE_ROWS, lc), 0)
    for s in range(q_ref.shape[0]):
        q, k, v = q_ref[s], k_ref[s], v_ref[s]
        pre = grow_ref[s, 0]
        lf = jnp.minimum(pre, 0.0) - jnp.log1p(jnp.exp(-jnp.abs(pre)))
        csum = sum(jnp.dot(part, upper, preferred_element_type=F32) for part in _split3(lf))
        grow = jnp.where(gate_row == 0, pre, csum)
        gcol = grow.T
        i_row, b_row = grow[0:1, :], grow[1:2, :]
        i_col, b_col = gcol[:, 0:1], gcol[:, 1:2]
        m_prev = m_ref[s]
        g = b_row[:, lc - 1:lc]

        dmat = jnp.where(causal, b_col - b_row + i_row, MASKED)
        inter = b_col + m_prev
        m_t = jnp.maximum(inter, jnp.max(dmat, axis=1, keepdims=True))
        qk = lax.dot_general(q, k, (((1,), (1,)), ((), ())), preferred_element_type=F32) * jnp.exp(dmat - m_t)
        w_inter = jnp.exp(inter - m_t)
        c_old = c_ref[s]
        num = (w_inter * jnp.dot(q, c_old.astype(BF16), preferred_element_type=F32)
               + jnp.dot(qk.astype(BF16), v, preferred_element_type=F32))
        den = (w_inter * jnp.sum(q.astype(F32) * n_ref[s], axis=1, keepdims=True)
               + jnp.sum(qk, axis=1, keepdims=True))
        h = num / jnp.maximum(jnp.abs(den), jnp.exp(-m_t))

        m_new = jnp.maximum(g + m_prev, jnp.max(g - b_row + i_row, axis=1, keepdims=True))
        w_s = jnp.exp(g - b_col + i_col - m_new)
        decay = jnp.exp(g + m_prev - m_new)
        kw = (k.astype(F32) * w_s).astype(BF16)
        c_ref[s] = decay * c_old + lax.dot_general(kw, v, (((0,), (0,)), ((), ())), preferred_element_type=F32)
        n_ref[s] = decay * n_ref[s] + jnp.dot(jnp.ones((8, lc), BF16), kw, preferred_element_type=F32)[0:1]
        m_ref[s] = m_new

        hgated = _sigmoid(om_ref[s].astype(F32)) * h
        mu = jnp.mean(hgated, axis=1, keepdims=True)
        cen = hgated - mu
        var = jnp.mean(cen * cen, axis=1, keepdims=True)
        hn = cen * lax.rsqrt(var + EPS) * hg_ref[...]
        zm = zm_ref[s].astype(F32)
        y_ref[s] = ((hn + skip_ref[...] * xc_ref[s].astype(F32)) * _silu(zm)).astype(y_ref.dtype)


def _mlstm_core(q, k, v, grow, rest3, xc, head_norm_g, skip, lc, om_col0, zm_col0):
    b_sz, s_len, width = q.shape
    dh = width // M_HEADS
    grp = MLSTM_GROUP
    assert b_sz % grp == 0
    seq = lambda col0: pl.BlockSpec((grp, lc, dh), lambda b, h, c: (b, c, col0 + h))
    vec = pl.BlockSpec((1, dh), lambda b, h, c: (0, h))
    return pl.pallas_call(
        _mlstm_core_kernel,
        grid=(b_sz // grp, M_HEADS, s_len // lc),
        in_specs=[seq(0), seq(0), seq(0),
                  pl.BlockSpec((grp, 1, GATE_ROWS, lc), lambda b, h, c: (b, h, 0, c)),
                  seq(om_col0 // dh), seq(zm_col0 // dh), seq(0), vec, vec],
        out_specs=seq(0),
        out_shape=jax.ShapeDtypeStruct((b_sz, s_len, width), BF16),
        scratch_shapes=[pltpu.VMEM((grp, dh, dh), F32), pltpu.VMEM((grp, 1, dh), F32),
                        pltpu.VMEM((grp, 1, 1), F32)],
        compiler_params=_cparams(3),
        name="mlstm_core",
    )(q, k, v, grow, rest3, rest3, xc, head_norm_g, skip)


def _merge_out_kernel(ya_ref, za_ref, gates_ref, gb_ref, ym_ref, x_ref, wpa_ref, wpb_ref, wout_ref, gout_ref,
                      out_ref):
    d_model = x_ref.shape[1]
    za = za_ref[...].astype(F32)
    ya = jnp.dot((ya_ref[...].astype(F32) * _silu(za)).astype(BF16), wpa_ref[...],
                 preferred_element_type=F32)
    ym = jnp.dot(ym_ref[...], wpb_ref[...], preferred_element_type=F32)
    gate = _sigmoid(gates_ref[...].astype(F32) + gb_ref[...])
    merged = gate[:, :d_model] * ya + gate[:, d_model:] * ym
    hres = x_ref[...] + jnp.dot(merged.astype(BF16), wout_ref[...], preferred_element_type=F32)
    ms = jnp.mean(hres * hres, axis=-1, keepdims=True)
    out_ref[...] = hres * lax.rsqrt(ms + EPS) * gout_ref[...]


def _merge_out(ya2, proj2, gate_b, ym2, x2, wpa, wpb, wout, gout, za_col0, gates_col0, tm):
    n, d_model = x2.shape
    aw = ya2.shape[1]
    mw = ym2.shape[1]
    tok = lambda w, cb=0: pl.BlockSpec((tm, w), lambda i: (i, cb))
    const = lambda *shape: pl.BlockSpec(shape, lambda i: (0,) * len(shape))
    return pl.pallas_call(
        _merge_out_kernel,
        grid=(n // tm,),
        in_specs=[tok(aw), tok(aw, za_col0 // aw), tok(2 * d_model, gates_col0 // (2 * d_model)),
                  const(1, 2 * d_model), tok(mw), tok(d_model), const(aw, d_model), const(mw, d_model),
                  const(d_model, d_model), const(1, d_model)],
        out_specs=tok(d_model),
        out_shape=jax.ShapeDtypeStruct((n, d_model), F32),
        compiler_params=_cparams(1),
        name="merge_out",
    )(ya2, proj2, proj2, gate_b, ym2, x2, wpa, wpb, wout, gout)


def _layer(h2, b_sz, s_len, norm_in_g, w_in, gate_b, conv_w, conv_b, wq_m, wk_m, wv_m, w_if, b_if,
           head_norm_g, skip_m, w_pa, w_pb, w_out, rel_bias, gout):
    n, d_model = h2.shape
    aw = w_pa.shape[0]
    mw = w_pb.shape[0]
    assert aw == A_HEADS * A_HEAD_DIM and aw == d_model and mw == 2 * d_model
    wb = w_in.astype(BF16)
    xm_col0, zm_col0, om_col0, gates_col0, za_col0 = 0, mw, 2 * mw, 3 * mw, 3 * mw + 2 * d_model
    n_rest = za_col0 + aw
    tn = T_INPROJ_COLS
    assert aw == tn
    rest_first = 4 * aw // tn
    rest_tiles_before_za = (n_rest - aw) // tn
    n_rest_tiles = n_rest // tn
    za_block = 3 * aw // tn

    def w_col_block(j):
        return jnp.where(j < rest_tiles_before_za, j + rest_first,
                         jnp.where(j < n_rest_tiles, za_block, j - n_rest_tiles))

    rest2, qkv = _norm_inproj(h2, norm_in_g.reshape(1, d_model), wb, w_col_block, n_rest, 3 * aw,
                              tm=T_INPROJ_ROWS, tn=tn)
    to3 = lambda t: t.reshape(b_sz, s_len, t.shape[-1])

    ya = _attention(qkv.reshape(3 * aw // LANES, b_sz, s_len, LANES), rel_bias, aw)

    assert xm_col0 == 0
    wif_pad = jnp.pad(w_if, ((0, 0), (0, LANES - w_if.shape[1]))).astype(BF16)
    bif_pad = jnp.pad(b_if, (0, LANES - b_if.shape[0])).reshape(1, LANES)
    q, k, v, xc, gate_pre = _mlstm_front(
        rest2, conv_w, conv_b.reshape(1, mw), jnp.stack([wq_m, wk_m, wv_m]), wif_pad, bif_pad,
        s_len, mw, tm=T_FRONT_ROWS)
    grow = jnp.transpose(gate_pre[:, :2 * M_HEADS].reshape(b_sz, s_len, 2, M_HEADS), (0, 3, 2, 1))
    grow = jnp.pad(grow, ((0, 0), (0, 0), (0, GATE_ROWS - 2), (0, 0)))
    ym = _mlstm_core(to3(q), to3(k), to3(v), grow, to3(rest2), to3(xc),
                     head_norm_g.reshape(1, mw), skip_m.reshape(1, mw), MLSTM_CHUNK, om_col0, zm_col0)

    return _merge_out(ya.reshape(n, aw), rest2, gate_b.reshape(1, 2 * d_model), ym.reshape(n, mw), h2,
                      w_pa.astype(BF16), w_pb.astype(BF16), w_out.astype(BF16), gout.reshape(1, d_model),
                      za_col0, gates_col0, tm=T_MERGE_ROWS)


def kernel(x, norm_in_g, w_in, gate_b, conv_w, conv_b, wq_m, wk_m, wv_m, w_if, b_if, head_norm_g, skip_m,
           w_pa, w_pb, w_out, rel_bias, norm_out_g):
    b_sz, s_len, d_model = x.shape
    depth = w_in.shape[0]
    assert depth == 1
    out = _layer(x.reshape(b_sz * s_len, d_model), b_sz, s_len, norm_in_g[0], w_in[0], gate_b[0], conv_w[0],
                 conv_b[0], wq_m[0], wk_m[0], wv_m[0], w_if[0], b_if[0], head_norm_g[0], skip_m[0], w_pa[0],
                 w_pb[0], w_out[0], rel_bias, norm_out_g)
    return out.reshape(b_sz, s_len, d_model)
```

```python
import functools
import math

import jax
import jax.numpy as jnp
from jax import lax
from jax.experimental import pallas as pl
from jax.experimental.pallas import tpu as pltpu

F32 = jnp.float32
BF16 = jnp.bfloat16

A_HEADS = 16
A_HEAD_DIM = 64
DILATED_PATTERNS = ((128, 1), (512, 4), (2048, 16))
MAX_DISTANCE = 2048
M_HEADS = 4
EPS = 1e-6
MASKED = -1e30
LOG2E = math.log2(math.e)


def _sigmoid(x):
    return 0.5 * jnp.tanh(0.5 * x) + 0.5


def _silu(x):
    h = 0.5 * x
    return h * jnp.tanh(h) + h


LANES = 128
MXU_DIM = 256
VMEM_LIMIT_BYTES = 56 * 1024 * 1024

MLSTM_CHUNK = 256
MLSTM_GROUP = 8
GATE_ROWS = 8
ATTN_PAIRS_PER_STEP = 2
ATTN_RESIDUE_PARTS = 4
ATTN_BLOCKS_PER_BODY = 16

T_INPROJ_ROWS = 2048
T_INPROJ_COLS = 1024
T_FRONT_ROWS = 512
FRONT_ROW_CHUNK = 256
T_MERGE_ROWS = 512


def _cparams(n_axes):
    return pltpu.CompilerParams(dimension_semantics=("arbitrary",) * n_axes,
                                vmem_limit_bytes=VMEM_LIMIT_BYTES)


def _norm_inproj_kernel(x_ref, g_ref, w_ref, rest_ref, slab_ref, xn_ref, *, n_rest_tiles):
    j = pl.program_id(1)

    @pl.when(j == 0)
    def _():
        xf = x_ref[...]
        ms = jnp.mean(xf * xf, axis=-1, keepdims=True)
        xn_ref[...] = (xf * lax.rsqrt(ms + EPS) * g_ref[...]).astype(BF16)

    @pl.when(j < n_rest_tiles)
    def _():
        rest_ref[...] = jnp.dot(xn_ref[...], w_ref[...], preferred_element_type=F32).astype(rest_ref.dtype)

    @pl.when(j >= n_rest_tiles)
    def _():
        o = jnp.dot(xn_ref[...], w_ref[...], preferred_element_type=F32)
        for c in range(slab_ref.shape[0]):
            slab_ref[c] = o[:, LANES * c:LANES * (c + 1)]


def _norm_inproj(x2, g, w, w_col_block, n_rest, n_slab_cols, tm, tn):
    n, d = x2.shape
    n_rest_tiles = n_rest // tn
    n_slab_tiles = n_slab_cols // tn
    return pl.pallas_call(
        functools.partial(_norm_inproj_kernel, n_rest_tiles=n_rest_tiles),
        grid=(n // tm, n_rest_tiles + n_slab_tiles),
        in_specs=[pl.BlockSpec((tm, d), lambda i, j: (i, 0)),
                  pl.BlockSpec((1, d), lambda i, j: (0, 0)),
                  pl.BlockSpec((d, tn), lambda i, j: (0, w_col_block(j)))],
        out_specs=[pl.BlockSpec((tm, tn), lambda i, j: (i, jnp.minimum(j, n_rest_tiles - 1))),
                   pl.BlockSpec((tn // LANES, tm, LANES), lambda i, j: (jnp.maximum(j - n_rest_tiles, 0), i, 0))],
        out_shape=[jax.ShapeDtypeStruct((n, n_rest), BF16),
                   jax.ShapeDtypeStruct((n_slab_cols // LANES, n, LANES), F32)],
        scratch_shapes=[pltpu.VMEM((tm, d), BF16)],
        compiler_params=_cparams(2),
        name="norm_inproj",
    )(x2, g, w)


def _t5_bucket(dist, n_buckets):
    max_exact = n_buckets // 2
    large = max_exact + (jnp.log(jnp.maximum(dist, max_exact).astype(F32) / max_exact)
                         / math.log(MAX_DISTANCE / max_exact) * (n_buckets - max_exact)).astype(jnp.int32)
    return jnp.where(dist < max_exact, dist, jnp.minimum(large, n_buckets - 1))


def _band_bias_rows(rel_bias, band, dilation):
    delta = jnp.arange(band + 1)
    vals = rel_bias.astype(F32)[_t5_bucket(delta * dilation, rel_bias.shape[0])] * LOG2E
    return jnp.concatenate([vals[::-1].T, jnp.full((rel_bias.shape[1], band - 1), MASKED, F32)], axis=1)


def _attn_kernel(q_ref, kc_ref, vc_ref, brow_ref, o_ref, po_ref, pl_ref, bias_ref,
                 qq_ref, kq_ref, vq_ref, kl_ref, vl_ref, *, patterns, scale):
    n_t, sb_len, _ = q_ref.shape
    first_sb = pl.program_id(2) == 0

    @pl.when(jnp.logical_and(pl.program_id(1) == 0, first_sb))
    def _():
        for t in range(n_t):
            for p, (window, d) in enumerate(patterns):
                band = window // d
                has_prev = lax.broadcasted_iota(jnp.int32, (band, 2 * band), 1) >= band
                for hh in range(2):
                    row = brow_ref[t, 2 * p + hh:2 * p + hh + 1, :]
                    table = pltpu.roll(jnp.broadcast_to(row, (band, 2 * band)), 0, 1, stride=1, stride_axis=0)
                    bias_ref[t, p, 1, hh] = table
                    bias_ref[t, p, 0, hh] = jnp.where(has_prev, table, MASKED)

    nq = ATTN_RESIDUE_PARTS
    sbq = sb_len // nq
    cur = pl.program_id(2) % 2
    prv = 1 - cur

    @pl.when((pl.program_id(0) == 0) & (pl.program_id(1) == 0) & first_sb)
    def _():
        kq_ref[...] = jnp.zeros_like(kq_ref)
        vq_ref[...] = jnp.zeros_like(vq_ref)
        kl_ref[...] = jnp.zeros_like(kl_ref)
        vl_ref[...] = jnp.zeros_like(vl_ref)

    for t in range(n_t):
        for c in range(nq):
            qq_ref[t, c] = q_ref[t, pl.ds(c, sbq, stride=nq), :]
            kq_ref[t, cur, c] = kc_ref[t, pl.ds(c, sbq, stride=nq), :]
            vq_ref[t, cur, c] = vc_ref[t, pl.ds(c, sbq, stride=nq), :]
    heads_per_tile = LANES // A_HEAD_DIM
    assert heads_per_tile == 2

    for p, (window, d) in enumerate(patterns):
        band = window // d
        blocks_per_residue = sb_len // window
        first_head = lax.broadcasted_iota(jnp.int32, (band, LANES), 1) < A_HEAD_DIM
        first_head_keys = lax.broadcasted_iota(jnp.int32, (2 * band, LANES), 1) < A_HEAD_DIM

        def rows(start, n, d=d):
            return pl.ds(start, n, stride=d) if d > 1 else pl.ds(start, n)

        def block(i, carry, p=p, d=d, band=band, blocks_per_residue=blocks_per_residue, first_head=first_head,
                  first_head_keys=first_head_keys, rows=rows, nq=nq, sbq=sbq):
            r = i // blocks_per_residue
            jb = i % blocks_per_residue
            start = r + jb * (band * d)
            variant = jnp.where(jnp.logical_and(first_sb, jb == 0), 0, 1)
            for t in range(n_t):
                if d % nq == 0:
                    e = d // nq
                    part, first_row = r % nq, r // nq + jb * (band * e)
                    rows_e = lambda a, n: pl.ds(a, n, stride=e) if e > 1 else pl.ds(a, n)
                    qf = qq_ref[t, part, rows_e(first_row, band), :]
                    carried = rows_e(sbq - band * e + r // nq, band)
                    k_before, v_before = kq_ref[t, prv, part, carried, :], vq_ref[t, prv, part, carried, :]
                    if blocks_per_residue > 1:
                        inside = rows_e(jnp.maximum(first_row - band * e, 0), band)
                        k_before = jnp.where(jb == 0, k_before, kq_ref[t, cur, part, inside, :])
                        v_before = jnp.where(jb == 0, v_before, vq_ref[t, cur, part, inside, :])
                    kf = jnp.concatenate([k_before, kq_ref[t, cur, part, rows_e(first_row, band), :]], axis=0)
                    vf = jnp.concatenate([v_before, vq_ref[t, cur, part, rows_e(first_row, band), :]], axis=0)
                else:
                    assert d == 1
                    qf = q_ref[t, pl.ds(start, band), :]
                    inside = pl.ds(jnp.maximum(start - band, 0), band)
                    kf = jnp.concatenate([jnp.where(jb == 0, kl_ref[t], kc_ref[t, inside, :]),
                                          kc_ref[t, pl.ds(start, band), :]], axis=0)
                    vf = jnp.concatenate([jnp.where(jb == 0, vl_ref[t], vc_ref[t, inside, :]),
                                          vc_ref[t, pl.ds(start, band), :]], axis=0)
                q2 = (qf * (scale * LOG2E)).astype(BF16)
                k2 = kf.astype(BF16)
                v2 = vf.astype(BF16)
                zeros = jnp.zeros_like(v2)
                es, ms = [], []
                for hh in range(heads_per_tile):
                    mine = first_head if hh == 0 else jnp.logical_not(first_head)
                    qm = jnp.where(mine, q2, jnp.zeros_like(q2))
                    s = lax.dot_general(qm, k2, (((1,), (1,)), ((), ())), preferred_element_type=F32)
                    s = s + bias_ref[t, p, variant, hh]
                    m = jnp.max(s, axis=1, keepdims=True)
                    es.append(jnp.exp2(s - m).astype(BF16))
                    ms.append(m)
                own0 = first_head_keys.astype(BF16)
                rhs = jnp.concatenate([jnp.concatenate([jnp.where(first_head_keys, v2, zeros), own0], axis=1),
                                       jnp.concatenate([jnp.where(first_head_keys, zeros, v2), 1 - own0], axis=1)],
                                      axis=0)
                both = jnp.dot(jnp.concatenate(es, axis=1), rhs, preferred_element_type=F32)
                num, den = both[:, :LANES], both[:, LANES:]
                po_ref[t, p, rows(start, band), :] = num / den
                pl_ref[t, p, rows(start, band), :] = jnp.where(first_head, ms[0], ms[1]) + jnp.log(den) * LOG2E
            return carry

        lax.fori_loop(0, sb_len // band, block, 0, unroll=ATTN_BLOCKS_PER_BODY // n_t)

    chunk = 256
    for t in range(n_t):
        for c in range(sb_len // chunk):
            sl = slice(chunk * c, chunk * (c + 1))
            lse = [pl_ref[t, p, sl, :] for p in range(len(patterns))]
            top = functools.reduce(jnp.maximum, lse)
            wgt = [jnp.exp2(l - top) for l in lse]
            num = sum(w * po_ref[t, p, sl, :] for p, w in enumerate(wgt))
            o_ref[0, sl, LANES * t:LANES * (t + 1)] = (num / sum(wgt)).astype(o_ref.dtype)

    last = patterns[0][0] // patterns[0][1]
    kl_ref[...] = kc_ref[:, sb_len - last:sb_len, :]
    vl_ref[...] = vc_ref[:, sb_len - last:sb_len, :]


def _attention(qkv, rel_bias, aw):
    _, b_sz, s_len, _ = qkv.shape
    sb_len = max(w for w, _ in DILATED_PATTERNS)
    band = DILATED_PATTERNS[0][0] // DILATED_PATTERNS[0][1]
    assert all(w // d == band and sb_len % w == 0 for w, d in DILATED_PATTERNS)
    assert s_len % sb_len == 0 and band % LANES == 0
    n_pairs = aw // LANES
    n_pat = len(DILATED_PATTERNS)
    n_t = ATTN_PAIRS_PER_STEP
    assert n_pairs % n_t == 0 and ATTN_BLOCKS_PER_BODY % n_t == 0
    steps_j = n_pairs // n_t
    brow = jnp.stack([_band_bias_rows(rel_bias, band, d) for _, d in DILATED_PATTERNS])
    brow = brow.reshape(n_pat, n_pairs, 2, 2 * band).transpose(1, 0, 2, 3).reshape(n_pairs, n_pat * 2, 2 * band)

    def blk(which):
        return pl.BlockSpec((n_t, None, sb_len, LANES), lambda j, b, s: (which * steps_j + j, b, s, 0))

    return pl.pallas_call(
        functools.partial(_attn_kernel, patterns=DILATED_PATTERNS, scale=A_HEAD_DIM ** -0.5),
        grid=(steps_j, b_sz, s_len // sb_len),
        in_specs=[blk(0), blk(1), blk(2),
                  pl.BlockSpec((n_t, n_pat * 2, 2 * band), lambda j, b, s: (j, 0, 0))],
        out_specs=pl.BlockSpec((1, sb_len, n_t * LANES), lambda j, b, s: (b, s, j)),
        out_shape=jax.ShapeDtypeStruct((b_sz, s_len, aw), BF16),
        scratch_shapes=[pltpu.VMEM((n_t, n_pat, sb_len, LANES), F32), pltpu.VMEM((n_t, n_pat, sb_len, LANES), F32),
                        pltpu.VMEM((n_t, n_pat, 2, 2, band, 2 * band), F32),
                        pltpu.VMEM((n_t, ATTN_RESIDUE_PARTS, sb_len // ATTN_RESIDUE_PARTS, LANES), F32),
                        pltpu.VMEM((n_t, 2, ATTN_RESIDUE_PARTS, sb_len // ATTN_RESIDUE_PARTS, LANES), F32),
                        pltpu.VMEM((n_t, 2, ATTN_RESIDUE_PARTS, sb_len // ATTN_RESIDUE_PARTS, LANES), F32),
                        pltpu.VMEM((n_t, band, LANES), F32), pltpu.VMEM((n_t, band, LANES), F32)],
        compiler_params=_cparams(3),
        name="dilated_attn",
    )(qkv, qkv, qkv, brow)


def _mlstm_front_kernel(x_ref, halo_ref, cw_ref, cb_ref, wblk_ref, wif_ref, bif_ref,
                        q_ref, k_ref, v_ref, xc_ref, gate_ref, xe_ref, wbd_ref, *, tiles_per_seq, k_scale, blk):
    tm, width = x_ref.shape
    taps = cw_ref.shape[0]
    pad = halo_ref.shape[0]

    @pl.when(pl.program_id(0) == 0)
    def _():
        r_id = lax.broadcasted_iota(jnp.int32, (MXU_DIM, MXU_DIM), 0)
        c_id = lax.broadcasted_iota(jnp.int32, (MXU_DIM, MXU_DIM), 1)
        same_block = (r_id // blk) == (c_id // blk)
        o_id = lax.broadcasted_iota(jnp.int32, (LANES, MXU_DIM), 0)
        spread = (o_id == lax.broadcasted_iota(jnp.int32, (LANES, MXU_DIM), 1) % blk).astype(BF16)
        for kind in range(wblk_ref.shape[0]):
            for j in range(width // MXU_DIM):
                rows = wblk_ref[kind, MXU_DIM * j:MXU_DIM * (j + 1), :].astype(BF16)
                tile = jnp.dot(rows, spread, preferred_element_type=F32)
                wbd_ref[kind, j] = jnp.where(same_block, tile, 0.0).astype(BF16)

    first = (pl.program_id(0) % tiles_per_seq) == 0
    halo = halo_ref[...].astype(F32)
    xe_ref[0:pad, :] = jnp.where(first, jnp.zeros_like(halo), halo)
    xe_ref[pad:pad + tm, :] = x_ref[...].astype(F32)
    n_tiles = width // MXU_DIM
    for r0 in range(0, tm, FRONT_ROW_CHUNK):
        rs = slice(r0, r0 + FRONT_ROW_CHUNK)
        xmb = x_ref[rs, :]
        y = cb_ref[...]
        for back in range(taps):
            y = y + (xe_ref[pad + r0 - back:pad + r0 - back + FRONT_ROW_CHUNK, :]
                     * cw_ref[taps - 1 - back:taps - back, :])
        xcb = _silu(y).astype(BF16)
        xc_ref[rs, :] = xcb
        qbs, kbs, vbs = [], [], []
        for j in range(n_tiles):
            sl = slice(MXU_DIM * j, MXU_DIM * (j + 1))
            qj = jnp.dot(xcb[:, sl], wbd_ref[0, j], preferred_element_type=F32)
            kj = jnp.dot(xcb[:, sl], wbd_ref[1, j], preferred_element_type=F32)
            vj = jnp.dot(xmb[:, sl], wbd_ref[2, j], preferred_element_type=F32)
            qbs.append(qj.astype(BF16))
            kbs.append(kj.astype(BF16))
            vbs.append(vj.astype(BF16))
            k_ref[rs, sl] = (kj * k_scale).astype(BF16)
        qb, kb, vb = (jnp.concatenate(t, axis=1) for t in (qbs, kbs, vbs))
        q_ref[rs, :] = qb
        v_ref[rs, :] = vb
        gate_ref[rs, :] = (bif_ref[...]
                           + jnp.dot(qb, wif_ref[0:width, :], preferred_element_type=F32)
                           + jnp.dot(kb, wif_ref[width:2 * width, :], preferred_element_type=F32)
                           + jnp.dot(vb, wif_ref[2 * width:3 * width, :], preferred_element_type=F32))


def _mlstm_front(rest2, conv_w, conv_b, w_blocks, wif_pad, bif_pad, s_len, width, tm):
    n = rest2.shape[0]
    halo = 16
    n_kinds, n_blocks, blk, _ = w_blocks.shape
    assert conv_w.shape[0] - 1 <= halo and n_blocks * blk == width and MXU_DIM % blk == 0
    wblk = jnp.pad(w_blocks.reshape(n_kinds, width, blk), ((0, 0), (0, 0), (0, LANES - blk)))
    const = lambda *shape: pl.BlockSpec(shape, lambda i: (0,) * len(shape))
    tok = lambda w: pl.BlockSpec((tm, w), lambda i: (i, 0))
    return pl.pallas_call(
        functools.partial(_mlstm_front_kernel, tiles_per_seq=s_len // tm,
                          k_scale=(width // M_HEADS) ** -0.5, blk=blk),
        grid=(n // tm,),
        in_specs=[tok(width),
                  pl.BlockSpec((halo, width), lambda i: (jnp.maximum(i * (tm // halo) - 1, 0), 0)),
                  const(*conv_w.shape), const(1, width),
                  const(*wblk.shape), const(*wif_pad.shape), const(1, LANES)],
        out_specs=[tok(width), tok(width), tok(width), tok(width), tok(LANES)],
        out_shape=[jax.ShapeDtypeStruct((n, width), BF16)] * 4 + [jax.ShapeDtypeStruct((n, LANES), F32)],
        scratch_shapes=[pltpu.VMEM((tm + halo, width), F32),
                        pltpu.VMEM((n_kinds, width // MXU_DIM, MXU_DIM, MXU_DIM), BF16)],
        compiler_params=_cparams(1),
        name="mlstm_front",
    )(rest2, rest2, conv_w, conv_b, wblk, wif_pad, bif_pad)


def _split3(x):
    hi = x.astype(BF16)
    r1 = x - hi.astype(F32)
    mid = r1.astype(BF16)
    lo = (r1 - mid.astype(F32)).astype(BF16)
    return hi, mid, lo


def _mlstm_core_kernel(q_ref, k_ref, v_ref, grow_ref, om_ref, zm_ref, xc_ref, hg_ref, skip_ref,
                       y_ref, c_ref, n_ref, m_ref):
    @pl.when(pl.program_id(2) == 0)
    def _():
        c_ref[...] = jnp.zeros_like(c_ref)
        n_ref[...] = jnp.zeros_like(n_ref)
        m_ref[...] = jnp.zeros_like(m_ref)

    lc = q_ref.shape[1]
    row_id = lax.broadcasted_iota(jnp.int32, (lc, lc), 0)
    col_id = lax.broadcasted_iota(jnp.int32, (lc, lc), 1)
    causal = row_id >= col_id
    upper = (row_id <= col_id).astype(BF16)
    gate_row = lax.broadcasted_iota(jnp.int32, (GATE_ROWS, lc), 0)
    for s in range(q_ref.shape[0]):
        q, k, v = q_ref[s], k_ref[s], v_ref[s]
        pre = grow_ref[s, 0]
        lf = jnp.minimum(pre, 0.0) - jnp.log1p(jnp.exp(-jnp.abs(pre)))
        csum = sum(jnp.dot(part, upper, preferred_element_type=F32) for part in _split3(lf))
        grow = jnp.where(gate_row == 0, pre, csum)
        gcol = grow.T
        i_row, b_row = grow[0:1, :], grow[1:2, :]
        i_col, b_col = gcol[:, 0:1], gcol[:, 1:2]
        m_prev = m_ref[s]
        g = b_row[:, lc - 1:lc]

        dmat = jnp.where(causal, b_col - b_row + i_row, MASKED)
        inter = b_col + m_prev
        m_t = jnp.maximum(inter, jnp.max(dmat, axis=1, keepdims=True))
        qk = lax.dot_general(q, k, (((1,), (1,)), ((), ())), preferred_element_type=F32) * jnp.exp(dmat - m_t)
        w_inter = jnp.exp(inter - m_t)
        c_old = c_ref[s]
        num = (w_inter * jnp.dot(q, c_old.astype(BF16), preferred_element_type=F32)
               + jnp.dot(qk.astype(BF16), v, preferred_element_type=F32))
        den = (w_inter * jnp.sum(q.astype(F32) * n_ref[s], axis=1, keepdims=True)
               + jnp.sum(qk, axis=1, keepdims=True))
        h = num / jnp.maximum(jnp.abs(den), jnp.exp(-m_t))

        m_new = jnp.maximum(g + m_prev, jnp.max(g - b_row + i_row, axis=1, keepdims=True))
        w_s = jnp.exp(g - b_col + i_col - m_new)
        decay = jnp.exp(g + m_prev - m_new)
        kw = (k.astype(F32) * w_s).astype(BF16)
        c_ref[s] = decay * c_old + lax.dot_general(kw, v, (((0,), (0,)), ((), ())), preferred_element_type=F32)
        n_ref[s] = decay * n_ref[s] + jnp.dot(jnp.ones((8, lc), BF16), kw, preferred_element_type=F32)[0:1]
        m_ref[s] = m_new

        hgated = _sigmoid(om_ref[s].astype(F32)) * h
        mu = jnp.mean(hgated, axis=1, keepdims=True)
        cen = hgated - mu
        var = jnp.mean(cen * cen, axis=1, keepdims=True)
        hn = cen * lax.rsqrt(var + EPS) * hg_ref[...]
        zm = zm_ref[s].astype(F32)
        y_ref[s] = ((hn + skip_ref[...] * xc_ref[s].astype(F32)) * _silu(zm)).astype(y_ref.dtype)


def _mlstm_core(q, k, v, grow, rest3, xc, head_norm_g, skip, lc, om_col0, zm_col0):
    b_sz, s_len, width = q.shape
    dh = width // M_HEADS
    grp = MLSTM_GROUP
    assert b_sz % grp == 0
    seq = lambda col0: pl.BlockSpec((grp, lc, dh), lambda b, h, c: (b, c, col0 + h))
    vec = pl.BlockSpec((1, dh), lambda b, h, c: (0, h))
    return pl.pallas_call(
        _mlstm_core_kernel,
        grid=(b_sz // grp, M_HEADS, s_len // lc),
        in_specs=[seq(0), seq(0), seq(0),
                  pl.BlockSpec((grp, 1, GATE_ROWS, lc), lambda b, h, c: (b, h, 0, c)),
                  seq(om_col0 // dh), seq(zm_col0 // dh), seq(0), vec, vec],
        out_specs=seq(0),
        out_shape=jax.ShapeDtypeStruct((b_sz, s_len, width), BF16),
        scratch_shapes=[pltpu.VMEM((grp, dh, dh), F32), pltpu.VMEM((grp, 1, dh), F32),
                        pltpu.VMEM((grp, 1, 1), F32)],
        compiler_params=_cparams(3),
        name="mlstm_core",
    )(q, k, v, grow, rest3, rest3, xc, head_norm_g, skip)


def _merge_out_kernel(ya_ref, za_ref, gates_ref, gb_ref, ym_ref, x_ref, wpa_ref, wpb_ref, wout_ref, gout_ref,
                      out_ref):
    d_model = x_ref.shape[1]
    za = za_ref[...].astype(F32)
    ya = jnp.dot((ya_ref[...].astype(F32) * _silu(za)).astype(BF16), wpa_ref[...],
                 preferred_element_type=F32)
    ym = jnp.dot(ym_ref[...], wpb_ref[...], preferred_element_type=F32)
    gate = _sigmoid(gates_ref[...].astype(F32) + gb_ref[...])
    merged = gate[:, :d_model] * ya + gate[:, d_model:] * ym
    hres = x_ref[...] + jnp.dot(merged.astype(BF16), wout_ref[...], preferred_element_type=F32)
    ms = jnp.mean(hres * hres, axis=-1, keepdims=True)
    out_ref[...] = hres * lax.rsqrt(ms + EPS) * gout_ref[...]


def _merge_out(ya2, proj2, gate_b, ym2, x2, wpa, wpb, wout, gout, za_col0, gates_col0, tm):
    n, d_model = x2.shape
    aw = ya2.shape[1]
    mw = ym2.shape[1]
    tok = lambda w, cb=0: pl.BlockSpec((tm, w), lambda i: (i, cb))
    const = lambda *shape: pl.BlockSpec(shape, lambda i: (0,) * len(shape))
    return pl.pallas_call(
        _merge_out_kernel,
        grid=(n // tm,),
        in_specs=[tok(aw), tok(aw, za_col0 // aw), tok(2 * d_model, gates_col0 // (2 * d_model)),
                  const(1, 2 * d_model), tok(mw), tok(d_model), const(aw, d_model), const(mw, d_model),
                  const(d_model, d_model), const(1, d_model)],
        out_specs=tok(d_model),
        out_shape=jax.ShapeDtypeStruct((n, d_model), F32),
        compiler_params=_cparams(1),
        name="merge_out",
    )(ya2, proj2, proj2, gate_b, ym2, x2, wpa, wpb, wout, gout)


def _layer(h2, b_sz, s_len, norm_in_g, w_in, gate_b, conv_w, conv_b, wq_m, wk_m, wv_m, w_if, b_if,
           head_norm_g, skip_m, w_pa, w_pb, w_out, rel_bias, gout):
    n, d_model = h2.shape
    aw = w_pa.shape[0]
    mw = w_pb.shape[0]
    assert aw == A_HEADS * A_HEAD_DIM and aw == d_model and mw == 2 * d_model
    wb = w_in.astype(BF16)
    xm_col0, zm_col0, om_col0, gates_col0, za_col0 = 0, mw, 2 * mw, 3 * mw, 3 * mw + 2 * d_model
    n_rest = za_col0 + aw
    tn = T_INPROJ_COLS
    assert aw == tn
    rest_first = 4 * aw // tn
    rest_tiles_before_za = (n_rest - aw) // tn
    n_rest_tiles = n_rest // tn
    za_block = 3 * aw // tn

    def w_col_block(j):
        return jnp.where(j < rest_tiles_before_za, j + rest_first,
                         jnp.where(j < n_rest_tiles, za_block, j - n_rest_tiles))

    rest2, qkv = _norm_inproj(h2, norm_in_g.reshape(1, d_model), wb, w_col_block, n_rest, 3 * aw,
                              tm=T_INPROJ_ROWS, tn=tn)
    to3 = lambda t: t.reshape(b_sz, s_len, t.shape[-1])

    ya = _attention(qkv.reshape(3 * aw // LANES, b_sz, s_len, LANES), rel_bias, aw)

    assert xm_col0 == 0
    wif_pad = jnp.pad(w_if, ((0, 0), (0, LANES - w_if.shape[1]))).astype(BF16)
    bif_pad = jnp.pad(b_if, (0, LANES - b_if.shape[0])).reshape(1, LANES)
    q, k, v, xc, gate_pre = _mlstm_front(
        rest2, conv_w, conv_b.reshape(1, mw), jnp.stack([wq_m, wk_m, wv_m]), wif_pad, bif_pad,
        s_len, mw, tm=T_FRONT_ROWS)
    grow = jnp.transpose(gate_pre[:, :2 * M_HEADS].reshape(b_sz, s_len, 2, M_HEADS), (0, 3, 2, 1))
    grow = jnp.pad(grow, ((0, 0), (0, 0), (0, GATE_ROWS - 2), (0, 0)))
    ym = _mlstm_core(to3(q), to3(k), to3(v), grow, to3(rest2), to3(xc),
                     head_norm_g.reshape(1, mw), skip_m.reshape(1, mw), MLSTM_CHUNK, om_col0, zm_col0)

    return _merge_out(ya.reshape(n, aw), rest2, gate_b.reshape(1, 2 * d_model), ym.reshape(n, mw), h2,
                      w_pa.astype(BF16), w_pb.astype(BF16), w_out.astype(BF16), gout.reshape(1, d_model),
                      za_col0, gates_col0, tm=T_MERGE_ROWS)


def kernel(x, norm_in_g, w_in, gate_b, conv_w, conv_b, wq_m, wk_m, wv_m, w_if, b_if, head_norm_g, skip_m,
           w_pa, w_pb, w_out, rel_bias, norm_out_g):
    b_sz, s_len, d_model = x.shape
    depth = w_in.shape[0]
    assert depth == 1
    out = _layer(x.reshape(b_sz * s_len, d_model), b_sz, s_len, norm_in_g[0], w_in[0], gate_b[0], conv_w[0],
                 conv_b[0], wq_m[0], wk_m[0], wv_m[0], w_if[0], b_if[0], head_norm_g[0], skip_m[0], w_pa[0],
                 w_pb[0], w_out[0], rel_bias, norm_out_g)
    return out.reshape(b_sz, s_len, d_model)
```

```python
import functools
import math

import jax
import jax.numpy as jnp
from jax import lax
from jax.experimental import pallas as pl
from jax.experimental.pallas import tpu as pltpu

F32 = jnp.float32
BF16 = jnp.bfloat16

A_HEADS = 16
A_HEAD_DIM = 64
DILATED_PATTERNS = ((128, 1), (512, 4), (2048, 16))
MAX_DISTANCE = 2048
M_HEADS = 4
EPS = 1e-6
MASKED = -1e30
LOG2E = math.log2(math.e)


def _sigmoid(x):
    return 0.5 * jnp.tanh(0.5 * x) + 0.5


def _silu(x):
    h = 0.5 * x
    return h * jnp.tanh(h) + h


LANES = 128
MXU_DIM = 256
VMEM_LIMIT_BYTES = 56 * 1024 * 1024

MLSTM_CHUNK = 256
MLSTM_GROUP = 8
GATE_ROWS = 8
ATTN_PAIRS_PER_STEP = 2
ATTN_RESIDUE_PARTS = 4
ATTN_BLOCKS_PER_BODY = 16

T_INPROJ_ROWS = 2048
T_INPROJ_COLS = 1024
T_FRONT_ROWS = 512
FRONT_ROW_CHUNK = 256
T_MERGE_ROWS = 512


def _cparams(n_axes):
    return pltpu.CompilerParams(dimension_semantics=("arbitrary",) * n_axes,
                                vmem_limit_bytes=VMEM_LIMIT_BYTES)


def _norm_inproj_kernel(x_ref, g_ref, w_ref, rest_ref, slab_ref, xn_ref, *, n_rest_tiles):
    j = pl.program_id(1)

    @pl.when(j == 0)
    def _():
        xf = x_ref[...]
        ms = jnp.mean(xf * xf, axis=-1, keepdims=True)
        xn_ref[...] = (xf * lax.rsqrt(ms + EPS) * g_ref[...]).astype(BF16)

    @pl.when(j < n_rest_tiles)
    def _():
        rest_ref[...] = jnp.dot(xn_ref[...], w_ref[...], preferred_element_type=F32).astype(rest_ref.dtype)

    @pl.when(j >= n_rest_tiles)
    def _():
        o = jnp.dot(xn_ref[...], w_ref[...], preferred_element_type=F32)
        for c in range(slab_ref.shape[0]):
            slab_ref[c] = o[:, LANES * c:LANES * (c + 1)]


def _norm_inproj(x2, g, w, w_col_block, n_rest, n_slab_cols, tm, tn):
    n, d = x2.shape
    n_rest_tiles = n_rest // tn
    n_slab_tiles = n_slab_cols // tn
    return pl.pallas_call(
        functools.partial(_norm_inproj_kernel, n_rest_tiles=n_rest_tiles),
        grid=(n // tm, n_rest_tiles + n_slab_tiles),
        in_specs=[pl.BlockSpec((tm, d), lambda i, j: (i, 0)),
                  pl.BlockSpec((1, d), lambda i, j: (0, 0)),
                  pl.BlockSpec((d, tn), lambda i, j: (0, w_col_block(j)))],
        out_specs=[pl.BlockSpec((tm, tn), lambda i, j: (i, jnp.minimum(j, n_rest_tiles - 1))),
                   pl.BlockSpec((tn // LANES, tm, LANES), lambda i, j: (jnp.maximum(j - n_rest_tiles, 0), i, 0))],
        out_shape=[jax.ShapeDtypeStruct((n, n_rest), BF16),
                   jax.ShapeDtypeStruct((n_slab_cols // LANES, n, LANES), F32)],
        scratch_shapes=[pltpu.VMEM((tm, d), BF16)],
        compiler_params=_cparams(2),
        name="norm_inproj",
    )(x2, g, w)


def _t5_bucket(dist, n_buckets):
    max_exact = n_buckets // 2
    large = max_exact + (jnp.log(jnp.maximum(dist, max_exact).astype(F32) / max_exact)
                         / math.log(MAX_DISTANCE / max_exact) * (n_buckets - max_exact)).astype(jnp.int32)
    return jnp.where(dist < max_exact, dist, jnp.minimum(large, n_buckets - 1))


def _band_bias_rows(rel_bias, band, dilation):
    delta = jnp.arange(band + 1)
    vals = rel_bias.astype(F32)[_t5_bucket(delta * dilation, rel_bias.shape[0])] * LOG2E
    return jnp.concatenate([vals[::-1].T, jnp.full((rel_bias.shape[1], band - 1), MASKED, F32)], axis=1)


def _attn_kernel(q_ref, kc_ref, vc_ref, brow_ref, o_ref, po_ref, pl_ref, bias_ref,
                 qq_ref, kq_ref, vq_ref, kl_ref, vl_ref, *, patterns, scale):
    n_t, sb_len, _ = q_ref.shape
    first_sb = pl.program_id(2) == 0

    @pl.when(jnp.logical_and(pl.program_id(1) == 0, first_sb))
    def _():
        for t in range(n_t):
            for p, (window, d) in enumerate(patterns):
                band = window // d
                has_prev = lax.broadcasted_iota(jnp.int32, (band, 2 * band), 1) >= band
                for hh in range(2):
                    row = brow_ref[t, 2 * p + hh:2 * p + hh + 1, :]
                    table = pltpu.roll(jnp.broadcast_to(row, (band, 2 * band)), 0, 1, stride=1, stride_axis=0)
                    bias_ref[t, p, 1, hh] = table
                    bias_ref[t, p, 0, hh] = jnp.where(has_prev, table, MASKED)

    nq = ATTN_RESIDUE_PARTS
    sbq = sb_len // nq
    cur = pl.program_id(2) % 2
    prv = 1 - cur

    @pl.when((pl.program_id(0) == 0) & (pl.program_id(1) == 0) & first_sb)
    def _():
        kq_ref[...] = jnp.zeros_like(kq_ref)
        vq_ref[...] = jnp.zeros_like(vq_ref)
        kl_ref[...] = jnp.zeros_like(kl_ref)
        vl_ref[...] = jnp.zeros_like(vl_ref)

    for t in range(n_t):
        for c in range(nq):
            qq_ref[t, c] = q_ref[t, pl.ds(c, sbq, stride=nq), :]
            kq_ref[t, cur, c] = kc_ref[t, pl.ds(c, sbq, stride=nq), :]
            vq_ref[t, cur, c] = vc_ref[t, pl.ds(c, sbq, stride=nq), :]
    heads_per_tile = LANES // A_HEAD_DIM
    assert heads_per_tile == 2

    for p, (window, d) in enumerate(patterns):
        band = window // d
        blocks_per_residue = sb_len // window
        first_head = lax.broadcasted_iota(jnp.int32, (band, LANES), 1) < A_HEAD_DIM
        first_head_keys = lax.broadcasted_iota(jnp.int32, (2 * band, LANES), 1) < A_HEAD_DIM

        def rows(start, n, d=d):
            return pl.ds(start, n, stride=d) if d > 1 else pl.ds(start, n)

        def block(i, carry, p=p, d=d, band=band, blocks_per_residue=blocks_per_residue, first_head=first_head,
                  first_head_keys=first_head_keys, rows=rows, nq=nq, sbq=sbq):
            r = i // blocks_per_residue
            jb = i % blocks_per_residue
            start = r + jb * (band * d)
            variant = jnp.where(jnp.logical_and(first_sb, jb == 0), 0, 1)
            for t in range(n_t):
                if d % nq == 0:
                    e = d // nq
                    part, first_row = r % nq, r // nq + jb * (band * e)
                    rows_e = lambda a, n: pl.ds(a, n, stride=e) if e > 1 else pl.ds(a, n)
                    qf = qq_ref[t, part, rows_e(first_row, band), :]
                    carried = rows_e(sbq - band * e + r // nq, band)
                    k_before, v_before = kq_ref[t, prv, part, carried, :], vq_ref[t, prv, part, carried, :]
                    if blocks_per_residue > 1:
                        inside = rows_e(jnp.maximum(first_row - band * e, 0), band)
                        k_before = jnp.where(jb == 0, k_before, kq_ref[t, cur, part, inside, :])
                        v_before = jnp.where(jb == 0, v_before, vq_ref[t, cur, part, inside, :])
                    kf = jnp.concatenate([k_before, kq_ref[t, cur, part, rows_e(first_row, band), :]], axis=0)
                    vf = jnp.concatenate([v_before, vq_ref[t, cur, part, rows_e(first_row, band), :]], axis=0)
                else:
                    assert d == 1
                    qf = q_ref[t, pl.ds(start, band), :]
                    inside = pl.ds(jnp.maximum(start - band, 0), band)
                    kf = jnp.concatenate([jnp.where(jb == 0, kl_ref[t], kc_ref[t, inside, :]),
                                          kc_ref[t, pl.ds(start, band), :]], axis=0)
                    vf = jnp.concatenate([jnp.where(jb == 0, vl_ref[t], vc_ref[t, inside, :]),
                                          vc_ref[t, pl.ds(start, band), :]], axis=0)
                q2 = (qf * (scale * LOG2E)).astype(BF16)
                k2 = kf.astype(BF16)
                v2 = vf.astype(BF16)
                zeros = jnp.zeros_like(v2)
                es, ms = [], []
                for hh in range(heads_per_tile):
                    mine = first_head if hh == 0 else jnp.logical_not(first_head)
                    qm = jnp.where(mine, q2, jnp.zeros_like(q2))
                    s = lax.dot_general(qm, k2, (((1,), (1,)), ((), ())), preferred_element_type=F32)
                    s = s + bias_ref[t, p, variant, hh]
                    m = jnp.max(s, axis=1, keepdims=True)
                    es.append(jnp.exp2(s - m).astype(BF16))
                    ms.append(m)
                own0 = first_head_keys.astype(BF16)
                rhs = jnp.concatenate([jnp.concatenate([jnp.where(first_head_keys, v2, zeros), own0], axis=1),
                                       jnp.concatenate([jnp.where(first_head_keys, zeros, v2), 1 - own0], axis=1)],
                                      axis=0)
                both = jnp.dot(jnp.concatenate(es, axis=1), rhs, preferred_element_type=F32)
                num, den = both[:, :LANES], both[:, LANES:]
                po_ref[t, p, rows(start, band), :] = num / den
                pl_ref[t, p, rows(start, band), :] = jnp.where(first_head, ms[0], ms[1]) + jnp.log(den) * LOG2E
            return carry

        lax.fori_loop(0, sb_len // band, block, 0, unroll=ATTN_BLOCKS_PER_BODY // n_t)

    chunk = 256
    for t in range(n_t):
        for c in range(sb_len // chunk):
            sl = slice(chunk * c, chunk * (c + 1))
            lse = [pl_ref[t, p, sl, :] for p in range(len(patterns))]
            top = functools.reduce(jnp.maximum, lse)
            wgt = [jnp.exp2(l - top) for l in lse]
            num = sum(w * po_ref[t, p, sl, :] for p, w in enumerate(wgt))
            o_ref[0, sl, LANES * t:LANES * (t + 1)] = (num / sum(wgt)).astype(o_ref.dtype)

    last = patterns[0][0] // patterns[0][1]
    kl_ref[...] = kc_ref[:, sb_len - last:sb_len, :]
    vl_ref[...] = vc_ref[:, sb_len - last:sb_len, :]


def _attention(qkv, rel_bias, aw):
    _, b_sz, s_len, _ = qkv.shape
    sb_len = max(w for w, _ in DILATED_PATTERNS)
    band = DILATED_PATTERNS[0][0] // DILATED_PATTERNS[0][1]
    assert all(w // d == band and sb_len % w == 0 for w, d in DILATED_PATTERNS)
    assert s_len % sb_len == 0 and band % LANES == 0
    n_pairs = aw // LANES
    n_pat = len(DILATED_PATTERNS)
    n_t = ATTN_PAIRS_PER_STEP
    assert n_pairs % n_t == 0 and ATTN_BLOCKS_PER_BODY % n_t == 0
    steps_j = n_pairs // n_t
    brow = jnp.stack([_band_bias_rows(rel_bias, band, d) for _, d in DILATED_PATTERNS])
    brow = brow.reshape(n_pat, n_pairs, 2, 2 * band).transpose(1, 0, 2, 3).reshape(n_pairs, n_pat * 2, 2 * band)

    def blk(which):
        return pl.BlockSpec((n_t, None, sb_len, LANES), lambda j, b, s: (which * steps_j + j, b, s, 0))

    return pl.pallas_call(
        functools.partial(_attn_kernel, patterns=DILATED_PATTERNS, scale=A_HEAD_DIM ** -0.5),
        grid=(steps_j, b_sz, s_len // sb_len),
        in_specs=[blk(0), blk(1), blk(2),
                  pl.BlockSpec((n_t, n_pat * 2, 2 * band), lambda j, b, s: (j, 0, 0))],
        out_specs=pl.BlockSpec((1, sb_len, n_t * LANES), lambda j, b, s: (b, s, j)),
        out_shape=jax.ShapeDtypeStruct((b_sz, s_len, aw), BF16),
        scratch_shapes=[pltpu.VMEM((n_t, n_pat, sb_len, LANES), F32), pltpu.VMEM((n_t, n_pat, sb_len, LANES), F32),
                        pltpu.VMEM((n_t, n_pat, 2, 2, band, 2 * band), F32),
                        pltpu.VMEM((n_t, ATTN_RESIDUE_PARTS, sb_len // ATTN_RESIDUE_PARTS, LANES), F32),
                        pltpu.VMEM((n_t, 2, ATTN_RESIDUE_PARTS, sb_len // ATTN_RESIDUE_PARTS, LANES), F32),
                        pltpu.VMEM((n_t, 2, ATTN_RESIDUE_PARTS, sb_len // ATTN_RESIDUE_PARTS, LANES), F32),
                        pltpu.VMEM((n_t, band, LANES), F32), pltpu.VMEM((n_t, band, LANES), F32)],
        compiler_params=_cparams(3),
        name="dilated_attn",
    )(qkv, qkv, qkv, brow)


def _mlstm_front_kernel(x_ref, halo_ref, cw_ref, cb_ref, wblk_ref, wif_ref, bif_ref,
                        q_ref, k_ref, v_ref, xc_ref, gate_ref, xe_ref, wbd_ref, *, tiles_per_seq, k_scale, blk):
    tm, width = x_ref.shape
    taps = cw_ref.shape[0]
    pad = halo_ref.shape[0]

    @pl.when(pl.program_id(0) == 0)
    def _():
        r_id = lax.broadcasted_iota(jnp.int32, (MXU_DIM, MXU_DIM), 0)
        c_id = lax.broadcasted_iota(jnp.int32, (MXU_DIM, MXU_DIM), 1)
        same_block = (r_id // blk) == (c_id // blk)
        o_id = lax.broadcasted_iota(jnp.int32, (LANES, MXU_DIM), 0)
        spread = (o_id == lax.broadcasted_iota(jnp.int32, (LANES, MXU_DIM), 1) % blk).astype(BF16)
        for kind in range(wblk_ref.shape[0]):
            for j in range(width // MXU_DIM):
                rows = wblk_ref[kind, MXU_DIM * j:MXU_DIM * (j + 1), :].astype(BF16)
                tile = jnp.dot(rows, spread, preferred_element_type=F32)
                wbd_ref[kind, j] = jnp.where(same_block, tile, 0.0).astype(BF16)

    first = (pl.program_id(0) % tiles_per_seq) == 0
    halo = halo_ref[...].astype(F32)
    xe_ref[0:pad, :] = jnp.where(first, jnp.zeros_like(halo), halo)
    xe_ref[pad:pad + tm, :] = x_ref[...].astype(F32)
    n_tiles = width // MXU_DIM
    for r0 in range(0, tm, FRONT_ROW_CHUNK):
        rs = slice(r0, r0 + FRONT_ROW_CHUNK)
        xmb = x_ref[rs, :]
        y = cb_ref[...]
        for back in range(taps):
            y = y + (xe_ref[pad + r0 - back:pad + r0 - back + FRONT_ROW_CHUNK, :]
                     * cw_ref[taps - 1 - back:taps - back, :])
        xcb = _silu(y).astype(BF16)
        xc_ref[rs, :] = xcb
        qbs, kbs, vbs = [], [], []
        for j in range(n_tiles):
            sl = slice(MXU_DIM * j, MXU_DIM * (j + 1))
            qj = jnp.dot(xcb[:, sl], wbd_ref[0, j], preferred_element_type=F32)
            kj = jnp.dot(xcb[:, sl], wbd_ref[1, j], preferred_element_type=F32)
            vj = jnp.dot(xmb[:, sl], wbd_ref[2, j], preferred_element_type=F32)
            qbs.append(qj.astype(BF16))
            kbs.append(kj.astype(BF16))
            vbs.append(vj.astype(BF16))
            k_ref[rs, sl] = (kj * k_scale).astype(BF16)
        qb, kb, vb = (jnp.concatenate(t, axis=1) for t in (qbs, kbs, vbs))
        q_ref[rs, :] = qb
        v_ref[rs, :] = vb
        gate_ref[rs, :] = (bif_ref[...]
                           + jnp.dot(qb, wif_ref[0:width, :], preferred_element_type=F32)
                           + jnp.dot(kb, wif_ref[width:2 * width, :], preferred_element_type=F32)
                           + jnp.dot(vb, wif_ref[2 * width:3 * width, :], preferred_element_type=F32))


def _mlstm_front(rest2, conv_w, conv_b, w_blocks, wif_pad, bif_pad, s_len, width, tm):
    n = rest2.shape[0]
    halo = 16
    n_kinds, n_blocks, blk, _ = w_blocks.shape
    assert conv_w.shape[0] - 1 <= halo and n_blocks * blk == width and MXU_DIM % blk == 0
    wblk = jnp.pad(w_blocks.reshape(n_kinds, width, blk), ((0, 0), (0, 0), (0, LANES - blk)))
    const = lambda *shape: pl.BlockSpec(shape, lambda i: (0,) * len(shape))
    tok = lambda w: pl.BlockSpec((tm, w), lambda i: (i, 0))
    return pl.pallas_call(
        functools.partial(_mlstm_front_kernel, tiles_per_seq=s_len // tm,
                          k_scale=(width // M_HEADS) ** -0.5, blk=blk),
        grid=(n // tm,),
        in_specs=[tok(width),
                  pl.BlockSpec((halo, width), lambda i: (jnp.maximum(i * (tm // halo) - 1, 0), 0)),
                  const(*conv_w.shape), const(1, width),
                  const(*wblk.shape), const(*wif_pad.shape), const(1, LANES)],
        out_specs=[tok(width), tok(width), tok(width), tok(width), tok(LANES)],
        out_shape=[jax.ShapeDtypeStruct((n, width), BF16)] * 4 + [jax.ShapeDtypeStruct((n, LANES), F32)],
        scratch_shapes=[pltpu.VMEM((tm + halo, width), F32),
                        pltpu.VMEM((n_kinds, width // MXU_DIM, MXU_DIM, MXU_DIM), BF16)],
        compiler_params=_cparams(1),
        name="mlstm_front",
    )(rest2, rest2, conv_w, conv_b, wblk, wif_pad, bif_pad)


def _split3(x):
    hi = x.astype(BF16)
    r1 = x - hi.astype(F32)
    mid = r1.astype(BF16)
    lo = (r1 - mid.astype(F32)).astype(BF16)
    return hi, mid, lo


def _mlstm_core_kernel(q_ref, k_ref, v_ref, grow_ref, om_ref, zm_ref, xc_ref, hg_ref, skip_ref,
                       y_ref, c_ref, n_ref, m_ref):
    @pl.when(pl.program_id(2) == 0)
    def _():
        c_ref[...] = jnp.zeros_like(c_ref)
        n_ref[...] = jnp.zeros_like(n_ref)
        m_ref[...] = jnp.zeros_like(m_ref)

    lc = q_ref.shape[1]
    row_id = lax.broadcasted_iota(jnp.int32, (lc, lc), 0)
    col_id = lax.broadcasted_iota(jnp.int32, (lc, lc), 1)
    causal = row_id >= col_id
    upper = (row_id <= col_id).astype(BF16)
    gate_row = lax.broadcasted_iota(jnp.int32, (GATE_ROWS, lc), 0)
    for s in range(q_ref.shape[0]):
        q, k, v = q_ref[s], k_ref[s], v_ref[s]
        pre = grow_ref[s, 0]
        lf = jnp.minimum(pre, 0.0) - jnp.log1p(jnp.exp(-jnp.abs(pre)))
        csum = sum(jnp.dot(part, upper, preferred_element_type=F32) for part in _split3(lf))
        grow = jnp.where(gate_row == 0, pre, csum)
        gcol = grow.T
        i_row, b_row = grow[0:1, :], grow[1:2, :]
        i_col, b_col = gcol[:, 0:1], gcol[:, 1:2]
        m_prev = m_ref[s]
        g = b_row[:, lc - 1:lc]

        dmat = jnp.where(causal, b_col - b_row + i_row, MASKED)
        inter = b_col + m_prev
        m_t = jnp.maximum(inter, jnp.max(dmat, axis=1, keepdims=True))
        qk = lax.dot_general(q, k, (((1,), (1,)), ((), ())), preferred_element_type=F32) * jnp.exp(dmat - m_t)
        w_inter = jnp.exp(inter - m_t)
        c_old = c_ref[s]
        num = (w_inter * jnp.dot(q, c_old.astype(BF16), preferred_element_type=F32)
               + jnp.dot(qk.astype(BF16), v, preferred_element_type=F32))
        den = (w_inter * jnp.sum(q.astype(F32) * n_ref[s], axis=1, keepdims=True)
               + jnp.sum(qk, axis=1, keepdims=True))
        h = num / jnp.maximum(jnp.abs(den), jnp.exp(-m_t))

        m_new = jnp.maximum(g + m_prev, jnp.max(g - b_row + i_row, axis=1, keepdims=True))
        w_s = jnp.exp(g - b_col + i_col - m_new)
        decay = jnp.exp(g + m_prev - m_new)
        kw = (k.astype(F32) * w_s).astype(BF16)
        c_ref[s] = decay * c_old + lax.dot_general(kw, v, (((0,), (0,)), ((), ())), preferred_element_type=F32)
        n_ref[s] = decay * n_ref[s] + jnp.dot(jnp.ones((GATE_ROWS, lc), BF16), kw,
                                              preferred_element_type=F32)[0:1]
        m_ref[s] = m_new

        hgated = _sigmoid(om_ref[s].astype(F32)) * h
        mu = jnp.mean(hgated, axis=1, keepdims=True)
        cen = hgated - mu
        var = jnp.mean(cen * cen, axis=1, keepdims=True)
        hn = cen * lax.rsqrt(var + EPS) * hg_ref[...]
        zm = zm_ref[s].astype(F32)
        y_ref[s] = ((hn + skip_ref[...] * xc_ref[s].astype(F32)) * _silu(zm)).astype(y_ref.dtype)


def _mlstm_core(q, k, v, grow, rest3, xc, head_norm_g, skip, lc, om_col0, zm_col0):
    b_sz, s_len, width = q.shape
    dh = width // M_HEADS
    grp = MLSTM_GROUP
    assert b_sz % grp == 0
    seq = lambda col0: pl.BlockSpec((grp, lc, dh), lambda b, h, c: (b, c, col0 + h))
    vec = pl.BlockSpec((1, dh), lambda b, h, c: (0, h))
    return pl.pallas_call(
        _mlstm_core_kernel,
        grid=(b_sz // grp, M_HEADS, s_len // lc),
        in_specs=[seq(0), seq(0), seq(0),
                  pl.BlockSpec((grp, 1, GATE_ROWS, lc), lambda b, h, c: (b, h, 0, c)),
                  seq(om_col0 // dh), seq(zm_col0 // dh), seq(0), vec, vec],
        out_specs=seq(0),
        out_shape=jax.ShapeDtypeStruct((b_sz, s_len, width), BF16),
        scratch_shapes=[pltpu.VMEM((grp, dh, dh), F32), pltpu.VMEM((grp, 1, dh), F32),
                        pltpu.VMEM((grp, 1, 1), F32)],
        compiler_params=_cparams(3),
        name="mlstm_core",
    )(q, k, v, grow, rest3, rest3, xc, head_norm_g, skip)


def _merge_out_kernel(ya_ref, za_ref, gates_ref, gb_ref, ym_ref, x_ref, wpa_ref, wpb_ref, wout_ref, gout_ref,
                      out_ref):
    d_model = x_ref.shape[1]
    za = za_ref[...].astype(F32)
    ya = jnp.dot((ya_ref[...].astype(F32) * _silu(za)).astype(BF16), wpa_ref[...],
                 preferred_element_type=F32)
    ym = jnp.dot(ym_ref[...], wpb_ref[...], preferred_element_type=F32)
    gate = _sigmoid(gates_ref[...].astype(F32) + gb_ref[...])
    merged = gate[:, :d_model] * ya + gate[:, d_model:] * ym
    hres = x_ref[...] + jnp.dot(merged.astype(BF16), wout_ref[...], preferred_element_type=F32)
    ms = jnp.mean(hres * hres, axis=-1, keepdims=True)
    out_ref[...] = hres * lax.rsqrt(ms + EPS) * gout_ref[...]


def _merge_out(ya2, proj2, gate_b, ym2, x2, wpa, wpb, wout, gout, za_col0, gates_col0, tm):
    n, d_model = x2.shape
    aw = ya2.shape[1]
    mw = ym2.shape[1]
    tok = lambda w, cb=0: pl.BlockSpec((tm, w), lambda i: (i, cb))
    const = lambda *shape: pl.BlockSpec(shape, lambda i: (0,) * len(shape))
    return pl.pallas_call(
        _merge_out_kernel,
        grid=(n // tm,),
        in_specs=[tok(aw), tok(aw, za_col0 // aw), tok(2 * d_model, gates_col0 // (2 * d_model)),
                  const(1, 2 * d_model), tok(mw), tok(d_model), const(aw, d_model), const(mw, d_model),
                  const(d_model, d_model), const(1, d_model)],
        out_specs=tok(d_model),
        out_shape=jax.ShapeDtypeStruct((n, d_model), F32),
        compiler_params=_cparams(1),
        name="merge_out",
    )(ya2, proj2, proj2, gate_b, ym2, x2, wpa, wpb, wout, gout)


def _layer(h2, b_sz, s_len, norm_in_g, w_in, gate_b, conv_w, conv_b, wq_m, wk_m, wv_m, w_if, b_if,
           head_norm_g, skip_m, w_pa, w_pb, w_out, rel_bias, gout):
    n, d_model = h2.shape
    aw = w_pa.shape[0]
    mw = w_pb.shape[0]
    assert aw == A_HEADS * A_HEAD_DIM and aw == d_model and mw == 2 * d_model
    wb = w_in.astype(BF16)
    xm_col0, zm_col0, om_col0, gates_col0, za_col0 = 0, mw, 2 * mw, 3 * mw, 3 * mw + 2 * d_model
    n_rest = za_col0 + aw
    tn = T_INPROJ_COLS
    assert aw == tn
    rest_first = 4 * aw // tn
    rest_tiles_before_za = (n_rest - aw) // tn
    n_rest_tiles = n_rest // tn
    za_block = 3 * aw // tn

    def w_col_block(j):
        return jnp.where(j < rest_tiles_before_za, j + rest_first,
                         jnp.where(j < n_rest_tiles, za_block, j - n_rest_tiles))

    rest2, qkv = _norm_inproj(h2, norm_in_g.reshape(1, d_model), wb, w_col_block, n_rest, 3 * aw,
                              tm=T_INPROJ_ROWS, tn=tn)
    to3 = lambda t: t.reshape(b_sz, s_len, t.shape[-1])

    ya = _attention(qkv.reshape(3 * aw // LANES, b_sz, s_len, LANES), rel_bias, aw)

    assert xm_col0 == 0
    wif_pad = jnp.pad(w_if, ((0, 0), (0, LANES - w_if.shape[1]))).astype(BF16)
    bif_pad = jnp.pad(b_if, (0, LANES - b_if.shape[0])).reshape(1, LANES)
    q, k, v, xc, gate_pre = _mlstm_front(
        rest2, conv_w, conv_b.reshape(1, mw), jnp.stack([wq_m, wk_m, wv_m]), wif_pad, bif_pad,
        s_len, mw, tm=T_FRONT_ROWS)
    grow = jnp.transpose(gate_pre[:, :2 * M_HEADS].reshape(b_sz, s_len, 2, M_HEADS), (0, 3, 2, 1))
    grow = jnp.pad(grow, ((0, 0), (0, 0), (0, GATE_ROWS - 2), (0, 0)))
    ym = _mlstm_core(to3(q), to3(k), to3(v), grow, to3(rest2), to3(xc),
                     head_norm_g.reshape(1, mw), skip_m.reshape(1, mw), MLSTM_CHUNK, om_col0, zm_col0)

    return _merge_out(ya.reshape(n, aw), rest2, gate_b.reshape(1, 2 * d_model), ym.reshape(n, mw), h2,
                      w_pa.astype(BF16), w_pb.astype(BF16), w_out.astype(BF16), gout.reshape(1, d_model),
                      za_col0, gates_col0, tm=T_MERGE_ROWS)


def kernel(x, norm_in_g, w_in, gate_b, conv_w, conv_b, wq_m, wk_m, wv_m, w_if, b_if, head_norm_g, skip_m,
           w_pa, w_pb, w_out, rel_bias, norm_out_g):
    b_sz, s_len, d_model = x.shape
    depth = w_in.shape[0]
    assert depth == 1
    out = _layer(x.reshape(b_sz * s_len, d_model), b_sz, s_len, norm_in_g[0], w_in[0], gate_b[0], conv_w[0],
                 conv_b[0], wq_m[0], wk_m[0], wv_m[0], w_if[0], b_if[0], head_norm_g[0], skip_m[0], w_pa[0],
                 w_pb[0], w_out[0], rel_bias, norm_out_g)
    return out.reshape(b_sz, s_len, d_model)
```

```python
import functools
import math

import jax
import jax.numpy as jnp
from jax import lax
from jax.experimental import pallas as pl
from jax.experimental.pallas import tpu as pltpu

F32 = jnp.float32
BF16 = jnp.bfloat16

A_HEADS = 16
A_HEAD_DIM = 64
DILATED_PATTERNS = ((128, 1), (512, 4), (2048, 16))
MAX_DISTANCE = 2048
M_HEADS = 4
EPS = 1e-6
MASKED = -1e30
LOG2E = math.log2(math.e)


def _sigmoid(x):
    return 0.5 * jnp.tanh(0.5 * x) + 0.5


def _silu(x):
    h = 0.5 * x
    return h * jnp.tanh(h) + h


LANES = 128
MXU_DIM = 256
VMEM_LIMIT_BYTES = 56 * 1024 * 1024

MLSTM_CHUNK = 256
MLSTM_GROUP = 8
GATE_ROWS = 8
ATTN_PAIRS_PER_STEP = 2
ATTN_RESIDUE_PARTS = 4
ATTN_BLOCKS_PER_BODY = 32

T_INPROJ_ROWS = 2048
T_INPROJ_COLS = 1024
T_FRONT_ROWS = 512
FRONT_ROW_CHUNK = 256
T_MERGE_ROWS = 512


def _cparams(n_axes):
    return pltpu.CompilerParams(dimension_semantics=("arbitrary",) * n_axes,
                                vmem_limit_bytes=VMEM_LIMIT_BYTES)


def _norm_inproj_kernel(x_ref, g_ref, w_ref, rest_ref, slab_ref, xn_ref, *, n_rest_tiles):
    j = pl.program_id(1)

    @pl.when(j == 0)
    def _():
        xf = x_ref[...]
        ms = jnp.mean(xf * xf, axis=-1, keepdims=True)
        xn_ref[...] = (xf * lax.rsqrt(ms + EPS) * g_ref[...]).astype(BF16)

    @pl.when(j < n_rest_tiles)
    def _():
        rest_ref[...] = jnp.dot(xn_ref[...], w_ref[...], preferred_element_type=F32).astype(rest_ref.dtype)

    @pl.when(j >= n_rest_tiles)
    def _():
        o = jnp.dot(xn_ref[...], w_ref[...], preferred_element_type=F32)
        for c in range(slab_ref.shape[0]):
            slab_ref[c] = o[:, LANES * c:LANES * (c + 1)]


def _norm_inproj(x2, g, w, w_col_block, n_rest, n_slab_cols, tm, tn):
    n, d = x2.shape
    n_rest_tiles = n_rest // tn
    n_slab_tiles = n_slab_cols // tn
    return pl.pallas_call(
        functools.partial(_norm_inproj_kernel, n_rest_tiles=n_rest_tiles),
        grid=(n // tm, n_rest_tiles + n_slab_tiles),
        in_specs=[pl.BlockSpec((tm, d), lambda i, j: (i, 0)),
                  pl.BlockSpec((1, d), lambda i, j: (0, 0)),
                  pl.BlockSpec((d, tn), lambda i, j: (0, w_col_block(j)))],
        out_specs=[pl.BlockSpec((tm, tn), lambda i, j: (i, jnp.minimum(j, n_rest_tiles - 1))),
                   pl.BlockSpec((tn // LANES, tm, LANES), lambda i, j: (jnp.maximum(j - n_rest_tiles, 0), i, 0))],
        out_shape=[jax.ShapeDtypeStruct((n, n_rest), BF16),
                   jax.ShapeDtypeStruct((n_slab_cols // LANES, n, LANES), F32)],
        scratch_shapes=[pltpu.VMEM((tm, d), BF16)],
        compiler_params=_cparams(2),
        name="norm_inproj",
    )(x2, g, w)


def _t5_bucket(dist, n_buckets):
    max_exact = n_buckets // 2
    large = max_exact + (jnp.log(jnp.maximum(dist, max_exact).astype(F32) / max_exact)
                         / math.log(MAX_DISTANCE / max_exact) * (n_buckets - max_exact)).astype(jnp.int32)
    return jnp.where(dist < max_exact, dist, jnp.minimum(large, n_buckets - 1))


def _band_bias_rows(rel_bias, band, dilation):
    delta = jnp.arange(band + 1)
    vals = rel_bias.astype(F32)[_t5_bucket(delta * dilation, rel_bias.shape[0])] * LOG2E
    return jnp.concatenate([vals[::-1].T, jnp.full((rel_bias.shape[1], band - 1), MASKED, F32)], axis=1)


def _attn_kernel(q_ref, kc_ref, vc_ref, brow_ref, o_ref, po_ref, pl_ref, bias_ref,
                 qq_ref, kq_ref, vq_ref, kl_ref, vl_ref, *, patterns, scale):
    n_t, sb_len, _ = q_ref.shape
    first_sb = pl.program_id(2) == 0

    @pl.when(jnp.logical_and(pl.program_id(1) == 0, first_sb))
    def _():
        for t in range(n_t):
            for p, (window, d) in enumerate(patterns):
                band = window // d
                has_prev = lax.broadcasted_iota(jnp.int32, (band, 2 * band), 1) >= band
                for hh in range(2):
                    row = brow_ref[t, 2 * p + hh:2 * p + hh + 1, :]
                    table = pltpu.roll(jnp.broadcast_to(row, (band, 2 * band)), 0, 1, stride=1, stride_axis=0)
                    bias_ref[t, p, 1, hh] = table
                    bias_ref[t, p, 0, hh] = jnp.where(has_prev, table, MASKED)

    nq = ATTN_RESIDUE_PARTS
    sbq = sb_len // nq
    cur = pl.program_id(2) % 2
    prv = 1 - cur

    @pl.when((pl.program_id(0) == 0) & (pl.program_id(1) == 0) & first_sb)
    def _():
        kq_ref[...] = jnp.zeros_like(kq_ref)
        vq_ref[...] = jnp.zeros_like(vq_ref)
        kl_ref[...] = jnp.zeros_like(kl_ref)
        vl_ref[...] = jnp.zeros_like(vl_ref)

    for t in range(n_t):
        for c in range(nq):
            qq_ref[t, c] = q_ref[t, pl.ds(c, sbq, stride=nq), :]
            kq_ref[t, cur, c] = kc_ref[t, pl.ds(c, sbq, stride=nq), :]
            vq_ref[t, cur, c] = vc_ref[t, pl.ds(c, sbq, stride=nq), :]
    heads_per_tile = LANES // A_HEAD_DIM
    assert heads_per_tile == 2

    for p, (window, d) in enumerate(patterns):
        band = window // d
        blocks_per_residue = sb_len // window
        first_head = lax.broadcasted_iota(jnp.int32, (band, LANES), 1) < A_HEAD_DIM
        first_head_keys = lax.broadcasted_iota(jnp.int32, (2 * band, LANES), 1) < A_HEAD_DIM

        def rows(start, n, d=d):
            return pl.ds(start, n, stride=d) if d > 1 else pl.ds(start, n)

        def block(i, carry, p=p, d=d, band=band, blocks_per_residue=blocks_per_residue, first_head=first_head,
                  first_head_keys=first_head_keys, rows=rows, nq=nq, sbq=sbq):
            r = i // blocks_per_residue
            jb = i % blocks_per_residue
            start = r + jb * (band * d)
            variant = jnp.where(jnp.logical_and(first_sb, jb == 0), 0, 1)
            for t in range(n_t):
                if d % nq == 0:
                    e = d // nq
                    part, first_row = r % nq, r // nq + jb * (band * e)
                    rows_e = lambda a, n: pl.ds(a, n, stride=e) if e > 1 else pl.ds(a, n)
                    qf = qq_ref[t, part, rows_e(first_row, band), :]
                    carried = rows_e(sbq - band * e + r // nq, band)
                    k_before, v_before = kq_ref[t, prv, part, carried, :], vq_ref[t, prv, part, carried, :]
                    if blocks_per_residue > 1:
                        inside = rows_e(jnp.maximum(first_row - band * e, 0), band)
                        k_before = jnp.where(jb == 0, k_before, kq_ref[t, cur, part, inside, :])
                        v_before = jnp.where(jb == 0, v_before, vq_ref[t, cur, part, inside, :])
                    kf = jnp.concatenate([k_before, kq_ref[t, cur, part, rows_e(first_row, band), :]], axis=0)
                    vf = jnp.concatenate([v_before, vq_ref[t, cur, part, rows_e(first_row, band), :]], axis=0)
                else:
                    assert d == 1
                    qf = q_ref[t, pl.ds(start, band), :]
                    inside = pl.ds(jnp.maximum(start - band, 0), band)
                    kf = jnp.concatenate([jnp.where(jb == 0, kl_ref[t], kc_ref[t, inside, :]),
                                          kc_ref[t, pl.ds(start, band), :]], axis=0)
                    vf = jnp.concatenate([jnp.where(jb == 0, vl_ref[t], vc_ref[t, inside, :]),
                                          vc_ref[t, pl.ds(start, band), :]], axis=0)
                q2 = (qf * (scale * LOG2E)).astype(BF16)
                k2 = kf.astype(BF16)
                v2 = vf.astype(BF16)
                zeros = jnp.zeros_like(v2)
                es, ms = [], []
                for hh in range(heads_per_tile):
                    mine = first_head if hh == 0 else jnp.logical_not(first_head)
                    qm = jnp.where(mine, q2, jnp.zeros_like(q2))
                    s = lax.dot_general(qm, k2, (((1,), (1,)), ((), ())), preferred_element_type=F32)
                    s = s + bias_ref[t, p, variant, hh]
                    m = jnp.max(s, axis=1, keepdims=True)
                    es.append(jnp.exp2(s - m).astype(BF16))
                    ms.append(m)
                own0 = first_head_keys.astype(BF16)
                rhs = jnp.concatenate([jnp.concatenate([jnp.where(first_head_keys, v2, zeros), own0], axis=1),
                                       jnp.concatenate([jnp.where(first_head_keys, zeros, v2), 1 - own0], axis=1)],
                                      axis=0)
                both = jnp.dot(jnp.concatenate(es, axis=1), rhs, preferred_element_type=F32)
                num, den = both[:, :LANES], both[:, LANES:]
                po_ref[t, p, rows(start, band), :] = num / den
                pl_ref[t, p, rows(start, band), :] = jnp.where(first_head, ms[0], ms[1]) + jnp.log(den) * LOG2E
            return carry

        lax.fori_loop(0, sb_len // band, block, 0, unroll=ATTN_BLOCKS_PER_BODY // n_t)

    chunk = 256
    for t in range(n_t):
        for c in range(sb_len // chunk):
            sl = slice(chunk * c, chunk * (c + 1))
            lse = [pl_ref[t, p, sl, :] for p in range(len(patterns))]
            top = functools.reduce(jnp.maximum, lse)
            wgt = [jnp.exp2(l - top) for l in lse]
            num = sum(w * po_ref[t, p, sl, :] for p, w in enumerate(wgt))
            o_ref[0, sl, LANES * t:LANES * (t + 1)] = (num / sum(wgt)).astype(o_ref.dtype)

    last = patterns[0][0] // patterns[0][1]
    kl_ref[...] = kc_ref[:, sb_len - last:sb_len, :]
    vl_ref[...] = vc_ref[:, sb_len - last:sb_len, :]


def _attention(qkv, rel_bias, aw):
    _, b_sz, s_len, _ = qkv.shape
    sb_len = max(w for w, _ in DILATED_PATTERNS)
    band = DILATED_PATTERNS[0][0] // DILATED_PATTERNS[0][1]
    assert all(w // d == band and sb_len % w == 0 for w, d in DILATED_PATTERNS)
    assert s_len % sb_len == 0 and band % LANES == 0
    n_pairs = aw // LANES
    n_pat = len(DILATED_PATTERNS)
    n_t = ATTN_PAIRS_PER_STEP
    assert n_pairs % n_t == 0 and ATTN_BLOCKS_PER_BODY % n_t == 0
    steps_j = n_pairs // n_t
    brow = jnp.stack([_band_bias_rows(rel_bias, band, d) for _, d in DILATED_PATTERNS])
    brow = brow.reshape(n_pat, n_pairs, 2, 2 * band).transpose(1, 0, 2, 3).reshape(n_pairs, n_pat * 2, 2 * band)

    def blk(which):
        return pl.BlockSpec((n_t, None, sb_len, LANES), lambda j, b, s: (which * steps_j + j, b, s, 0))

    return pl.pallas_call(
        functools.partial(_attn_kernel, patterns=DILATED_PATTERNS, scale=A_HEAD_DIM ** -0.5),
        grid=(steps_j, b_sz, s_len // sb_len),
        in_specs=[blk(0), blk(1), blk(2),
                  pl.BlockSpec((n_t, n_pat * 2, 2 * band), lambda j, b, s: (j, 0, 0))],
        out_specs=pl.BlockSpec((1, sb_len, n_t * LANES), lambda j, b, s: (b, s, j)),
        out_shape=jax.ShapeDtypeStruct((b_sz, s_len, aw), BF16),
        scratch_shapes=[pltpu.VMEM((n_t, n_pat, sb_len, LANES), F32), pltpu.VMEM((n_t, n_pat, sb_len, LANES), F32),
                        pltpu.VMEM((n_t, n_pat, 2, 2, band, 2 * band), F32),
                        pltpu.VMEM((n_t, ATTN_RESIDUE_PARTS, sb_len // ATTN_RESIDUE_PARTS, LANES), F32),
                        pltpu.VMEM((n_t, 2, ATTN_RESIDUE_PARTS, sb_len // ATTN_RESIDUE_PARTS, LANES), F32),
                        pltpu.VMEM((n_t, 2, ATTN_RESIDUE_PARTS, sb_len // ATTN_RESIDUE_PARTS, LANES), F32),
                        pltpu.VMEM((n_t, band, LANES), F32), pltpu.VMEM((n_t, band, LANES), F32)],
        compiler_params=_cparams(3),
        name="dilated_attn",
    )(qkv, qkv, qkv, brow)


def _mlstm_front_kernel(x_ref, halo_ref, cw_ref, cb_ref, wblk_ref, wif_ref, bif_ref,
                        q_ref, k_ref, v_ref, xc_ref, gate_ref, xe_ref, wbd_ref, *, tiles_per_seq, k_scale, blk):
    tm, width = x_ref.shape
    taps = cw_ref.shape[0]
    pad = halo_ref.shape[0]

    @pl.when(pl.program_id(0) == 0)
    def _():
        r_id = lax.broadcasted_iota(jnp.int32, (MXU_DIM, MXU_DIM), 0)
        c_id = lax.broadcasted_iota(jnp.int32, (MXU_DIM, MXU_DIM), 1)
        same_block = (r_id // blk) == (c_id // blk)
        o_id = lax.broadcasted_iota(jnp.int32, (LANES, MXU_DIM), 0)
        spread = (o_id == lax.broadcasted_iota(jnp.int32, (LANES, MXU_DIM), 1) % blk).astype(BF16)
        for kind in range(wblk_ref.shape[0]):
            for j in range(width // MXU_DIM):
                rows = wblk_ref[kind, MXU_DIM * j:MXU_DIM * (j + 1), :].astype(BF16)
                tile = jnp.dot(rows, spread, preferred_element_type=F32)
                wbd_ref[kind, j] = jnp.where(same_block, tile, 0.0).astype(BF16)

    first = (pl.program_id(0) % tiles_per_seq) == 0
    halo = halo_ref[...].astype(F32)
    xe_ref[0:pad, :] = jnp.where(first, jnp.zeros_like(halo), halo)
    xe_ref[pad:pad + tm, :] = x_ref[...].astype(F32)
    n_tiles = width // MXU_DIM
    for r0 in range(0, tm, FRONT_ROW_CHUNK):
        rs = slice(r0, r0 + FRONT_ROW_CHUNK)
        xmb = x_ref[rs, :]
        y = cb_ref[...]
        for back in range(taps):
            y = y + (xe_ref[pad + r0 - back:pad + r0 - back + FRONT_ROW_CHUNK, :]
                     * cw_ref[taps - 1 - back:taps - back, :])
        xcb = _silu(y).astype(BF16)
        xc_ref[rs, :] = xcb
        qbs, kbs, vbs = [], [], []
        for j in range(n_tiles):
            sl = slice(MXU_DIM * j, MXU_DIM * (j + 1))
            qj = jnp.dot(xcb[:, sl], wbd_ref[0, j], preferred_element_type=F32)
            kj = jnp.dot(xcb[:, sl], wbd_ref[1, j], preferred_element_type=F32)
            vj = jnp.dot(xmb[:, sl], wbd_ref[2, j], preferred_element_type=F32)
            qbs.append(qj.astype(BF16))
            kbs.append(kj.astype(BF16))
            vbs.append(vj.astype(BF16))
            k_ref[rs, sl] = (kj * k_scale).astype(BF16)
        qb, kb, vb = (jnp.concatenate(t, axis=1) for t in (qbs, kbs, vbs))
        q_ref[rs, :] = qb
        v_ref[rs, :] = vb
        gate_ref[rs, :] = (bif_ref[...]
                           + jnp.dot(qb, wif_ref[0:width, :], preferred_element_type=F32)
                           + jnp.dot(kb, wif_ref[width:2 * width, :], preferred_element_type=F32)
                           + jnp.dot(vb, wif_ref[2 * width:3 * width, :], preferred_element_type=F32))


def _mlstm_front(rest2, conv_w, conv_b, w_blocks, wif_pad, bif_pad, s_len, width, tm):
    n = rest2.shape[0]
    halo = 16
    n_kinds, n_blocks, blk, _ = w_blocks.shape
    assert conv_w.shape[0] - 1 <= halo and n_blocks * blk == width and MXU_DIM % blk == 0
    wblk = jnp.pad(w_blocks.reshape(n_kinds, width, blk), ((0, 0), (0, 0), (0, LANES - blk)))
    const = lambda *shape: pl.BlockSpec(shape, lambda i: (0,) * len(shape))
    tok = lambda w: pl.BlockSpec((tm, w), lambda i: (i, 0))
    return pl.pallas_call(
        functools.partial(_mlstm_front_kernel, tiles_per_seq=s_len // tm,
                          k_scale=(width // M_HEADS) ** -0.5, blk=blk),
        grid=(n // tm,),
        in_specs=[tok(width),
                  pl.BlockSpec((halo, width), lambda i: (jnp.maximum(i * (tm // halo) - 1, 0), 0)),
                  const(*conv_w.shape), const(1, width),
                  const(*wblk.shape), const(*wif_pad.shape), const(1, LANES)],
        out_specs=[tok(width), tok(width), tok(width), tok(width), tok(LANES)],
        out_shape=[jax.ShapeDtypeStruct((n, width), BF16)] * 4 + [jax.ShapeDtypeStruct((n, LANES), F32)],
        scratch_shapes=[pltpu.VMEM((tm + halo, width), F32),
                        pltpu.VMEM((n_kinds, width // MXU_DIM, MXU_DIM, MXU_DIM), BF16)],
        compiler_params=_cparams(1),
        name="mlstm_front",
    )(rest2, rest2, conv_w, conv_b, wblk, wif_pad, bif_pad)


def _split3(x):
    hi = x.astype(BF16)
    r1 = x - hi.astype(F32)
    mid = r1.astype(BF16)
    lo = (r1 - mid.astype(F32)).astype(BF16)
    return hi, mid, lo


def _mlstm_core_kernel(q_ref, k_ref, v_ref, grow_ref, om_ref, zm_ref, xc_ref, hg_ref, skip_ref,
                       y_ref, c_ref, n_ref, m_ref):
    @pl.when(pl.program_id(2) == 0)
    def _():
        c_ref[...] = jnp.zeros_like(c_ref)
        n_ref[...] = jnp.zeros_like(n_ref)
        m_ref[...] = jnp.zeros_like(m_ref)

    lc = q_ref.shape[1]
    row_id = lax.broadcasted_iota(jnp.int32, (lc, lc), 0)
    col_id = lax.broadcasted_iota(jnp.int32, (lc, lc), 1)
    causal = row_id >= col_id
    upper = (row_id <= col_id).astype(BF16)
    gate_row = lax.broadcasted_iota(jnp.int32, (GATE_ROWS, lc), 0)
    for s in range(q_ref.shape[0]):
        q, k, v = q_ref[s], k_ref[s], v_ref[s]
        pre = grow_ref[s, 0]
        lf = jnp.minimum(pre, 0.0) - jnp.log1p(jnp.exp(-jnp.abs(pre)))
        csum = sum(jnp.dot(part, upper, preferred_element_type=F32) for part in _split3(lf))
        grow = jnp.where(gate_row == 0, pre, csum)
        gcol = grow.T
        i_row, b_row = grow[0:1, :], grow[1:2, :]
        i_col, b_col = gcol[:, 0:1], gcol[:, 1:2]
        m_prev = m_ref[s]
        g = b_row[:, lc - 1:lc]

        dmat = jnp.where(causal, b_col - b_row + i_row, MASKED)
        inter = b_col + m_prev
        m_t = jnp.maximum(inter, jnp.max(dmat, axis=1, keepdims=True))
        qk = lax.dot_general(q, k, (((1,), (1,)), ((), ())), preferred_element_type=F32) * jnp.exp(dmat - m_t)
        w_inter = jnp.exp(inter - m_t)
        c_old = c_ref[s]
        num = (w_inter * jnp.dot(q, c_old.astype(BF16), preferred_element_type=F32)
               + jnp.dot(qk.astype(BF16), v, preferred_element_type=F32))
        den = (w_inter * jnp.sum(q.astype(F32) * n_ref[s], axis=1, keepdims=True)
               + jnp.sum(qk, axis=1, keepdims=True))
        h = num / jnp.maximum(jnp.abs(den), jnp.exp(-m_t))

        m_new = jnp.maximum(g + m_prev, jnp.max(g - b_row + i_row, axis=1, keepdims=True))
        w_s = jnp.exp(g - b_col + i_col - m_new)
        decay = jnp.exp(g + m_prev - m_new)
        kw = (k.astype(F32) * w_s).astype(BF16)
        c_ref[s] = decay * c_old + lax.dot_general(kw, v, (((0,), (0,)), ((), ())), preferred_element_type=F32)
        n_ref[s] = decay * n_ref[s] + jnp.dot(jnp.ones((GATE_ROWS, lc), BF16), kw,
                                              preferred_element_type=F32)[0:1]
        m_ref[s] = m_new

        hgated = _sigmoid(om_ref[s].astype(F32)) * h
        mu = jnp.mean(hgated, axis=1, keepdims=True)
        cen = hgated - mu
        var = jnp.mean(cen * cen, axis=1, keepdims=True)
        hn = cen * lax.rsqrt(var + EPS) * hg_ref[...]
        zm = zm_ref[s].astype(F32)
        y_ref[s] = ((hn + skip_ref[...] * xc_ref[s].astype(F32)) * _silu(zm)).astype(y_ref.dtype)


def _mlstm_core(q, k, v, grow, rest3, xc, head_norm_g, skip, lc, om_col0, zm_col0):
    b_sz, s_len, width = q.shape
    dh = width // M_HEADS
    grp = MLSTM_GROUP
    assert b_sz % grp == 0
    seq = lambda col0: pl.BlockSpec((grp, lc, dh), lambda b, h, c: (b, c, col0 + h))
    vec = pl.BlockSpec((1, dh), lambda b, h, c: (0, h))
    return pl.pallas_call(
        _mlstm_core_kernel,
        grid=(b_sz // grp, M_HEADS, s_len // lc),
        in_specs=[seq(0), seq(0), seq(0),
                  pl.BlockSpec((grp, 1, GATE_ROWS, lc), lambda b, h, c: (b, h, 0, c)),
                  seq(om_col0 // dh), seq(zm_col0 // dh), seq(0), vec, vec],
        out_specs=seq(0),
        out_shape=jax.ShapeDtypeStruct((b_sz, s_len, width), BF16),
        scratch_shapes=[pltpu.VMEM((grp, dh, dh), F32), pltpu.VMEM((grp, 1, dh), F32),
                        pltpu.VMEM((grp, 1, 1), F32)],
        compiler_params=_cparams(3),
        name="mlstm_core",
    )(q, k, v, grow, rest3, rest3, xc, head_norm_g, skip)


def _merge_out_kernel(ya_ref, za_ref, gates_ref, gb_ref, ym_ref, x_ref, wpa_ref, wpb_ref, wout_ref, gout_ref,
                      out_ref):
    d_model = x_ref.shape[1]
    za = za_ref[...].astype(F32)
    ya = jnp.dot((ya_ref[...].astype(F32) * _silu(za)).astype(BF16), wpa_ref[...],
                 preferred_element_type=F32)
    ym = jnp.dot(ym_ref[...], wpb_ref[...], preferred_element_type=F32)
    gate = _sigmoid(gates_ref[...].astype(F32) + gb_ref[...])
    merged = gate[:, :d_model] * ya + gate[:, d_model:] * ym
    hres = x_ref[...] + jnp.dot(merged.astype(BF16), wout_ref[...], preferred_element_type=F32)
    ms = jnp.mean(hres * hres, axis=-1, keepdims=True)
    out_ref[...] = hres * lax.rsqrt(ms + EPS) * gout_ref[...]


def _merge_out(ya2, proj2, gate_b, ym2, x2, wpa, wpb, wout, gout, za_col0, gates_col0, tm):
    n, d_model = x2.shape
    aw = ya2.shape[1]
    mw = ym2.shape[1]
    tok = lambda w, cb=0: pl.BlockSpec((tm, w), lambda i: (i, cb))
    const = lambda *shape: pl.BlockSpec(shape, lambda i: (0,) * len(shape))
    return pl.pallas_call(
        _merge_out_kernel,
        grid=(n // tm,),
        in_specs=[tok(aw), tok(aw, za_col0 // aw), tok(2 * d_model, gates_col0 // (2 * d_model)),
                  const(1, 2 * d_model), tok(mw), tok(d_model), const(aw, d_model), const(mw, d_model),
                  const(d_model, d_model), const(1, d_model)],
        out_specs=tok(d_model),
        out_shape=jax.ShapeDtypeStruct((n, d_model), F32),
        compiler_params=_cparams(1),
        name="merge_out",
    )(ya2, proj2, proj2, gate_b, ym2, x2, wpa, wpb, wout, gout)


def _layer(h2, b_sz, s_len, norm_in_g, w_in, gate_b, conv_w, conv_b, wq_m, wk_m, wv_m, w_if, b_if,
           head_norm_g, skip_m, w_pa, w_pb, w_out, rel_bias, gout):
    n, d_model = h2.shape
    aw = w_pa.shape[0]
    mw = w_pb.shape[0]
    assert aw == A_HEADS * A_HEAD_DIM and aw == d_model and mw == 2 * d_model
    wb = w_in.astype(BF16)
    xm_col0, zm_col0, om_col0, gates_col0, za_col0 = 0, mw, 2 * mw, 3 * mw, 3 * mw + 2 * d_model
    n_rest = za_col0 + aw
    tn = T_INPROJ_COLS
    assert aw == tn
    rest_first = 4 * aw // tn
    rest_tiles_before_za = (n_rest - aw) // tn
    n_rest_tiles = n_rest // tn
    za_block = 3 * aw // tn

    def w_col_block(j):
        return jnp.where(j < rest_tiles_before_za, j + rest_first,
                         jnp.where(j < n_rest_tiles, za_block, j - n_rest_tiles))

    rest2, qkv = _norm_inproj(h2, norm_in_g.reshape(1, d_model), wb, w_col_block, n_rest, 3 * aw,
                              tm=T_INPROJ_ROWS, tn=tn)
    to3 = lambda t: t.reshape(b_sz, s_len, t.shape[-1])

    ya = _attention(qkv.reshape(3 * aw // LANES, b_sz, s_len, LANES), rel_bias, aw)

    assert xm_col0 == 0
    wif_pad = jnp.pad(w_if, ((0, 0), (0, LANES - w_if.shape[1]))).astype(BF16)
    bif_pad = jnp.pad(b_if, (0, LANES - b_if.shape[0])).reshape(1, LANES)
    q, k, v, xc, gate_pre = _mlstm_front(
        rest2, conv_w, conv_b.reshape(1, mw), jnp.stack([wq_m, wk_m, wv_m]), wif_pad, bif_pad,
        s_len, mw, tm=T_FRONT_ROWS)
    grow = jnp.transpose(gate_pre[:, :2 * M_HEADS].reshape(b_sz, s_len, 2, M_HEADS), (0, 3, 2, 1))
    grow = jnp.pad(grow, ((0, 0), (0, 0), (0, GATE_ROWS - 2), (0, 0)))
    ym = _mlstm_core(to3(q), to3(k), to3(v), grow, to3(rest2), to3(xc),
                     head_norm_g.reshape(1, mw), skip_m.reshape(1, mw), MLSTM_CHUNK, om_col0, zm_col0)

    return _merge_out(ya.reshape(n, aw), rest2, gate_b.reshape(1, 2 * d_model), ym.reshape(n, mw), h2,
                      w_pa.astype(BF16), w_pb.astype(BF16), w_out.astype(BF16), gout.reshape(1, d_model),
                      za_col0, gates_col0, tm=T_MERGE_ROWS)


def kernel(x, norm_in_g, w_in, gate_b, conv_w, conv_b, wq_m, wk_m, wv_m, w_if, b_if, head_norm_g, skip_m,
           w_pa, w_pb, w_out, rel_bias, norm_out_g):
    b_sz, s_len, d_model = x.shape
    depth = w_in.shape[0]
    assert depth == 1
    out = _layer(x.reshape(b_sz * s_len, d_model), b_sz, s_len, norm_in_g[0], w_in[0], gate_b[0], conv_w[0],
                 conv_b[0], wq_m[0], wk_m[0], wv_m[0], w_if[0], b_if[0], head_norm_g[0], skip_m[0], w_pa[0],
                 w_pb[0], w_out[0], rel_bias, norm_out_g)
    return out.reshape(b_sz, s_len, d_model)
```

```python
import functools
import math

import jax
import jax.numpy as jnp
from jax import lax
from jax.experimental import pallas as pl
from jax.experimental.pallas import tpu as pltpu

F32 = jnp.float32
BF16 = jnp.bfloat16

A_HEADS = 16
A_HEAD_DIM = 64
DILATED_PATTERNS = ((128, 1), (512, 4), (2048, 16))
MAX_DISTANCE = 2048
M_HEADS = 4
EPS = 1e-6
MASKED = -1e30
LOG2E = math.log2(math.e)


def _sigmoid(x):
    return 0.5 * jnp.tanh(0.5 * x) + 0.5


def _silu(x):
    h = 0.5 * x
    return h * jnp.tanh(h) + h


LANES = 128
MXU_DIM = 256
VMEM_LIMIT_BYTES = 56 * 1024 * 1024

MLSTM_CHUNK = 256
CORE_INTERLEAVE = 4
MLSTM_GROUP = 8
GATE_ROWS = 8
ATTN_PAIRS_PER_STEP = 2
ATTN_RESIDUE_PARTS = 4
ATTN_BLOCKS_PER_BODY = 32

T_INPROJ_ROWS = 2048
T_INPROJ_COLS = 1024
T_FRONT_ROWS = 512
FRONT_ROW_CHUNK = 256
T_MERGE_ROWS = 512


def _cparams(n_axes):
    return pltpu.CompilerParams(dimension_semantics=("arbitrary",) * n_axes,
                                vmem_limit_bytes=VMEM_LIMIT_BYTES)


def _norm_inproj_kernel(x_ref, g_ref, w_ref, rest_ref, slab_ref, xn_ref, *, n_rest_tiles):
    j = pl.program_id(1)

    @pl.when(j == 0)
    def _():
        xf = x_ref[...]
        ms = jnp.mean(xf * xf, axis=-1, keepdims=True)
        xn_ref[...] = (xf * lax.rsqrt(ms + EPS) * g_ref[...]).astype(BF16)

    @pl.when(j < n_rest_tiles)
    def _():
        rest_ref[...] = jnp.dot(xn_ref[...], w_ref[...], preferred_element_type=F32).astype(rest_ref.dtype)

    @pl.when(j >= n_rest_tiles)
    def _():
        o = jnp.dot(xn_ref[...], w_ref[...], preferred_element_type=F32)
        for c in range(slab_ref.shape[0]):
            slab_ref[c] = o[:, LANES * c:LANES * (c + 1)]


def _norm_inproj(x2, g, w, w_col_block, n_rest, n_slab_cols, tm, tn):
    n, d = x2.shape
    n_rest_tiles = n_rest // tn
    n_slab_tiles = n_slab_cols // tn
    return pl.pallas_call(
        functools.partial(_norm_inproj_kernel, n_rest_tiles=n_rest_tiles),
        grid=(n // tm, n_rest_tiles + n_slab_tiles),
        in_specs=[pl.BlockSpec((tm, d), lambda i, j: (i, 0)),
                  pl.BlockSpec((1, d), lambda i, j: (0, 0)),
                  pl.BlockSpec((d, tn), lambda i, j: (0, w_col_block(j)))],
        out_specs=[pl.BlockSpec((tm, tn), lambda i, j: (i, jnp.minimum(j, n_rest_tiles - 1))),
                   pl.BlockSpec((tn // LANES, tm, LANES), lambda i, j: (jnp.maximum(j - n_rest_tiles, 0), i, 0))],
        out_shape=[jax.ShapeDtypeStruct((n, n_rest), BF16),
                   jax.ShapeDtypeStruct((n_slab_cols // LANES, n, LANES), F32)],
        scratch_shapes=[pltpu.VMEM((tm, d), BF16)],
        compiler_params=_cparams(2),
        name="norm_inproj",
    )(x2, g, w)


def _t5_bucket(dist, n_buckets):
    max_exact = n_buckets // 2
    large = max_exact + (jnp.log(jnp.maximum(dist, max_exact).astype(F32) / max_exact)
                         / math.log(MAX_DISTANCE / max_exact) * (n_buckets - max_exact)).astype(jnp.int32)
    return jnp.where(dist < max_exact, dist, jnp.minimum(large, n_buckets - 1))


def _band_bias_rows(rel_bias, band, dilation):
    delta = jnp.arange(band + 1)
    vals = rel_bias.astype(F32)[_t5_bucket(delta * dilation, rel_bias.shape[0])] * LOG2E
    return jnp.concatenate([vals[::-1].T, jnp.full((rel_bias.shape[1], band - 1), MASKED, F32)], axis=1)


def _attn_kernel(q_ref, kc_ref, vc_ref, brow_ref, o_ref, po_ref, pl_ref, bias_ref,
                 qq_ref, kq_ref, vq_ref, kl_ref, vl_ref, *, patterns, scale):
    n_t, sb_len, _ = q_ref.shape
    first_sb = pl.program_id(2) == 0

    @pl.when(jnp.logical_and(pl.program_id(1) == 0, first_sb))
    def _():
        for t in range(n_t):
            for p, (window, d) in enumerate(patterns):
                band = window // d
                has_prev = lax.broadcasted_iota(jnp.int32, (band, 2 * band), 1) >= band
                for hh in range(2):
                    row = brow_ref[t, 2 * p + hh:2 * p + hh + 1, :]
                    table = pltpu.roll(jnp.broadcast_to(row, (band, 2 * band)), 0, 1, stride=1, stride_axis=0)
                    bias_ref[t, p, 1, hh] = table
                    bias_ref[t, p, 0, hh] = jnp.where(has_prev, table, MASKED)

    nq = ATTN_RESIDUE_PARTS
    sbq = sb_len // nq
    cur = pl.program_id(2) % 2
    prv = 1 - cur

    @pl.when((pl.program_id(0) == 0) & (pl.program_id(1) == 0) & first_sb)
    def _():
        kq_ref[...] = jnp.zeros_like(kq_ref)
        vq_ref[...] = jnp.zeros_like(vq_ref)
        kl_ref[...] = jnp.zeros_like(kl_ref)
        vl_ref[...] = jnp.zeros_like(vl_ref)

    for t in range(n_t):
        for c in range(nq):
            qq_ref[t, c] = q_ref[t, pl.ds(c, sbq, stride=nq), :]
            kq_ref[t, cur, c] = kc_ref[t, pl.ds(c, sbq, stride=nq), :]
            vq_ref[t, cur, c] = vc_ref[t, pl.ds(c, sbq, stride=nq), :]
    heads_per_tile = LANES // A_HEAD_DIM
    assert heads_per_tile == 2

    for p, (window, d) in enumerate(patterns):
        band = window // d
        blocks_per_residue = sb_len // window
        first_head = lax.broadcasted_iota(jnp.int32, (band, LANES), 1) < A_HEAD_DIM
        first_head_keys = lax.broadcasted_iota(jnp.int32, (2 * band, LANES), 1) < A_HEAD_DIM

        def rows(start, n, d=d):
            return pl.ds(start, n, stride=d) if d > 1 else pl.ds(start, n)

        def block(i, carry, p=p, d=d, band=band, blocks_per_residue=blocks_per_residue, first_head=first_head,
                  first_head_keys=first_head_keys, rows=rows, nq=nq, sbq=sbq):
            r = i // blocks_per_residue
            jb = i % blocks_per_residue
            start = r + jb * (band * d)
            variant = jnp.where(jnp.logical_and(first_sb, jb == 0), 0, 1)
            for t in range(n_t):
                if d % nq == 0:
                    e = d // nq
                    part, first_row = r % nq, r // nq + jb * (band * e)
                    rows_e = lambda a, n: pl.ds(a, n, stride=e) if e > 1 else pl.ds(a, n)
                    qf = qq_ref[t, part, rows_e(first_row, band), :]
                    carried = rows_e(sbq - band * e + r // nq, band)
                    k_before, v_before = kq_ref[t, prv, part, carried, :], vq_ref[t, prv, part, carried, :]
                    if blocks_per_residue > 1:
                        inside = rows_e(jnp.maximum(first_row - band * e, 0), band)
                        k_before = jnp.where(jb == 0, k_before, kq_ref[t, cur, part, inside, :])
                        v_before = jnp.where(jb == 0, v_before, vq_ref[t, cur, part, inside, :])
                    kf = jnp.concatenate([k_before, kq_ref[t, cur, part, rows_e(first_row, band), :]], axis=0)
                    vf = jnp.concatenate([v_before, vq_ref[t, cur, part, rows_e(first_row, band), :]], axis=0)
                else:
                    assert d == 1
                    qf = q_ref[t, pl.ds(start, band), :]
                    inside = pl.ds(jnp.maximum(start - band, 0), band)
                    kf = jnp.concatenate([jnp.where(jb == 0, kl_ref[t], kc_ref[t, inside, :]),
                                          kc_ref[t, pl.ds(start, band), :]], axis=0)
                    vf = jnp.concatenate([jnp.where(jb == 0, vl_ref[t], vc_ref[t, inside, :]),
                                          vc_ref[t, pl.ds(start, band), :]], axis=0)
                q2 = (qf * (scale * LOG2E)).astype(BF16)
                k2 = kf.astype(BF16)
                v2 = vf.astype(BF16)
                zeros = jnp.zeros_like(v2)
                es, ms = [], []
                for hh in range(heads_per_tile):
                    mine = first_head if hh == 0 else jnp.logical_not(first_head)
                    qm = jnp.where(mine, q2, jnp.zeros_like(q2))
                    s = lax.dot_general(qm, k2, (((1,), (1,)), ((), ())), preferred_element_type=F32)
                    s = s + bias_ref[t, p, variant, hh]
                    m = jnp.max(s, axis=1, keepdims=True)
                    es.append(jnp.exp2(s - m).astype(BF16))
                    ms.append(m)
                own0 = first_head_keys.astype(BF16)
                rhs = jnp.concatenate([jnp.concatenate([jnp.where(first_head_keys, v2, zeros), own0], axis=1),
                                       jnp.concatenate([jnp.where(first_head_keys, zeros, v2), 1 - own0], axis=1)],
                                      axis=0)
                both = jnp.dot(jnp.concatenate(es, axis=1), rhs, preferred_element_type=F32)
                num, den = both[:, :LANES], both[:, LANES:]
                po_ref[t, p, rows(start, band), :] = num / den
                pl_ref[t, p, rows(start, band), :] = jnp.where(first_head, ms[0], ms[1]) + jnp.log(den) * LOG2E
            return carry

        lax.fori_loop(0, sb_len // band, block, 0, unroll=ATTN_BLOCKS_PER_BODY // n_t)

    chunk = 256
    for t in range(n_t):
        for c in range(sb_len // chunk):
            sl = slice(chunk * c, chunk * (c + 1))
            lse = [pl_ref[t, p, sl, :] for p in range(len(patterns))]
            top = functools.reduce(jnp.maximum, lse)
            wgt = [jnp.exp2(l - top) for l in lse]
            num = sum(w * po_ref[t, p, sl, :] for p, w in enumerate(wgt))
            o_ref[0, sl, LANES * t:LANES * (t + 1)] = (num / sum(wgt)).astype(o_ref.dtype)

    last = patterns[0][0] // patterns[0][1]
    kl_ref[...] = kc_ref[:, sb_len - last:sb_len, :]
    vl_ref[...] = vc_ref[:, sb_len - last:sb_len, :]


def _attention(qkv, rel_bias, aw):
    _, b_sz, s_len, _ = qkv.shape
    sb_len = max(w for w, _ in DILATED_PATTERNS)
    band = DILATED_PATTERNS[0][0] // DILATED_PATTERNS[0][1]
    assert all(w // d == band and sb_len % w == 0 for w, d in DILATED_PATTERNS)
    assert s_len % sb_len == 0 and band % LANES == 0
    n_pairs = aw // LANES
    n_pat = len(DILATED_PATTERNS)
    n_t = ATTN_PAIRS_PER_STEP
    assert n_pairs % n_t == 0 and ATTN_BLOCKS_PER_BODY % n_t == 0
    steps_j = n_pairs // n_t
    brow = jnp.stack([_band_bias_rows(rel_bias, band, d) for _, d in DILATED_PATTERNS])
    brow = brow.reshape(n_pat, n_pairs, 2, 2 * band).transpose(1, 0, 2, 3).reshape(n_pairs, n_pat * 2, 2 * band)

    def blk(which):
        return pl.BlockSpec((n_t, None, sb_len, LANES), lambda j, b, s: (which * steps_j + j, b, s, 0))

    return pl.pallas_call(
        functools.partial(_attn_kernel, patterns=DILATED_PATTERNS, scale=A_HEAD_DIM ** -0.5),
        grid=(steps_j, b_sz, s_len // sb_len),
        in_specs=[blk(0), blk(1), blk(2),
                  pl.BlockSpec((n_t, n_pat * 2, 2 * band), lambda j, b, s: (j, 0, 0))],
        out_specs=pl.BlockSpec((1, sb_len, n_t * LANES), lambda j, b, s: (b, s, j)),
        out_shape=jax.ShapeDtypeStruct((b_sz, s_len, aw), BF16),
        scratch_shapes=[pltpu.VMEM((n_t, n_pat, sb_len, LANES), F32), pltpu.VMEM((n_t, n_pat, sb_len, LANES), F32),
                        pltpu.VMEM((n_t, n_pat, 2, 2, band, 2 * band), F32),
                        pltpu.VMEM((n_t, ATTN_RESIDUE_PARTS, sb_len // ATTN_RESIDUE_PARTS, LANES), F32),
                        pltpu.VMEM((n_t, 2, ATTN_RESIDUE_PARTS, sb_len // ATTN_RESIDUE_PARTS, LANES), F32),
                        pltpu.VMEM((n_t, 2, ATTN_RESIDUE_PARTS, sb_len // ATTN_RESIDUE_PARTS, LANES), F32),
                        pltpu.VMEM((n_t, band, LANES), F32), pltpu.VMEM((n_t, band, LANES), F32)],
        compiler_params=_cparams(3),
        name="dilated_attn",
    )(qkv, qkv, qkv, brow)


def _mlstm_front_kernel(x_ref, halo_ref, cw_ref, cb_ref, wblk_ref, wif_ref, bif_ref,
                        q_ref, k_ref, v_ref, xc_ref, gate_ref, xe_ref, wbd_ref, *, tiles_per_seq, k_scale, blk):
    tm, width = x_ref.shape
    taps = cw_ref.shape[0]
    pad = halo_ref.shape[0]

    @pl.when(pl.program_id(0) == 0)
    def _():
        r_id = lax.broadcasted_iota(jnp.int32, (MXU_DIM, MXU_DIM), 0)
        c_id = lax.broadcasted_iota(jnp.int32, (MXU_DIM, MXU_DIM), 1)
        same_block = (r_id // blk) == (c_id // blk)
        o_id = lax.broadcasted_iota(jnp.int32, (LANES, MXU_DIM), 0)
        spread = (o_id == lax.broadcasted_iota(jnp.int32, (LANES, MXU_DIM), 1) % blk).astype(BF16)
        for kind in range(wblk_ref.shape[0]):
            for j in range(width // MXU_DIM):
                rows = wblk_ref[kind, MXU_DIM * j:MXU_DIM * (j + 1), :].astype(BF16)
                tile = jnp.dot(rows, spread, preferred_element_type=F32)
                wbd_ref[kind, j] = jnp.where(same_block, tile, 0.0).astype(BF16)

    first = (pl.program_id(0) % tiles_per_seq) == 0
    halo = halo_ref[...].astype(F32)
    xe_ref[0:pad, :] = jnp.where(first, jnp.zeros_like(halo), halo)
    xe_ref[pad:pad + tm, :] = x_ref[...].astype(F32)
    n_tiles = width // MXU_DIM
    for r0 in range(0, tm, FRONT_ROW_CHUNK):
        rs = slice(r0, r0 + FRONT_ROW_CHUNK)
        xmb = x_ref[rs, :]
        y = cb_ref[...]
        for back in range(taps):
            y = y + (xe_ref[pad + r0 - back:pad + r0 - back + FRONT_ROW_CHUNK, :]
                     * cw_ref[taps - 1 - back:taps - back, :])
        xcb = _silu(y).astype(BF16)
        xc_ref[rs, :] = xcb
        qbs, kbs, vbs = [], [], []
        for j in range(n_tiles):
            sl = slice(MXU_DIM * j, MXU_DIM * (j + 1))
            qj = jnp.dot(xcb[:, sl], wbd_ref[0, j], preferred_element_type=F32)
            kj = jnp.dot(xcb[:, sl], wbd_ref[1, j], preferred_element_type=F32)
            vj = jnp.dot(xmb[:, sl], wbd_ref[2, j], preferred_element_type=F32)
            qbs.append(qj.astype(BF16))
            kbs.append(kj.astype(BF16))
            vbs.append(vj.astype(BF16))
            k_ref[rs, sl] = (kj * k_scale).astype(BF16)
        qb, kb, vb = (jnp.concatenate(t, axis=1) for t in (qbs, kbs, vbs))
        q_ref[rs, :] = qb
        v_ref[rs, :] = vb
        gate_ref[rs, :] = (bif_ref[...]
                           + jnp.dot(qb, wif_ref[0:width, :], preferred_element_type=F32)
                           + jnp.dot(kb, wif_ref[width:2 * width, :], preferred_element_type=F32)
                           + jnp.dot(vb, wif_ref[2 * width:3 * width, :], preferred_element_type=F32))


def _mlstm_front(rest2, conv_w, conv_b, w_blocks, wif_pad, bif_pad, s_len, width, tm):
    n = rest2.shape[0]
    halo = 16
    n_kinds, n_blocks, blk, _ = w_blocks.shape
    assert conv_w.shape[0] - 1 <= halo and n_blocks * blk == width and MXU_DIM % blk == 0
    wblk = jnp.pad(w_blocks.reshape(n_kinds, width, blk), ((0, 0), (0, 0), (0, LANES - blk)))
    const = lambda *shape: pl.BlockSpec(shape, lambda i: (0,) * len(shape))
    tok = lambda w: pl.BlockSpec((tm, w), lambda i: (i, 0))
    return pl.pallas_call(
        functools.partial(_mlstm_front_kernel, tiles_per_seq=s_len // tm,
                          k_scale=(width // M_HEADS) ** -0.5, blk=blk),
        grid=(n // tm,),
        in_specs=[tok(width),
                  pl.BlockSpec((halo, width), lambda i: (jnp.maximum(i * (tm // halo) - 1, 0), 0)),
                  const(*conv_w.shape), const(1, width),
                  const(*wblk.shape), const(*wif_pad.shape), const(1, LANES)],
        out_specs=[tok(width), tok(width), tok(width), tok(width), tok(LANES)],
        out_shape=[jax.ShapeDtypeStruct((n, width), BF16)] * 4 + [jax.ShapeDtypeStruct((n, LANES), F32)],
        scratch_shapes=[pltpu.VMEM((tm + halo, width), F32),
                        pltpu.VMEM((n_kinds, width // MXU_DIM, MXU_DIM, MXU_DIM), BF16)],
        compiler_params=_cparams(1),
        name="mlstm_front",
    )(rest2, rest2, conv_w, conv_b, wblk, wif_pad, bif_pad)


def _split3(x):
    hi = x.astype(BF16)
    r1 = x - hi.astype(F32)
    mid = r1.astype(BF16)
    lo = (r1 - mid.astype(F32)).astype(BF16)
    return hi, mid, lo


def _mlstm_core_kernel(q_ref, k_ref, v_ref, grow_ref, om_ref, zm_ref, xc_ref, hg_ref, skip_ref,
                       y_ref, c_ref, n_ref, m_ref):
    @pl.when(pl.program_id(2) == 0)
    def _():
        c_ref[...] = jnp.zeros_like(c_ref)
        n_ref[...] = jnp.zeros_like(n_ref)
        m_ref[...] = jnp.zeros_like(m_ref)

    lc = q_ref.shape[1]
    row_id = lax.broadcasted_iota(jnp.int32, (lc, lc), 0)
    col_id = lax.broadcasted_iota(jnp.int32, (lc, lc), 1)
    causal = row_id >= col_id
    upper = (row_id <= col_id).astype(BF16)
    gate_row = lax.broadcasted_iota(jnp.int32, (GATE_ROWS, lc), 0)
    def recurrence(s):
        q, k, v = q_ref[s], k_ref[s], v_ref[s]
        pre = grow_ref[s, 0]
        lf = jnp.minimum(pre, 0.0) - jnp.log1p(jnp.exp(-jnp.abs(pre)))
        csum = sum(jnp.dot(part, upper, preferred_element_type=F32) for part in _split3(lf))
        grow = jnp.where(gate_row == 0, pre, csum)
        gcol = grow.T
        i_row, b_row = grow[0:1, :], grow[1:2, :]
        i_col, b_col = gcol[:, 0:1], gcol[:, 1:2]
        m_prev = m_ref[s]
        g = b_row[:, lc - 1:lc]

        dmat = jnp.where(causal, b_col - b_row + i_row, MASKED)
        inter = b_col + m_prev
        m_t = jnp.maximum(inter, jnp.max(dmat, axis=1, keepdims=True))
        qk = lax.dot_general(q, k, (((1,), (1,)), ((), ())), preferred_element_type=F32) * jnp.exp(dmat - m_t)
        w_inter = jnp.exp(inter - m_t)
        c_old = c_ref[s]
        num = (w_inter * jnp.dot(q, c_old.astype(BF16), preferred_element_type=F32)
               + jnp.dot(qk.astype(BF16), v, preferred_element_type=F32))
        den = (w_inter * jnp.sum(q.astype(F32) * n_ref[s], axis=1, keepdims=True)
               + jnp.sum(qk, axis=1, keepdims=True))
        h = num / jnp.maximum(jnp.abs(den), jnp.exp(-m_t))

        m_new = jnp.maximum(g + m_prev, jnp.max(g - b_row + i_row, axis=1, keepdims=True))
        w_s = jnp.exp(g - b_col + i_col - m_new)
        decay = jnp.exp(g + m_prev - m_new)
        kw = (k.astype(F32) * w_s).astype(BF16)
        c_ref[s] = decay * c_old + lax.dot_general(kw, v, (((0,), (0,)), ((), ())), preferred_element_type=F32)
        n_ref[s] = decay * n_ref[s] + jnp.dot(jnp.ones((GATE_ROWS, lc), BF16), kw,
                                              preferred_element_type=F32)[0:1]
        m_ref[s] = m_new
        return h

    def epilogue(s, h):
        hgated = _sigmoid(om_ref[s].astype(F32)) * h
        mu = jnp.mean(hgated, axis=1, keepdims=True)
        cen = hgated - mu
        var = jnp.mean(cen * cen, axis=1, keepdims=True)
        hn = cen * lax.rsqrt(var + EPS) * hg_ref[...]
        zm = zm_ref[s].astype(F32)
        y_ref[s] = ((hn + skip_ref[...] * xc_ref[s].astype(F32)) * _silu(zm)).astype(y_ref.dtype)

    n_seq = q_ref.shape[0]
    for s0 in range(0, n_seq, CORE_INTERLEAVE):
        hs = [recurrence(s) for s in range(s0, s0 + CORE_INTERLEAVE)]
        for s, h in zip(range(s0, s0 + CORE_INTERLEAVE), hs):
            epilogue(s, h)


def _mlstm_core(q, k, v, grow, rest3, xc, head_norm_g, skip, lc, om_col0, zm_col0):
    b_sz, s_len, width = q.shape
    dh = width // M_HEADS
    grp = MLSTM_GROUP
    assert b_sz % grp == 0
    seq = lambda col0: pl.BlockSpec((grp, lc, dh), lambda b, h, c: (b, c, col0 + h))
    vec = pl.BlockSpec((1, dh), lambda b, h, c: (0, h))
    return pl.pallas_call(
        _mlstm_core_kernel,
        grid=(b_sz // grp, M_HEADS, s_len // lc),
        in_specs=[seq(0), seq(0), seq(0),
                  pl.BlockSpec((grp, 1, GATE_ROWS, lc), lambda b, h, c: (b, h, 0, c)),
                  seq(om_col0 // dh), seq(zm_col0 // dh), seq(0), vec, vec],
        out_specs=seq(0),
        out_shape=jax.ShapeDtypeStruct((b_sz, s_len, width), BF16),
        scratch_shapes=[pltpu.VMEM((grp, dh, dh), F32), pltpu.VMEM((grp, 1, dh), F32),
                        pltpu.VMEM((grp, 1, 1), F32)],
        compiler_params=_cparams(3),
        name="mlstm_core",
    )(q, k, v, grow, rest3, rest3, xc, head_norm_g, skip)


def _merge_out_kernel(ya_ref, za_ref, gates_ref, gb_ref, ym_ref, x_ref, wpa_ref, wpb_ref, wout_ref, gout_ref,
                      out_ref):
    d_model = x_ref.shape[1]
    za = za_ref[...].astype(F32)
    ya = jnp.dot((ya_ref[...].astype(F32) * _silu(za)).astype(BF16), wpa_ref[...],
                 preferred_element_type=F32)
    ym = jnp.dot(ym_ref[...], wpb_ref[...], preferred_element_type=F32)
    gate = _sigmoid(gates_ref[...].astype(F32) + gb_ref[...])
    merged = gate[:, :d_model] * ya + gate[:, d_model:] * ym
    hres = x_ref[...] + jnp.dot(merged.astype(BF16), wout_ref[...], preferred_element_type=F32)
    ms = jnp.mean(hres * hres, axis=-1, keepdims=True)
    out_ref[...] = hres * lax.rsqrt(ms + EPS) * gout_ref[...]


def _merge_out(ya2, proj2, gate_b, ym2, x2, wpa, wpb, wout, gout, za_col0, gates_col0, tm):
    n, d_model = x2.shape
    aw = ya2.shape[1]
    mw = ym2.shape[1]
    tok = lambda w, cb=0: pl.BlockSpec((tm, w), lambda i: (i, cb))
    const = lambda *shape: pl.BlockSpec(shape, lambda i: (0,) * len(shape))
    return pl.pallas_call(
        _merge_out_kernel,
        grid=(n // tm,),
        in_specs=[tok(aw), tok(aw, za_col0 // aw), tok(2 * d_model, gates_col0 // (2 * d_model)),
                  const(1, 2 * d_model), tok(mw), tok(d_model), const(aw, d_model), const(mw, d_model),
                  const(d_model, d_model), const(1, d_model)],
        out_specs=tok(d_model),
        out_shape=jax.ShapeDtypeStruct((n, d_model), F32),
        compiler_params=_cparams(1),
        name="merge_out",
    )(ya2, proj2, proj2, gate_b, ym2, x2, wpa, wpb, wout, gout)


def _layer(h2, b_sz, s_len, norm_in_g, w_in, gate_b, conv_w, conv_b, wq_m, wk_m, wv_m, w_if, b_if,
           head_norm_g, skip_m, w_pa, w_pb, w_out, rel_bias, gout):
    n, d_model = h2.shape
    aw = w_pa.shape[0]
    mw = w_pb.shape[0]
    assert aw == A_HEADS * A_HEAD_DIM and aw == d_model and mw == 2 * d_model
    wb = w_in.astype(BF16)
    xm_col0, zm_col0, om_col0, gates_col0, za_col0 = 0, mw, 2 * mw, 3 * mw, 3 * mw + 2 * d_model
    n_rest = za_col0 + aw
    tn = T_INPROJ_COLS
    assert aw == tn
    rest_first = 4 * aw // tn
    rest_tiles_before_za = (n_rest - aw) // tn
    n_rest_tiles = n_rest // tn
    za_block = 3 * aw // tn

    def w_col_block(j):
        return jnp.where(j < rest_tiles_before_za, j + rest_first,
                         jnp.where(j < n_rest_tiles, za_block, j - n_rest_tiles))

    rest2, qkv = _norm_inproj(h2, norm_in_g.reshape(1, d_model), wb, w_col_block, n_rest, 3 * aw,
                              tm=T_INPROJ_ROWS, tn=tn)
    to3 = lambda t: t.reshape(b_sz, s_len, t.shape[-1])

    ya = _attention(qkv.reshape(3 * aw // LANES, b_sz, s_len, LANES), rel_bias, aw)

    assert xm_col0 == 0
    wif_pad = jnp.pad(w_if, ((0, 0), (0, LANES - w_if.shape[1]))).astype(BF16)
    bif_pad = jnp.pad(b_if, (0, LANES - b_if.shape[0])).reshape(1, LANES)
    q, k, v, xc, gate_pre = _mlstm_front(
        rest2, conv_w, conv_b.reshape(1, mw), jnp.stack([wq_m, wk_m, wv_m]), wif_pad, bif_pad,
        s_len, mw, tm=T_FRONT_ROWS)
    grow = jnp.transpose(gate_pre[:, :2 * M_HEADS].reshape(b_sz, s_len, 2, M_HEADS), (0, 3, 2, 1))
    grow = jnp.pad(grow, ((0, 0), (0, 0), (0, GATE_ROWS - 2), (0, 0)))
    ym = _mlstm_core(to3(q), to3(k), to3(v), grow, to3(rest2), to3(xc),
                     head_norm_g.reshape(1, mw), skip_m.reshape(1, mw), MLSTM_CHUNK, om_col0, zm_col0)

    return _merge_out(ya.reshape(n, aw), rest2, gate_b.reshape(1, 2 * d_model), ym.reshape(n, mw), h2,
                      w_pa.astype(BF16), w_pb.astype(BF16), w_out.astype(BF16), gout.reshape(1, d_model),
                      za_col0, gates_col0, tm=T_MERGE_ROWS)


def kernel(x, norm_in_g, w_in, gate_b, conv_w, conv_b, wq_m, wk_m, wv_m, w_if, b_if, head_norm_g, skip_m,
           w_pa, w_pb, w_out, rel_bias, norm_out_g):
    b_sz, s_len, d_model = x.shape
    depth = w_in.shape[0]
    assert depth == 1
    out = _layer(x.reshape(b_sz * s_len, d_model), b_sz, s_len, norm_in_g[0], w_in[0], gate_b[0], conv_w[0],
                 conv_b[0], wq_m[0], wk_m[0], wv_m[0], w_if[0], b_if[0], head_norm_g[0], skip_m[0], w_pa[0],
                 w_pb[0], w_out[0], rel_bias, norm_out_g)
    return out.reshape(b_sz, s_len, d_model)
```
